```python
import jax, jax.numpy as jnp
from jax import lax
import numpy as np

D_MODEL = 1024
BATCH = 8
SEQ = 2048
DEPTH = 2
DEC_BATCH = 128
DEC_SEQ = 8
PAST_LEN = 16384
PAGE_SIZE = 128

D_A = D_MODEL // 2
A_GROUPS = 4
A_WIDTH = 31
D_B = D_MODEL // 2
B_GROUPS = 4
B_WIDTH = 3
D_C = D_MODEL // 2
C_HEADS = 4
C_HEAD_DIM = D_C // C_HEADS
CHUNK = 128
N_BRANCH = 3
D_FF = 7 * D_MODEL // 2
N_EXPERTS = 8
TOP_K = 2
N_DENSE = (DEPTH + 1) // 2
N_MOE = DEPTH // 2
EPS = 1e-6
IN_COLS = 2 * D_A + 3 * D_B + 2 * D_C + N_BRANCH * D_MODEL

kernel_name = "hybrid_conv_shortconv_chunkmlp_decoder_step"


def rms_norm(x, g):
    xf = x.astype(jnp.float32)
    y = xf * lax.rsqrt(jnp.mean(xf * xf, -1, keepdims=True) + EPS)
    return (y * g.astype(jnp.float32)).astype(x.dtype)


def layer_norm(x, g, b):
    xf = x.astype(jnp.float32)
    mu = jnp.mean(xf, -1, keepdims=True)
    var = jnp.mean(jnp.square(xf - mu), -1, keepdims=True)
    y = (xf - mu) * lax.rsqrt(var + EPS) * g.astype(jnp.float32) + b.astype(jnp.float32)
    return y.astype(x.dtype)


def dwconv_valid(xp, w):
    c = xp.shape[-1]
    return lax.conv_general_dilated(
        xp, w[:, None, :].astype(xp.dtype), window_strides=(1,), padding="VALID",
        dimension_numbers=("NWC", "WIO", "NWC"), feature_group_count=c)


def chunk_spatial(v, ws, bs):
    n, t, _ = v.shape
    n_chunks = -(-t // CHUNK)
    pad = n_chunks * CHUNK - t
    vp = jnp.pad(v, ((0, 0), (0, pad), (0, 0))).reshape(n, n_chunks, CHUNK, C_HEADS, C_HEAD_DIM)
    mask = jnp.tril(jnp.ones((CHUNK, CHUNK), dtype=bool))
    wm = jnp.where(mask[None], ws, 0).astype(v.dtype)
    s = jnp.einsum("hts,ncshd->ncthd", wm, vp) + bs.T.astype(v.dtype)[None, None, :, :, None]
    return s.reshape(n, n_chunks * CHUNK, D_C)[:, :t]


def mixer(h, past_a, past_b, w_in, conv_a_w, conv_a_b, ln_a_g, ln_a_b, conv_b_w,
          ln_c_g, ln_c_b, w_spatial, b_spatial, w_proj_a, w_proj_b, w_proj_c, w_out):
    n, t, _ = h.shape
    proj = h @ w_in
    cuts = list(np.cumsum([D_A, D_A, D_B, D_B, D_B, D_C, D_C]))
    a_val, a_gate, b_h, b_b, b_c, c_u, c_v, gates = jnp.split(proj, cuts, axis=-1)

    a = a_val * jax.nn.sigmoid(a_gate)
    a_full = jnp.concatenate([past_a.astype(a.dtype), a], axis=1)
    a_conv = dwconv_valid(a_full, conv_a_w) + conv_a_b.astype(a.dtype)
    a_out = jax.nn.silu(layer_norm(a_conv, ln_a_g, ln_a_b)) @ w_proj_a
    new_a = a_full[:, -(A_WIDTH - 1):]

    z = b_c * b_h
    z_full = jnp.concatenate([past_b.astype(z.dtype), z], axis=1)
    b_out = (b_b * dwconv_valid(z_full, conv_b_w)) @ w_proj_b
    new_b = z_full[:, -(B_WIDTH - 1):]

    v = layer_norm(c_v, ln_c_g, ln_c_b)
    c_out = (c_u * chunk_spatial(v, w_spatial, b_spatial)) @ w_proj_c
    last_start = ((t - 1) // CHUNK) * CHUNK
    new_v = v[:, last_start:]

    g = jax.nn.sigmoid(gates)
    g_a, g_b, g_c = jnp.split(g, N_BRANCH, axis=-1)
    merged = g_a * a_out + g_b * b_out + g_c * c_out
    return merged @ w_out, new_a, new_b, new_v


def swiglu(h, wg, wu, wd):
    return (jax.nn.silu(h @ wg) * (h @ wu)) @ wd


def moe(h, w_router, b_router, wg, wu, wd):
    n, t, d = h.shape
    ht = h.reshape(n * t, d)
    logits = (ht @ w_router).astype(jnp.float32) + b_router.astype(jnp.float32)
    top_val, top_idx = lax.top_k(logits, TOP_K)
    probs = jax.nn.softmax(top_val, axis=-1)
    combine = jnp.sum(jax.nn.one_hot(top_idx, N_EXPERTS, dtype=jnp.float32) * probs[..., None], axis=1)
    out = jnp.zeros_like(ht)
    for e in range(N_EXPERTS):
        out = out + combine[:, e:e + 1].astype(ht.dtype) * swiglu(ht, wg[e], wu[e], wd[e])
    return out.reshape(n, t, d)


def trunk(x, past_a, past_b, p):
    new_a, new_b, new_v = [], [], []
    for l in range(DEPTH):
        h = rms_norm(x, p["norm_mix_g"][l])
        m, na, nb, nv = mixer(
            h, past_a[l], past_b[l], p["w_in"][l], p["conv_a_w"][l], p["conv_a_b"][l],
            p["ln_a_g"][l], p["ln_a_b"][l], p["conv_b_w"][l], p["ln_c_g"][l], p["ln_c_b"][l],
            p["w_spatial"][l], p["b_spatial"][l], p["w_proj_a"][l], p["w_proj_b"][l],
            p["w_proj_c"][l], p["w_out"][l])
        x = x + m
        h = rms_norm(x, p["norm_ffn_g"][l])
        i = l // 2
        if l % 2 == 0:
            f = swiglu(h, p["ffn_w_gate"][i], p["ffn_w_up"][i], p["ffn_w_down"][i])
        else:
            f = moe(h, p["w_router"][i], p["b_router"][i], p["moe_w_gate"][i],
                    p["moe_w_up"][i], p["moe_w_down"][i])
        x = x + f
        new_a.append(na)
        new_b.append(nb)
        new_v.append(nv)
    y = rms_norm(x, p["norm_final_g"])
    return y, jnp.stack(new_a), jnp.stack(new_b), jnp.stack(new_v)


def setup_inputs(seed: int = 0) -> dict:
    key = jax.random.key(seed)
    ks = iter(jax.random.split(key, 40))
    nrm = lambda shape, scale: jax.random.normal(next(ks), shape, jnp.float32) * scale
    gain = lambda shape: 1.0 + nrm(shape, 0.05)
    d = D_MODEL
    return {
        "x_prompt": nrm((BATCH, SEQ, d), 1.0),
        "x_sample": nrm((DEC_BATCH, DEC_SEQ, d), 1.0),
        "state_conv_a": nrm((DEPTH, DEC_BATCH, A_WIDTH - 1, D_A), 0.5),
        "state_conv_b": nrm((DEPTH, DEC_BATCH, B_WIDTH - 1, D_B), 1.0),
        "norm_mix_g": gain((DEPTH, d)),
        "w_in": nrm((DEPTH, d, IN_COLS), d ** -0.5),
        "conv_a_w": nrm((DEPTH, A_WIDTH, D_A), A_WIDTH ** -0.5),
        "conv_a_b": nrm((DEPTH, D_A), 0.02),
        "ln_a_g": gain((DEPTH, D_A)),
        "ln_a_b": nrm((DEPTH, D_A), 0.02),
        "conv_b_w": nrm((DEPTH, B_WIDTH, D_B), B_WIDTH ** -0.5),
        "ln_c_g": gain((DEPTH, D_C)),
        "ln_c_b": nrm((DEPTH, D_C), 0.02),
        "w_spatial": nrm((DEPTH, C_HEADS, CHUNK, CHUNK), 0.5 * CHUNK ** -0.5),
        "b_spatial": 1.0 + nrm((DEPTH, C_HEADS, CHUNK), 0.1),
        "w_proj_a": nrm((DEPTH, D_A, d), D_A ** -0.5),
        "w_proj_b": nrm((DEPTH, D_B, d), D_B ** -0.5),
        "w_proj_c": nrm((DEPTH, D_C, d), D_C ** -0.5),
        "w_out": nrm((DEPTH, d, d), d ** -0.5),
        "norm_ffn_g": gain((DEPTH, d)),
        "ffn_w_gate": nrm((N_DENSE, d, D_FF), d ** -0.5),
        "ffn_w_up": nrm((N_DENSE, d, D_FF), d ** -0.5),
        "ffn_w_down": nrm((N_DENSE, D_FF, d), D_FF ** -0.5),
        "w_router": nrm((N_MOE, d, N_EXPERTS), d ** -0.5),
        "b_router": nrm((N_MOE, N_EXPERTS), 0.01),
        "moe_w_gate": nrm((N_MOE, N_EXPERTS, d, D_FF), d ** -0.5),
        "moe_w_up": nrm((N_MOE, N_EXPERTS, d, D_FF), d ** -0.5),
        "moe_w_down": nrm((N_MOE, N_EXPERTS, D_FF, d), D_FF ** -0.5),
        "norm_final_g": gain((d,)),
    }


def reference(x_prompt, x_sample, state_conv_a, state_conv_b, norm_mix_g, w_in, conv_a_w,
              conv_a_b, ln_a_g, ln_a_b, conv_b_w, ln_c_g, ln_c_b, w_spatial, b_spatial,
              w_proj_a, w_proj_b, w_proj_c, w_out, norm_ffn_g, ffn_w_gate, ffn_w_up,
              ffn_w_down, w_router, b_router, moe_w_gate, moe_w_up, moe_w_down, norm_final_g):
    p = dict(norm_mix_g=norm_mix_g, w_in=w_in, conv_a_w=conv_a_w, conv_a_b=conv_a_b,
             ln_a_g=ln_a_g, ln_a_b=ln_a_b, conv_b_w=conv_b_w, ln_c_g=ln_c_g, ln_c_b=ln_c_b,
             w_spatial=w_spatial, b_spatial=b_spatial, w_proj_a=w_proj_a, w_proj_b=w_proj_b,
             w_proj_c=w_proj_c, w_out=w_out, norm_ffn_g=norm_ffn_g, ffn_w_gate=ffn_w_gate,
             ffn_w_up=ffn_w_up, ffn_w_down=ffn_w_down, w_router=w_router, b_router=b_router,
             moe_w_gate=moe_w_gate, moe_w_up=moe_w_up, moe_w_down=moe_w_down,
             norm_final_g=norm_final_g)
    zeros_a = jnp.zeros((DEPTH, BATCH, A_WIDTH - 1, D_A), x_prompt.dtype)
    zeros_b = jnp.zeros((DEPTH, BATCH, B_WIDTH - 1, D_B), x_prompt.dtype)
    y_prompt, conv_a_p, conv_b_p, v_p = trunk(x_prompt, zeros_a, zeros_b, p)
    y_sample, conv_a_s, conv_b_s, v_s = trunk(x_sample, state_conv_a, state_conv_b, p)
    return (y_prompt, y_sample, conv_a_p, conv_a_s, conv_b_p, conv_b_s, v_p, v_s)
```

```python
import functools

import jax
import jax.numpy as jnp
from jax import lax
from jax.experimental import pallas as pl
from jax.experimental.pallas import tpu as pltpu

D_MODEL = 1024
D_BR = 512
A_WIDTH = 31
B_WIDTH = 3
C_HEADS = 4
CHUNK = 128
D_FF = 3584
N_EXPERTS = 8
EPS = 1e-6

A_HIST = 32
B_HIST = 8
LANES = 128
VMEM_LIMIT = 56 * 1024 * 1024

BF16 = jnp.bfloat16
F32 = jnp.float32


def _rms(x, g):
    return x * lax.rsqrt(jnp.mean(x * x, -1, keepdims=True) + EPS) * g


def _layer_norm(x, g, b):
    mu = jnp.mean(x, -1, keepdims=True)
    xc = x - mu
    var = jnp.mean(xc * xc, -1, keepdims=True)
    return xc * lax.rsqrt(var + EPS) * g + b


def _sigmoid(x):
    return 1.0 / (1.0 + jnp.exp(-x))


def _dot(a, b):
    return jnp.dot(a, b, preferred_element_type=F32)


def _mixer_kernel(x_ref, pa_ref, pb_ref, g_ref, win_ref, caw_ref, cab_ref, lag_ref, lab_ref,
                  cbw_ref, lcg_ref, lcb_ref, wsp_ref, bsp_ref, wpa_ref, wpb_ref, wpc_ref,
                  wout_ref, xo_ref, na_ref, nb_ref, nv_ref, fa, fz, sc, *, nb, tt, nv_rows):
    i = pl.program_id(1)
    rows = nb * tt

    @pl.when(i == 0)
    def _():
        fa[:, A_HIST - (A_WIDTH - 1):A_HIST, :] = pa_ref[...]
        fz[:, B_HIST - (B_WIDTH - 1):B_HIST, :] = pb_ref[...]

    x = x_ref[...].reshape(rows, D_MODEL)
    hb = _rms(x, g_ref[...]).astype(BF16)

    def proj(c0, c1):
        return _dot(hb, win_ref[:, c0:c1])

    a = proj(0, D_BR) * _sigmoid(proj(D_BR, 2 * D_BR))
    fa[:, A_HIST:A_HIST + tt, :] = a.reshape(nb, tt, D_BR)
    off_a = A_HIST - (A_WIDTH - 1)
    acc = jnp.zeros((nb, tt, D_BR), F32)
    for k in range(A_WIDTH):
        acc = acc + caw_ref[k:k + 1, :] * fa[:, off_a + k:off_a + k + tt, :]
    a_conv = acc.reshape(rows, D_BR) + cab_ref[...]
    a_ln = _layer_norm(a_conv, lag_ref[...], lab_ref[...])
    a_out = _dot((a_ln * _sigmoid(a_ln)).astype(BF16), wpa_ref[...])
    na_ref[...] = fa[:, tt + A_HIST - (A_WIDTH - 1):tt + A_HIST, :]
    fa[:, 0:A_HIST, :] = fa[:, tt:tt + A_HIST, :]

    c0 = 2 * D_BR
    z = proj(c0 + 2 * D_BR, c0 + 3 * D_BR) * proj(c0, c0 + D_BR)
    fz[:, B_HIST:B_HIST + tt, :] = z.reshape(nb, tt, D_BR)
    off_b = B_HIST - (B_WIDTH - 1)
    accb = jnp.zeros((nb, tt, D_BR), F32)
    for k in range(B_WIDTH):
        accb = accb + cbw_ref[k:k + 1, :] * fz[:, off_b + k:off_b + k + tt, :]
    b_out = _dot((proj(c0 + D_BR, c0 + 2 * D_BR) * accb.reshape(rows, D_BR)).astype(BF16),
                 wpb_ref[...])
    nb_ref[...] = fz[:, tt + B_HIST - (B_WIDTH - 1):tt + B_HIST, :]
    fz[:, 0:B_HIST, :] = fz[:, tt:tt + B_HIST, :]

    c0 = 5 * D_BR
    v = _layer_norm(proj(c0 + D_BR, c0 + 2 * D_BR), lcg_ref[...], lcb_ref[...])
    nv_ref[...] = v[rows - nv_rows * nb:, :].reshape(nb, nv_rows, D_BR)
    vb = v.astype(BF16)
    t_idx = lax.broadcasted_iota(jnp.int32, (CHUNK, CHUNK), 0)
    s_idx = lax.broadcasted_iota(jnp.int32, (CHUNK, CHUNK), 1)
    causal = s_idx <= t_idx
    for hd in range(C_HEADS):
        wm = jnp.where(causal, wsp_ref[hd], 0.0).astype(BF16)
        for c in range(rows // CHUNK):
            blk = (slice(c * CHUNK, (c + 1) * CHUNK), slice(hd * LANES, (hd + 1) * LANES))
            sc[blk] = _dot(wm, vb[blk])
    c_out = _dot((proj(c0, c0 + D_BR) * (sc[...].reshape(rows // CHUNK, CHUNK, D_BR)
                                        + bsp_ref[...]).reshape(rows, D_BR)).astype(BF16),
                 wpc_ref[...])

    c0 = 7 * D_BR
    merged = (_sigmoid(proj(c0, c0 + D_MODEL)) * a_out
              + _sigmoid(proj(c0 + D_MODEL, c0 + 2 * D_MODEL)) * b_out
              + _sigmoid(proj(c0 + 2 * D_MODEL, c0 + 3 * D_MODEL)) * c_out)
    xo_ref[...] = (x + _dot(merged.astype(BF16), wout_ref[...])).reshape(nb, tt, D_MODEL)


def _const_spec(shape):
    nd = len(shape)
    return pl.BlockSpec(shape, lambda b, i: (0,) * nd, pipeline_mode=pl.Buffered(1))


def _mixer(x, past_a, past_b, w, *, nb, tt):
    n, t, _ = x.shape
    assert n % nb == 0 and t % tt == 0 and (nb * tt) % CHUNK == 0
    nv_rows = t - ((t - 1) // CHUNK) * CHUNK
    assert nv_rows <= tt
    n_t = t // tt
    consts = [w["g"], w["w_in"], w["conv_a_w"], w["conv_a_b"], w["ln_a_g"], w["ln_a_b"],
              w["conv_b_w"], w["ln_c_g"], w["ln_c_b"], w["wsp"], w["bsp"], w["w_proj_a"],
              w["w_proj_b"], w["w_proj_c"], w["w_out"]]
    seq_spec = lambda r, c: pl.BlockSpec((nb, r, c), lambda b, i: (b, 0, 0))
    return pl.pallas_call(
        functools.partial(_mixer_kernel, nb=nb, tt=tt, nv_rows=nv_rows),
        grid=(n // nb, n_t),
        in_specs=[pl.BlockSpec((nb, tt, D_MODEL), lambda b, i: (b, i, 0)),
                  seq_spec(A_WIDTH - 1, D_BR), seq_spec(B_WIDTH - 1, D_BR)]
                 + [_const_spec(c.shape) for c in consts],
        out_specs=[pl.BlockSpec((nb, tt, D_MODEL), lambda b, i: (b, i, 0)),
                   seq_spec(A_WIDTH - 1, D_BR), seq_spec(B_WIDTH - 1, D_BR),
                   seq_spec(nv_rows, D_BR)],
        out_shape=[jax.ShapeDtypeStruct((n, t, D_MODEL), F32),
                   jax.ShapeDtypeStruct((n, A_WIDTH - 1, D_BR), F32),
                   jax.ShapeDtypeStruct((n, B_WIDTH - 1, D_BR), F32),
                   jax.ShapeDtypeStruct((n, nv_rows, D_BR), F32)],
        scratch_shapes=[pltpu.VMEM((nb, A_HIST + tt, D_BR), F32),
                        pltpu.VMEM((nb, B_HIST + tt, D_BR), F32),
                        pltpu.VMEM((nb * tt, D_BR), F32)],
        compiler_params=pltpu.CompilerParams(
            dimension_semantics=("arbitrary", "arbitrary"), vmem_limit_bytes=VMEM_LIMIT),
        name="mixer",
    )(x, past_a, past_b, *consts)


def _ffn_kernel(x_ref, g_ref, wr_ref, br_ref, wg_ref, wu_ref, wd_ref, gf_ref, o_ref,
                hb, acc, cw, *, routed, final):
    e = pl.program_id(1)
    j = pl.program_id(2)
    first = jnp.logical_and(e == 0, j == 0)
    last = jnp.logical_and(e == pl.num_programs(1) - 1, j == pl.num_programs(2) - 1)
    tm = x_ref.shape[0]

    @pl.when(first)
    def _():
        h = _rms(x_ref[...], g_ref[...])
        hb[...] = h.astype(BF16)
        acc[...] = jnp.zeros_like(acc)
        if routed:
            col = lax.broadcasted_iota(jnp.int32, (tm, LANES), 1)
            logits = jnp.dot(h, wr_ref[...], preferred_element_type=F32,
                             precision=lax.Precision.HIGHEST) + br_ref[...]
            logits = jnp.where(col < N_EXPERTS, logits, -jnp.inf)
            m1 = jnp.max(logits, -1, keepdims=True)
            i1 = jnp.min(jnp.where(logits == m1, col, LANES), -1, keepdims=True)
            rest = jnp.where(col == i1, -jnp.inf, logits)
            m2 = jnp.max(rest, -1, keepdims=True)
            i2 = jnp.min(jnp.where(rest == m2, col, LANES), -1, keepdims=True)
            e2 = jnp.exp(m2 - m1)
            p1 = 1.0 / (1.0 + e2)
            cw[...] = jnp.where(col == i1, p1, 0.0) + jnp.where(col == i2, e2 * p1, 0.0)

    h_b = hb[...]
    gate = _dot(h_b, wg_ref[...])
    act = gate * _sigmoid(gate) * _dot(h_b, wu_ref[...])
    if routed:
        col = lax.broadcasted_iota(jnp.int32, (tm, LANES), 1)
        act = act * jnp.sum(jnp.where(col == e, cw[...], 0.0), -1, keepdims=True)
    acc[...] += _dot(act.astype(BF16), wd_ref[...])

    @pl.when(last)
    def _():
        y = x_ref[...] + acc[...]
        if final:
            y = _rms(y, gf_ref[...])
        o_ref[...] = y


def _ffn(x, g, w_router, b_router, wg, wu, wd, g_final, *, routed, final, tm, tf):
    m, _ = x.shape
    n_e = wg.shape[0]
    assert m % tm == 0 and D_FF % tf == 0
    row = lambda i, e, j: (i, 0)
    const = lambda i, e, j: (0, 0)
    return pl.pallas_call(
        functools.partial(_ffn_kernel, routed=routed, final=final),
        grid=(m // tm, n_e, D_FF // tf),
        in_specs=[pl.BlockSpec((tm, D_MODEL), row),
                  pl.BlockSpec((1, D_MODEL), const),
                  pl.BlockSpec((D_MODEL, LANES), const),
                  pl.BlockSpec((1, LANES), const),
                  pl.BlockSpec((None, D_MODEL, tf), lambda i, e, j: (e, 0, j)),
                  pl.BlockSpec((None, D_MODEL, tf), lambda i, e, j: (e, 0, j)),
                  pl.BlockSpec((None, tf, D_MODEL), lambda i, e, j: (e, j, 0)),
                  pl.BlockSpec((1, D_MODEL), const)],
        out_specs=pl.BlockSpec((tm, D_MODEL), row),
        out_shape=jax.ShapeDtypeStruct((m, D_MODEL), F32),
        scratch_shapes=[pltpu.VMEM((tm, D_MODEL), BF16),
                        pltpu.VMEM((tm, D_MODEL), F32),
                        pltpu.VMEM((tm, LANES), F32)],
        compiler_params=pltpu.CompilerParams(
            dimension_semantics=("arbitrary", "arbitrary", "arbitrary"),
            vmem_limit_bytes=VMEM_LIMIT),
        name="ffn",
    )(x, g, w_router, b_router, wg, wu, wd, g_final)


def _row(v):
    return v.reshape(1, -1)


def kernel(x_prompt, x_sample, state_conv_a, state_conv_b, norm_mix_g, w_in, conv_a_w, conv_a_b, ln_a_g, ln_a_b, conv_b_w, ln_c_g, ln_c_b, w_spatial, b_spatial, w_proj_a, w_proj_b, w_proj_c, w_out, norm_ffn_g, ffn_w_gate, ffn_w_up, ffn_w_down, w_router, b_router, moe_w_gate, moe_w_up, moe_w_down, norm_final_g):
    depth = w_in.shape[0]
    n_p, t_p, _ = x_prompt.shape
    n_s, t_s, _ = x_sample.shape
    seqs_per_chunk = CHUNK // t_s

    zeros_a = jnp.zeros((n_p, A_WIDTH - 1, D_BR), F32)
    zeros_b = jnp.zeros((n_p, B_WIDTH - 1, D_BR), F32)
    dummy_router = jnp.zeros((D_MODEL, LANES), F32)
    dummy_bias = jnp.zeros((1, LANES), F32)

    xp, xs = x_prompt, x_sample
    outs = {k: [] for k in ("ap", "as", "bp", "bs", "vp", "vs")}
    for l in range(depth):
        bsp_p = jnp.repeat(b_spatial[l].T, LANES, axis=1)
        eye = jnp.eye(seqs_per_chunk, dtype=F32)
        wsp_s = jnp.einsum("ab,hts->hatbs", eye, w_spatial[l][:, :t_s, :t_s]).reshape(
            C_HEADS, CHUNK, CHUNK)
        bsp_s = jnp.tile(bsp_p[:t_s], (seqs_per_chunk, 1))
        w = dict(g=_row(norm_mix_g[l]), w_in=w_in[l].astype(BF16), conv_a_w=conv_a_w[l],
                 conv_a_b=_row(conv_a_b[l]), ln_a_g=_row(ln_a_g[l]), ln_a_b=_row(ln_a_b[l]),
                 conv_b_w=conv_b_w[l], ln_c_g=_row(ln_c_g[l]), ln_c_b=_row(ln_c_b[l]),
                 w_proj_a=w_proj_a[l].astype(BF16), w_proj_b=w_proj_b[l].astype(BF16),
                 w_proj_c=w_proj_c[l].astype(BF16), w_out=w_out[l].astype(BF16))
        xp, a_p, b_p, v_p = _mixer(xp, zeros_a, zeros_b, dict(w, wsp=w_spatial[l], bsp=bsp_p),
                                   nb=1, tt=512)
        xs, a_s, b_s, v_s = _mixer(xs, state_conv_a[l], state_conv_b[l],
                                   dict(w, wsp=wsp_s, bsp=bsp_s), nb=32, tt=t_s)
        for k, val in zip(("ap", "as", "bp", "bs", "vp", "vs"), (a_p, a_s, b_p, b_s, v_p, v_s)):
            outs[k].append(val)

        i = l // 2
        final = l == depth - 1
        if l % 2 == 0:
            args = (dummy_router, dummy_bias, ffn_w_gate[i][None].astype(BF16),
                    ffn_w_up[i][None].astype(BF16), ffn_w_down[i][None].astype(BF16))
            routed = False
        else:
            wr = jnp.pad(w_router[i], ((0, 0), (0, LANES - N_EXPERTS)))
            br = jnp.pad(_row(b_router[i]), ((0, 0), (0, LANES - N_EXPERTS)))
            args = (wr, br, moe_w_gate[i].astype(BF16), moe_w_up[i].astype(BF16),
                    moe_w_down[i].astype(BF16))
            routed = True
        ffn = functools.partial(_ffn, routed=routed, final=final, tm=512, tf=512)
        g_ffn = _row(norm_ffn_g[l])
        g_fin = _row(norm_final_g)
        xp = ffn(xp.reshape(n_p * t_p, D_MODEL), g_ffn, *args, g_fin).reshape(n_p, t_p, D_MODEL)
        xs = ffn(xs.reshape(n_s * t_s, D_MODEL), g_ffn, *args, g_fin).reshape(n_s, t_s, D_MODEL)

    st = lambda k: jnp.stack(outs[k])
    return (xp, xs, st("ap"), st("as"), st("bp"), st("bs"), st("vp"), st("vs"))
```

```python
import functools

import jax
import jax.numpy as jnp
from jax import lax
from jax.experimental import pallas as pl
from jax.experimental.pallas import tpu as pltpu

D_MODEL = 1024
D_BR = 512
A_WIDTH = 31
B_WIDTH = 3
C_HEADS = 4
CHUNK = 128
D_FF = 3584
N_EXPERTS = 8
EPS = 1e-6

A_HIST = 32
B_HIST = 8
LANES = 128
VMEM_LIMIT = 56 * 1024 * 1024

BF16 = jnp.bfloat16
F32 = jnp.float32


def _rms(x, g):
    return x * lax.rsqrt(jnp.mean(x * x, -1, keepdims=True) + EPS) * g


def _layer_norm(x, g, b):
    mu = jnp.mean(x, -1, keepdims=True)
    xc = x - mu
    var = jnp.mean(xc * xc, -1, keepdims=True)
    return xc * lax.rsqrt(var + EPS) * g + b


def _sigmoid(x):
    return 1.0 / (1.0 + jnp.exp(-x))


def _dot(a, b):
    return jnp.dot(a, b, preferred_element_type=F32)


def _mixer_kernel(x_ref, pa_ref, pb_ref, g_ref, win_ref, caw_ref, cab_ref, lag_ref, lab_ref,
                  cbw_ref, lcg_ref, lcb_ref, wsp_ref, bsp_ref, wpa_ref, wpb_ref, wpc_ref,
                  wout_ref, xo_ref, na_ref, nb_ref, nv_ref, fa, fz, sc, *, nb, tt, nv_rows):
    i = pl.program_id(1)
    rows = nb * tt

    @pl.when(i == 0)
    def _():
        fa[:, A_HIST - (A_WIDTH - 1):A_HIST, :] = pa_ref[...]
        fz[:, B_HIST - (B_WIDTH - 1):B_HIST, :] = pb_ref[...]

    x = x_ref[...].reshape(rows, D_MODEL)
    hb = _rms(x, g_ref[...]).astype(BF16)

    def proj(c0, c1):
        return _dot(hb, win_ref[:, c0:c1])

    a = proj(0, D_BR) * _sigmoid(proj(D_BR, 2 * D_BR))
    fa[:, A_HIST:A_HIST + tt, :] = a.reshape(nb, tt, D_BR)
    off_a = A_HIST - (A_WIDTH - 1)
    acc = jnp.zeros((nb, tt, D_BR), F32)
    for k in range(A_WIDTH):
        acc = acc + caw_ref[k:k + 1, :] * fa[:, off_a + k:off_a + k + tt, :]
    a_conv = acc.reshape(rows, D_BR) + cab_ref[...]
    a_ln = _layer_norm(a_conv, lag_ref[...], lab_ref[...])
    a_out = _dot((a_ln * _sigmoid(a_ln)).astype(BF16), wpa_ref[...])
    na_ref[...] = fa[:, tt + A_HIST - (A_WIDTH - 1):tt + A_HIST, :]
    fa[:, 0:A_HIST, :] = fa[:, tt:tt + A_HIST, :]

    c0 = 2 * D_BR
    z = proj(c0 + 2 * D_BR, c0 + 3 * D_BR) * proj(c0, c0 + D_BR)
    fz[:, B_HIST:B_HIST + tt, :] = z.reshape(nb, tt, D_BR)
    off_b = B_HIST - (B_WIDTH - 1)
    accb = jnp.zeros((nb, tt, D_BR), F32)
    for k in range(B_WIDTH):
        accb = accb + cbw_ref[k:k + 1, :] * fz[:, off_b + k:off_b + k + tt, :]
    b_out = _dot((proj(c0 + D_BR, c0 + 2 * D_BR) * accb.reshape(rows, D_BR)).astype(BF16),
                 wpb_ref[...])
    nb_ref[...] = fz[:, tt + B_HIST - (B_WIDTH - 1):tt + B_HIST, :]
    fz[:, 0:B_HIST, :] = fz[:, tt:tt + B_HIST, :]

    c0 = 5 * D_BR
    v = _layer_norm(proj(c0 + D_BR, c0 + 2 * D_BR), lcg_ref[...], lcb_ref[...])
    nv_ref[...] = v[rows - nv_rows * nb:, :].reshape(nb, nv_rows, D_BR)
    vb = v.astype(BF16)
    t_idx = lax.broadcasted_iota(jnp.int32, (CHUNK, CHUNK), 0)
    s_idx = lax.broadcasted_iota(jnp.int32, (CHUNK, CHUNK), 1)
    causal = s_idx <= t_idx
    for hd in range(C_HEADS):
        wm = jnp.where(causal, wsp_ref[hd], 0.0).astype(BF16)
        for c in range(rows // CHUNK):
            blk = (slice(c * CHUNK, (c + 1) * CHUNK), slice(hd * LANES, (hd + 1) * LANES))
            sc[blk] = _dot(wm, vb[blk])
    c_out = _dot((proj(c0, c0 + D_BR) * (sc[...].reshape(rows // CHUNK, CHUNK, D_BR)
                                        + bsp_ref[...]).reshape(rows, D_BR)).astype(BF16),
                 wpc_ref[...])

    c0 = 7 * D_BR
    merged = (_sigmoid(proj(c0, c0 + D_MODEL)) * a_out
              + _sigmoid(proj(c0 + D_MODEL, c0 + 2 * D_MODEL)) * b_out
              + _sigmoid(proj(c0 + 2 * D_MODEL, c0 + 3 * D_MODEL)) * c_out)
    xo_ref[...] = (x + _dot(merged.astype(BF16), wout_ref[...])).reshape(nb, tt, D_MODEL)


def _const_spec(shape):
    nd = len(shape)
    return pl.BlockSpec(shape, lambda b, i: (0,) * nd, pipeline_mode=pl.Buffered(1))


def _mixer(x, past_a, past_b, w, *, nb, tt):
    n, t, _ = x.shape
    assert n % nb == 0 and t % tt == 0 and (nb * tt) % CHUNK == 0
    nv_rows = t - ((t - 1) // CHUNK) * CHUNK
    assert nv_rows <= tt
    n_t = t // tt
    consts = [w["g"], w["w_in"], w["conv_a_w"], w["conv_a_b"], w["ln_a_g"], w["ln_a_b"],
              w["conv_b_w"], w["ln_c_g"], w["ln_c_b"], w["wsp"], w["bsp"], w["w_proj_a"],
              w["w_proj_b"], w["w_proj_c"], w["w_out"]]
    seq_spec = lambda r, c: pl.BlockSpec((nb, r, c), lambda b, i: (b, 0, 0))
    return pl.pallas_call(
        functools.partial(_mixer_kernel, nb=nb, tt=tt, nv_rows=nv_rows),
        grid=(n // nb, n_t),
        in_specs=[pl.BlockSpec((nb, tt, D_MODEL), lambda b, i: (b, i, 0)),
                  seq_spec(A_WIDTH - 1, D_BR), seq_spec(B_WIDTH - 1, D_BR)]
                 + [_const_spec(c.shape) for c in consts],
        out_specs=[pl.BlockSpec((nb, tt, D_MODEL), lambda b, i: (b, i, 0)),
                   seq_spec(A_WIDTH - 1, D_BR), seq_spec(B_WIDTH - 1, D_BR),
                   seq_spec(nv_rows, D_BR)],
        out_shape=[jax.ShapeDtypeStruct((n, t, D_MODEL), F32),
                   jax.ShapeDtypeStruct((n, A_WIDTH - 1, D_BR), F32),
                   jax.ShapeDtypeStruct((n, B_WIDTH - 1, D_BR), F32),
                   jax.ShapeDtypeStruct((n, nv_rows, D_BR), F32)],
        scratch_shapes=[pltpu.VMEM((nb, A_HIST + tt, D_BR), F32),
                        pltpu.VMEM((nb, B_HIST + tt, D_BR), F32),
                        pltpu.VMEM((nb * tt, D_BR), F32)],
        compiler_params=pltpu.CompilerParams(
            dimension_semantics=("arbitrary", "arbitrary"), vmem_limit_bytes=VMEM_LIMIT),
        name="mixer",
    )(x, past_a, past_b, *consts)


def _swiglu_part(h_b, wg_ref, wu_ref, wd_ref):
    gate = _dot(h_b, wg_ref[...])
    act = gate * _sigmoid(gate) * _dot(h_b, wu_ref[...])
    return _dot(act.astype(BF16), wd_ref[...])


def _ffn_kernel(x_ref, g_ref, wg_ref, wu_ref, wd_ref, o_ref, hb, acc):
    j = pl.program_id(1)

    @pl.when(j == 0)
    def _():
        hb[...] = _rms(x_ref[...], g_ref[...]).astype(BF16)
        acc[...] = x_ref[...]

    acc[...] += _swiglu_part(hb[...], wg_ref, wu_ref, wd_ref)

    @pl.when(j == pl.num_programs(1) - 1)
    def _():
        o_ref[...] = acc[...]


def _ffn(x, g, wg, wu, wd, *, tm, tf):
    m, _ = x.shape
    assert m % tm == 0 and D_FF % tf == 0
    row = lambda i, j: (i, 0)
    return pl.pallas_call(
        _ffn_kernel,
        grid=(m // tm, D_FF // tf),
        in_specs=[pl.BlockSpec((tm, D_MODEL), row),
                  pl.BlockSpec((1, D_MODEL), lambda i, j: (0, 0)),
                  pl.BlockSpec((D_MODEL, tf), lambda i, j: (0, j)),
                  pl.BlockSpec((D_MODEL, tf), lambda i, j: (0, j)),
                  pl.BlockSpec((tf, D_MODEL), lambda i, j: (j, 0))],
        out_specs=pl.BlockSpec((tm, D_MODEL), row),
        out_shape=jax.ShapeDtypeStruct((m, D_MODEL), F32),
        scratch_shapes=[pltpu.VMEM((tm, D_MODEL), BF16), pltpu.VMEM((tm, D_MODEL), F32)],
        compiler_params=pltpu.CompilerParams(
            dimension_semantics=("arbitrary", "arbitrary"), vmem_limit_bytes=VMEM_LIMIT),
        name="ffn",
    )(x, g, wg, wu, wd)


SLAB = D_MODEL // LANES


def _to_slab(ref, val, rows):
    for c in range(SLAB):
        ref[pl.ds(c, rows, stride=SLAB), :] = val[:, c * LANES:(c + 1) * LANES]


def _route_kernel(x_ref, g_ref, wr_ref, br_ref, hs_ref, mi_ref, mf_ref, cnt_ref, carry, *, cap):
    i = pl.program_id(0)
    tm = x_ref.shape[0]

    @pl.when(i == 0)
    def _():
        carry[...] = jnp.zeros_like(carry)

    h = _rms(x_ref[...], g_ref[...])
    _to_slab(hs_ref, h, tm)

    col = lax.broadcasted_iota(jnp.int32, (tm, LANES), 1)
    logits = jnp.dot(h, wr_ref[...], preferred_element_type=F32,
                     precision=lax.Precision.HIGHEST) + br_ref[...]
    logits = jnp.where(col < N_EXPERTS, logits, -jnp.inf)
    m1 = jnp.max(logits, -1, keepdims=True)
    i1 = jnp.min(jnp.where(logits == m1, col, LANES), -1, keepdims=True)
    rest = jnp.where(col == i1, -jnp.inf, logits)
    m2 = jnp.max(rest, -1, keepdims=True)
    i2 = jnp.min(jnp.where(rest == m2, col, LANES), -1, keepdims=True)
    e2 = jnp.exp(m2 - m1)
    p1 = 1.0 / (1.0 + e2)
    p2 = e2 * p1

    chosen = jnp.logical_or(col == i1, col == i2).astype(F32)
    r_idx = lax.broadcasted_iota(jnp.int32, (tm, tm), 0)
    c_idx = lax.broadcasted_iota(jnp.int32, (tm, tm), 1)
    before = (c_idx < r_idx).astype(BF16)
    rank = carry[...] + _dot(before, chosen.astype(BF16))
    dest = col.astype(F32) * float(cap) + rank
    d1 = jnp.sum(jnp.where(col == i1, dest, 0.0), -1, keepdims=True).astype(jnp.int32)
    d2 = jnp.sum(jnp.where(col == i2, dest, 0.0), -1, keepdims=True).astype(jnp.int32)
    mi_ref[...] = jnp.where(col == 0, d1, jnp.where(col == 1, d2, 0))
    mf_ref[...] = jnp.where(col == 0, p1, jnp.where(col == 1, p2, 0.0))
    carry[...] += jnp.sum(chosen, 0, keepdims=True)
    cnt_ref[...] = carry[...]


def _route(x, g, wr, br, *, tm, cap):
    m = x.shape[0]
    assert m % tm == 0
    row = lambda i: (i, 0)
    const = lambda i: (0, 0)
    return pl.pallas_call(
        functools.partial(_route_kernel, cap=cap),
        grid=(m // tm,),
        in_specs=[pl.BlockSpec((tm, D_MODEL), row), pl.BlockSpec((1, D_MODEL), const),
                  pl.BlockSpec((D_MODEL, LANES), const), pl.BlockSpec((1, LANES), const)],
        out_specs=[pl.BlockSpec((tm * SLAB, LANES), row), pl.BlockSpec((tm, LANES), row),
                   pl.BlockSpec((tm, LANES), row), pl.BlockSpec((1, LANES), const)],
        out_shape=[jax.ShapeDtypeStruct((m * SLAB, LANES), F32),
                   jax.ShapeDtypeStruct((m, LANES), jnp.int32),
                   jax.ShapeDtypeStruct((m, LANES), F32),
                   jax.ShapeDtypeStruct((1, LANES), F32)],
        scratch_shapes=[pltpu.VMEM((1, LANES), F32)],
        compiler_params=pltpu.CompilerParams(dimension_semantics=("arbitrary",),
                                             vmem_limit_bytes=VMEM_LIMIT),
        name="route",
    )(x, g, wr, br)


def _row_copy(src_ref, src_row, dst_ref, dst_row, sem):
    return pltpu.make_async_copy(
        src_ref.at[pl.ds(pl.multiple_of(src_row * SLAB, SLAB), SLAB), :],
        dst_ref.at[pl.ds(pl.multiple_of(dst_row * SLAB, SLAB), SLAB), :], sem)


def _wait_rows(src_ref, dst_ref, n_rows, sem):
    pltpu.make_async_copy(src_ref.at[pl.ds(0, n_rows * SLAB), :],
                          dst_ref.at[pl.ds(0, n_rows * SLAB), :], sem).wait()


def _dispatch_kernel(idx_ref, src_ref, dst_ref, sem, *, rows):
    i = pl.program_id(0)
    base = i * rows
    slot = i % 2

    def issue(r, c):
        for k in range(2):
            _row_copy(src_ref, base + r, dst_ref, idx_ref[0, 0, 2 * r + k], sem.at[slot]).start()
        return c

    lax.fori_loop(0, rows, issue, 0)

    @pl.when(i > 0)
    def _():
        _wait_rows(src_ref, dst_ref, 2 * rows, sem.at[1 - slot])

    @pl.when(i == pl.num_programs(0) - 1)
    def _():
        _wait_rows(src_ref, dst_ref, 2 * rows, sem.at[slot])


def _dispatch(h_slab, dest, *, rows, cap_rows):
    n_blk = dest.shape[0]
    return pl.pallas_call(
        functools.partial(_dispatch_kernel, rows=rows),
        grid=(n_blk,),
        in_specs=[pl.BlockSpec((1, 1, 2 * rows), lambda i: (i, 0, 0), memory_space=pltpu.SMEM),
                  pl.BlockSpec(memory_space=pl.ANY)],
        out_specs=pl.BlockSpec(memory_space=pl.ANY),
        out_shape=jax.ShapeDtypeStruct((cap_rows * SLAB, LANES), F32),
        scratch_shapes=[pltpu.SemaphoreType.DMA((2,))],
        compiler_params=pltpu.CompilerParams(dimension_semantics=("arbitrary",)),
        name="dispatch",
    )(dest, h_slab)


def _experts_kernel(tblk, texp, tn, xs_ref, wg_ref, wu_ref, wd_ref, o_ref, hb, acc):
    t = pl.program_id(0)
    j = pl.program_id(1)
    n = tn[t]
    tm = hb.shape[0]

    @pl.when(n > 0)
    def _():
        @pl.when(j == 0)
        def _():
            valid = lax.broadcasted_iota(jnp.int32, (tm, LANES), 0) < n
            for c in range(SLAB):
                xc = xs_ref[pl.ds(c, tm, stride=SLAB), :]
                hb[:, c * LANES:(c + 1) * LANES] = jnp.where(valid, xc, 0.0).astype(BF16)

        part = _swiglu_part(hb[...], wg_ref, wu_ref, wd_ref)

        @pl.when(j == 0)
        def _():
            acc[...] = part

        @pl.when(jnp.logical_and(j > 0, j < pl.num_programs(1) - 1))
        def _():
            acc[...] += part

        @pl.when(j == pl.num_programs(1) - 1)
        def _():
            _to_slab(o_ref, acc[...] + part, tm)


def _experts(xs_slab, tblk, texp, tn, wg, wu, wd, *, tm, tf):
    n_tiles = tblk.shape[0]
    n_j = D_FF // tf
    assert n_j >= 2
    jj = lambda t, j, tn: jnp.where(tn[t] > 0, j, n_j - 1)
    grid_spec = pltpu.PrefetchScalarGridSpec(
        num_scalar_prefetch=3,
        grid=(n_tiles, n_j),
        in_specs=[pl.BlockSpec((tm * SLAB, LANES), lambda t, j, tblk, texp, tn: (tblk[t], 0)),
                  pl.BlockSpec((None, D_MODEL, tf),
                               lambda t, j, tblk, texp, tn: (texp[t], 0, jj(t, j, tn))),
                  pl.BlockSpec((None, D_MODEL, tf),
                               lambda t, j, tblk, texp, tn: (texp[t], 0, jj(t, j, tn))),
                  pl.BlockSpec((None, tf, D_MODEL),
                               lambda t, j, tblk, texp, tn: (texp[t], jj(t, j, tn), 0))],
        out_specs=pl.BlockSpec((tm * SLAB, LANES), lambda t, j, tblk, texp, tn: (tblk[t], 0)),
        scratch_shapes=[pltpu.VMEM((tm, D_MODEL), BF16), pltpu.VMEM((tm, D_MODEL), F32)])
    return pl.pallas_call(
        _experts_kernel,
        grid_spec=grid_spec,
        out_shape=jax.ShapeDtypeStruct(xs_slab.shape, F32),
        compiler_params=pltpu.CompilerParams(dimension_semantics=("arbitrary", "arbitrary"),
                                             vmem_limit_bytes=VMEM_LIMIT),
        name="experts",
    )(tblk, texp, tn, xs_slab, wg, wu, wd)


def _combine_kernel(idx_ref, idx_next_ref, x_ref, mf_ref, gf_ref, ys_ref, op_ref, os_ref,
                    buf, ysum, sem, *, rows, n_prompt_tiles):
    i = pl.program_id(0)
    n_i = pl.num_programs(0)
    slot = i % 2

    def issue(ref, s):
        def body(r, c):
            for k in range(2):
                _row_copy(ys_ref, ref[0, 0, 2 * r + k], buf.at[s], k * rows + r, sem.at[s]).start()
            return c
        lax.fori_loop(0, rows, body, 0)

    def finish(s):
        _wait_rows(ys_ref, buf.at[s], 2 * rows, sem.at[s])
        p1 = mf_ref[:, 0:1]
        p2 = mf_ref[:, 1:2]
        for c in range(SLAB):
            cs = slice(c * LANES, (c + 1) * LANES)
            y1 = buf[s, pl.ds(c, rows, stride=SLAB), :]
            y2 = buf[s, pl.ds(rows * SLAB + c, rows, stride=SLAB), :]
            ysum[:, cs] = x_ref[:, cs] + p1 * y1 + p2 * y2
        res = _rms(ysum[...], gf_ref[...])

        @pl.when(i < n_prompt_tiles)
        def _():
            op_ref[...] = res

        @pl.when(i >= n_prompt_tiles)
        def _():
            os_ref[...] = res

    for s in range(2):
        @pl.when(slot == s)
        def _():
            @pl.when(i == 0)
            def _():
                issue(idx_ref, s)

            @pl.when(i + 1 < n_i)
            def _():
                issue(idx_next_ref, 1 - s)

            finish(s)


def _combine(x, mf, g_final, y_slab, dest, *, rows, m_prompt):
    m = x.shape[0]
    n_blk = dest.shape[0]
    assert m == n_blk * rows and m_prompt % rows == 0
    n_p = m_prompt // rows
    idx_spec = lambda f: pl.BlockSpec((1, 1, 2 * rows), f, memory_space=pltpu.SMEM)
    return pl.pallas_call(
        functools.partial(_combine_kernel, rows=rows, n_prompt_tiles=n_p),
        grid=(n_blk,),
        in_specs=[idx_spec(lambda i: (i, 0, 0)),
                  idx_spec(lambda i: (jnp.minimum(i + 1, n_blk - 1), 0, 0)),
                  pl.BlockSpec((rows, D_MODEL), lambda i: (i, 0)),
                  pl.BlockSpec((rows, LANES), lambda i: (i, 0)),
                  pl.BlockSpec((1, D_MODEL), lambda i: (0, 0)),
                  pl.BlockSpec(memory_space=pl.ANY)],
        out_specs=[pl.BlockSpec((rows, D_MODEL), lambda i: (jnp.minimum(i, n_p - 1), 0)),
                   pl.BlockSpec((rows, D_MODEL), lambda i: (jnp.maximum(i - n_p, 0), 0))],
        out_shape=[jax.ShapeDtypeStruct((m_prompt, D_MODEL), F32),
                   jax.ShapeDtypeStruct((m - m_prompt, D_MODEL), F32)],
        scratch_shapes=[pltpu.VMEM((2, 2 * rows * SLAB, LANES), F32),
                        pltpu.VMEM((rows, D_MODEL), F32),
                        pltpu.SemaphoreType.DMA((2,))],
        compiler_params=pltpu.CompilerParams(dimension_semantics=("arbitrary",),
                                             vmem_limit_bytes=VMEM_LIMIT),
        name="combine",
    )(dest, dest, x, mf, g_final, y_slab)


def _expert_tiles(counts, *, tm, cap, n_tiles):
    tiles = (counts + tm - 1) // tm
    ends = jnp.cumsum(tiles)
    t = jnp.arange(n_tiles, dtype=jnp.int32)
    tc = jnp.minimum(t, ends[-1] - 1)
    e = jnp.sum((tc[:, None] >= ends[None, :]).astype(jnp.int32), axis=1)
    k = tc - (ends - tiles)[e]
    tblk = e * (cap // tm) + k
    tn = jnp.where(t < ends[-1], jnp.minimum(counts[e] - k * tm, tm), 0)
    return tblk.astype(jnp.int32), e.astype(jnp.int32), tn.astype(jnp.int32)


def _moe_final(x, g_ffn, wr, br, wg, wu, wd, g_final, *, m_prompt):
    m = x.shape[0]
    tm, tf, out_rows, in_rows = 512, 512, 1024, 256
    cap = -(-m // tm) * tm
    h_slab, mi, mf, cnt = _route(x, g_ffn, wr, br, tm=tm, cap=cap)
    dest = mi[:, :2]
    counts = cnt[0, :N_EXPERTS].astype(jnp.int32)
    n_tiles = 2 * m // tm + N_EXPERTS
    tblk, texp, tn = _expert_tiles(counts, tm=tm, cap=cap, n_tiles=n_tiles)
    xs_slab = _dispatch(h_slab, dest.reshape(m // out_rows, 1, 2 * out_rows), rows=out_rows,
                        cap_rows=N_EXPERTS * cap)
    y_slab = _experts(xs_slab, tblk, texp, tn, wg, wu, wd, tm=tm, tf=tf)
    return _combine(x, mf, g_final, y_slab, dest.reshape(m // in_rows, 1, 2 * in_rows),
                    rows=in_rows, m_prompt=m_prompt)


def _row(v):
    return v.reshape(1, -1)


def kernel(x_prompt, x_sample, state_conv_a, state_conv_b, norm_mix_g, w_in, conv_a_w, conv_a_b, ln_a_g, ln_a_b, conv_b_w, ln_c_g, ln_c_b, w_spatial, b_spatial, w_proj_a, w_proj_b, w_proj_c, w_out, norm_ffn_g, ffn_w_gate, ffn_w_up, ffn_w_down, w_router, b_router, moe_w_gate, moe_w_up, moe_w_down, norm_final_g):
    depth = w_in.shape[0]
    n_p, t_p, _ = x_prompt.shape
    n_s, t_s, _ = x_sample.shape
    seqs_per_chunk = CHUNK // t_s

    zeros_a = jnp.zeros((n_p, A_WIDTH - 1, D_BR), F32)
    zeros_b = jnp.zeros((n_p, B_WIDTH - 1, D_BR), F32)

    xp, xs = x_prompt, x_sample
    outs = {k: [] for k in ("ap", "as", "bp", "bs", "vp", "vs")}
    for l in range(depth):
        bsp_p = jnp.repeat(b_spatial[l].T, LANES, axis=1)
        eye = jnp.eye(seqs_per_chunk, dtype=F32)
        wsp_s = jnp.einsum("ab,hts->hatbs", eye, w_spatial[l][:, :t_s, :t_s]).reshape(
            C_HEADS, CHUNK, CHUNK)
        bsp_s = jnp.tile(bsp_p[:t_s], (seqs_per_chunk, 1))
        w = dict(g=_row(norm_mix_g[l]), w_in=w_in[l].astype(BF16), conv_a_w=conv_a_w[l],
                 conv_a_b=_row(conv_a_b[l]), ln_a_g=_row(ln_a_g[l]), ln_a_b=_row(ln_a_b[l]),
                 conv_b_w=conv_b_w[l], ln_c_g=_row(ln_c_g[l]), ln_c_b=_row(ln_c_b[l]),
                 w_proj_a=w_proj_a[l].astype(BF16), w_proj_b=w_proj_b[l].astype(BF16),
                 w_proj_c=w_proj_c[l].astype(BF16), w_out=w_out[l].astype(BF16))
        xp, a_p, b_p, v_p = _mixer(xp, zeros_a, zeros_b, dict(w, wsp=w_spatial[l], bsp=bsp_p),
                                   nb=1, tt=512)
        xs, a_s, b_s, v_s = _mixer(xs, state_conv_a[l], state_conv_b[l],
                                   dict(w, wsp=wsp_s, bsp=bsp_s), nb=32, tt=t_s)
        for k, val in zip(("ap", "as", "bp", "bs", "vp", "vs"), (a_p, a_s, b_p, b_s, v_p, v_s)):
            outs[k].append(val)

        i = l // 2
        g_ffn = _row(norm_ffn_g[l])
        g_fin = _row(norm_final_g)
        xp = xp.reshape(n_p * t_p, D_MODEL)
        xs = xs.reshape(n_s * t_s, D_MODEL)
        if l % 2 == 0:
            args = (g_ffn, ffn_w_gate[i].astype(BF16), ffn_w_up[i].astype(BF16),
                    ffn_w_down[i].astype(BF16))
            xp = _ffn(xp, *args, tm=512, tf=512)
            xs = _ffn(xs, *args, tm=512, tf=512)
        else:
            assert l == depth - 1, "the routed layer applies the final norm"
            wr = jnp.pad(w_router[i], ((0, 0), (0, LANES - N_EXPERTS)))
            br = jnp.pad(_row(b_router[i]), ((0, 0), (0, LANES - N_EXPERTS)))
            xp, xs = _moe_final(jnp.concatenate([xp, xs], axis=0), g_ffn, wr, br,
                                moe_w_gate[i].astype(BF16), moe_w_up[i].astype(BF16),
                                moe_w_down[i].astype(BF16), g_fin, m_prompt=n_p * t_p)
        xp = xp.reshape(n_p, t_p, D_MODEL)
        xs = xs.reshape(n_s, t_s, D_MODEL)

    st = lambda k: jnp.stack(outs[k])
    return (xp, xs, st("ap"), st("as"), st("bp"), st("bs"), st("vp"), st("vs"))
```

```python
import functools

import jax
import jax.numpy as jnp
from jax import lax
from jax.experimental import pallas as pl
from jax.experimental.pallas import tpu as pltpu

D_MODEL = 1024
D_BR = 512
A_WIDTH = 31
B_WIDTH = 3
C_HEADS = 4
CHUNK = 128
D_FF = 3584
N_EXPERTS = 8
EPS = 1e-6

A_HIST = 32
B_HIST = 8
LANES = 128
SUBLANES = 8
CONV_ROWS = 32
VMEM_LIMIT = 56 * 1024 * 1024

BF16 = jnp.bfloat16
F32 = jnp.float32


def _rms(x, g):
    return x * lax.rsqrt(jnp.mean(x * x, -1, keepdims=True) + EPS) * g


def _layer_norm(x, g, b):
    mu = jnp.mean(x, -1, keepdims=True)
    xc = x - mu
    var = jnp.mean(xc * xc, -1, keepdims=True)
    return xc * lax.rsqrt(var + EPS) * g + b


def _sigmoid(x):
    return 0.5 * jnp.tanh(0.5 * x) + 0.5


def _dot(a, b):
    return jnp.dot(a, b, preferred_element_type=F32)


def _mixer_kernel(x_ref, pa_ref, pb_ref, g_ref, win_ref, caw_ref, cab_ref, lag_ref, lab_ref,
                  cbw_ref, lcg_ref, lcb_ref, wsp_ref, bsp_ref, wpa_ref, wpb_ref, wpc_ref,
                  wout_ref, xo_ref, na_ref, nb_ref, nv_ref, fa, fz, sc, sh, ca, *, nb, tt, nv_rows):
    i = pl.program_id(1)
    rows = nb * tt

    @pl.when(i == 0)
    def _():
        fa[:, A_HIST - (A_WIDTH - 1):A_HIST, :] = pa_ref[...]
        fz[:, B_HIST - (B_WIDTH - 1):B_HIST, :] = pb_ref[...]

    x = x_ref[...].reshape(rows, D_MODEL)
    hb = _rms(x, g_ref[...]).astype(BF16)

    def proj(c0, c1):
        return _dot(hb, win_ref[:, c0:c1])

    a = proj(0, D_BR) * _sigmoid(proj(D_BR, 2 * D_BR))
    fa[:, A_HIST:A_HIST + tt, :] = a.reshape(nb, tt, D_BR)
    off_a = A_HIST - (A_WIDTH - 1)
    for j in range(SUBLANES):
        span = tt + SUBLANES * ((A_WIDTH - 1 - j) // SUBLANES)
        sh[j, :, 0:span, :] = fa[:, off_a + j:off_a + j + span, :]
    rb = min(tt, CONV_ROWS)
    sb = CONV_ROWS // rb
    n_rb = tt // rb

    def conv_step(u, carry):
        s0 = (u // n_rb) * sb
        r0 = (u % n_rb) * rb
        acc = jnp.zeros((sb, rb, D_BR), F32)
        for k in range(A_WIDTH):
            q, j = divmod(k, SUBLANES)
            win = sh[j, pl.ds(s0, sb), pl.ds(pl.multiple_of(r0 + SUBLANES * q, SUBLANES), rb), :]
            acc = acc + caw_ref[k:k + 1, :] * win
        ca[pl.ds(s0, sb), pl.ds(pl.multiple_of(r0, SUBLANES), rb), :] = acc + cab_ref[...]
        return carry

    lax.fori_loop(0, (nb // sb) * n_rb, conv_step, 0)
    a_ln = _layer_norm(ca[...].reshape(rows, D_BR), lag_ref[...], lab_ref[...])
    a_out = _dot((a_ln * _sigmoid(a_ln)).astype(BF16), wpa_ref[...])
    na_ref[...] = fa[:, tt + A_HIST - (A_WIDTH - 1):tt + A_HIST, :]
    fa[:, 0:A_HIST, :] = fa[:, tt:tt + A_HIST, :]

    c0 = 2 * D_BR
    z = proj(c0 + 2 * D_BR, c0 + 3 * D_BR) * proj(c0, c0 + D_BR)
    fz[:, B_HIST:B_HIST + tt, :] = z.reshape(nb, tt, D_BR)
    off_b = B_HIST - (B_WIDTH - 1)
    accb = jnp.zeros((nb, tt, D_BR), F32)
    for k in range(B_WIDTH):
        accb = accb + cbw_ref[k:k + 1, :] * fz[:, off_b + k:off_b + k + tt, :]
    b_out = _dot((proj(c0 + D_BR, c0 + 2 * D_BR) * accb.reshape(rows, D_BR)).astype(BF16),
                 wpb_ref[...])
    nb_ref[...] = fz[:, tt + B_HIST - (B_WIDTH - 1):tt + B_HIST, :]
    fz[:, 0:B_HIST, :] = fz[:, tt:tt + B_HIST, :]

    c0 = 5 * D_BR
    v = _layer_norm(proj(c0 + D_BR, c0 + 2 * D_BR), lcg_ref[...], lcb_ref[...])
    nv_ref[...] = v[rows - nv_rows * nb:, :].reshape(nb, nv_rows, D_BR)
    vb = v.astype(BF16)
    t_idx = lax.broadcasted_iota(jnp.int32, (CHUNK, CHUNK), 0)
    s_idx = lax.broadcasted_iota(jnp.int32, (CHUNK, CHUNK), 1)
    causal = s_idx <= t_idx
    for hd in range(C_HEADS):
        wm = jnp.where(causal, wsp_ref[hd], 0.0).astype(BF16)
        for c in range(rows // CHUNK):
            blk = (slice(c * CHUNK, (c + 1) * CHUNK), slice(hd * LANES, (hd + 1) * LANES))
            sc[blk] = _dot(wm, vb[blk])
    c_out = _dot((proj(c0, c0 + D_BR) * (sc[...].reshape(rows // CHUNK, CHUNK, D_BR)
                                        + bsp_ref[...]).reshape(rows, D_BR)).astype(BF16),
                 wpc_ref[...])

    c0 = 7 * D_BR
    merged = (_sigmoid(proj(c0, c0 + D_MODEL)) * a_out
              + _sigmoid(proj(c0 + D_MODEL, c0 + 2 * D_MODEL)) * b_out
              + _sigmoid(proj(c0 + 2 * D_MODEL, c0 + 3 * D_MODEL)) * c_out)
    xo_ref[...] = (x + _dot(merged.astype(BF16), wout_ref[...])).reshape(nb, tt, D_MODEL)


def _const_spec(shape):
    nd = len(shape)
    return pl.BlockSpec(shape, lambda b, i: (0,) * nd, pipeline_mode=pl.Buffered(1))


def _mixer(x, past_a, past_b, w, *, nb, tt):
    n, t, _ = x.shape
    assert n % nb == 0 and t % tt == 0 and (nb * tt) % CHUNK == 0
    assert tt % SUBLANES == 0 and (tt % CONV_ROWS == 0 or
                                   (CONV_ROWS % tt == 0 and nb % (CONV_ROWS // tt) == 0))
    nv_rows = t - ((t - 1) // CHUNK) * CHUNK
    assert nv_rows <= tt
    n_t = t // tt
    consts = [w["g"], w["w_in"], w["conv_a_w"], w["conv_a_b"], w["ln_a_g"], w["ln_a_b"],
              w["conv_b_w"], w["ln_c_g"], w["ln_c_b"], w["wsp"], w["bsp"], w["w_proj_a"],
              w["w_proj_b"], w["w_proj_c"], w["w_out"]]
    seq_spec = lambda r, c: pl.BlockSpec((nb, r, c), lambda b, i: (b, 0, 0))
    return pl.pallas_call(
        functools.partial(_mixer_kernel, nb=nb, tt=tt, nv_rows=nv_rows),
        grid=(n // nb, n_t),
        in_specs=[pl.BlockSpec((nb, tt, D_MODEL), lambda b, i: (b, i, 0)),
                  seq_spec(A_WIDTH - 1, D_BR), seq_spec(B_WIDTH - 1, D_BR)]
                 + [_const_spec(c.shape) for c in consts],
        out_specs=[pl.BlockSpec((nb, tt, D_MODEL), lambda b, i: (b, i, 0)),
                   seq_spec(A_WIDTH - 1, D_BR), seq_spec(B_WIDTH - 1, D_BR),
                   seq_spec(nv_rows, D_BR)],
        out_shape=[jax.ShapeDtypeStruct((n, t, D_MODEL), F32),
                   jax.ShapeDtypeStruct((n, A_WIDTH - 1, D_BR), F32),
                   jax.ShapeDtypeStruct((n, B_WIDTH - 1, D_BR), F32),
                   jax.ShapeDtypeStruct((n, nv_rows, D_BR), F32)],
        scratch_shapes=[pltpu.VMEM((nb, A_HIST + tt, D_BR), F32),
                        pltpu.VMEM((nb, B_HIST + tt, D_BR), F32),
                        pltpu.VMEM((nb * tt, D_BR), F32),
                        pltpu.VMEM((SUBLANES, nb, tt + A_HIST - SUBLANES, D_BR), F32),
                        pltpu.VMEM((nb, tt, D_BR), F32)],
        compiler_params=pltpu.CompilerParams(
            dimension_semantics=("arbitrary", "arbitrary"), vmem_limit_bytes=VMEM_LIMIT),
        name="mixer",
    )(x, past_a, past_b, *consts)


def _swiglu_part(h_b, wg_ref, wu_ref, wd_ref):
    gate = _dot(h_b, wg_ref[...])
    act = gate * _sigmoid(gate) * _dot(h_b, wu_ref[...])
    return _dot(act.astype(BF16), wd_ref[...])


def _ffn_kernel(x_ref, g_ref, wg_ref, wu_ref, wd_ref, o_ref, hb, acc):
    j = pl.program_id(1)

    @pl.when(j == 0)
    def _():
        hb[...] = _rms(x_ref[...], g_ref[...]).astype(BF16)
        acc[...] = x_ref[...]

    acc[...] += _swiglu_part(hb[...], wg_ref, wu_ref, wd_ref)

    @pl.when(j == pl.num_programs(1) - 1)
    def _():
        o_ref[...] = acc[...]


def _ffn(x, g, wg, wu, wd, *, tm, tf):
    m, _ = x.shape
    assert m % tm == 0 and D_FF % tf == 0
    row = lambda i, j: (i, 0)
    return pl.pallas_call(
        _ffn_kernel,
        grid=(m // tm, D_FF // tf),
        in_specs=[pl.BlockSpec((tm, D_MODEL), row),
                  pl.BlockSpec((1, D_MODEL), lambda i, j: (0, 0)),
                  pl.BlockSpec((D_MODEL, tf), lambda i, j: (0, j)),
                  pl.BlockSpec((D_MODEL, tf), lambda i, j: (0, j)),
                  pl.BlockSpec((tf, D_MODEL), lambda i, j: (j, 0))],
        out_specs=pl.BlockSpec((tm, D_MODEL), row),
        out_shape=jax.ShapeDtypeStruct((m, D_MODEL), F32),
        scratch_shapes=[pltpu.VMEM((tm, D_MODEL), BF16), pltpu.VMEM((tm, D_MODEL), F32)],
        compiler_params=pltpu.CompilerParams(
            dimension_semantics=("arbitrary", "arbitrary"), vmem_limit_bytes=VMEM_LIMIT),
        name="ffn",
    )(x, g, wg, wu, wd)


SLAB = D_MODEL // LANES


def _to_slab(ref, val, rows):
    for c in range(SLAB):
        ref[pl.ds(c, rows, stride=SLAB), :] = val[:, c * LANES:(c + 1) * LANES]


def _route_kernel(x_ref, g_ref, wr_ref, br_ref, hs_ref, mi_ref, mf_ref, cnt_ref, carry, *, cap):
    i = pl.program_id(0)
    tm = x_ref.shape[0]

    @pl.when(i == 0)
    def _():
        carry[...] = jnp.zeros_like(carry)

    h = _rms(x_ref[...], g_ref[...])
    _to_slab(hs_ref, h, tm)

    col = lax.broadcasted_iota(jnp.int32, (tm, LANES), 1)
    logits = jnp.dot(h, wr_ref[...], preferred_element_type=F32,
                     precision=lax.Precision.HIGHEST) + br_ref[...]
    logits = jnp.where(col < N_EXPERTS, logits, -jnp.inf)
    m1 = jnp.max(logits, -1, keepdims=True)
    i1 = jnp.min(jnp.where(logits == m1, col, LANES), -1, keepdims=True)
    rest = jnp.where(col == i1, -jnp.inf, logits)
    m2 = jnp.max(rest, -1, keepdims=True)
    i2 = jnp.min(jnp.where(rest == m2, col, LANES), -1, keepdims=True)
    e2 = jnp.exp(m2 - m1)
    p1 = 1.0 / (1.0 + e2)
    p2 = e2 * p1

    chosen = jnp.logical_or(col == i1, col == i2).astype(F32)
    r_idx = lax.broadcasted_iota(jnp.int32, (tm, tm), 0)
    c_idx = lax.broadcasted_iota(jnp.int32, (tm, tm), 1)
    before = (c_idx < r_idx).astype(BF16)
    rank = carry[...] + _dot(before, chosen.astype(BF16))
    dest = col.astype(F32) * float(cap) + rank
    d1 = jnp.sum(jnp.where(col == i1, dest, 0.0), -1, keepdims=True).astype(jnp.int32)
    d2 = jnp.sum(jnp.where(col == i2, dest, 0.0), -1, keepdims=True).astype(jnp.int32)
    mi_ref[...] = jnp.where(col == 0, d1, jnp.where(col == 1, d2, 0))
    mf_ref[...] = jnp.where(col == 0, p1, jnp.where(col == 1, p2, 0.0))
    carry[...] += jnp.sum(chosen, 0, keepdims=True)
    cnt_ref[...] = carry[...]


def _route(x, g, wr, br, *, tm, cap):
    m = x.shape[0]
    assert m % tm == 0
    row = lambda i: (i, 0)
    const = lambda i: (0, 0)
    return pl.pallas_call(
        functools.partial(_route_kernel, cap=cap),
        grid=(m // tm,),
        in_specs=[pl.BlockSpec((tm, D_MODEL), row), pl.BlockSpec((1, D_MODEL), const),
                  pl.BlockSpec((D_MODEL, LANES), const), pl.BlockSpec((1, LANES), const)],
        out_specs=[pl.BlockSpec((tm * SLAB, LANES), row), pl.BlockSpec((tm, LANES), row),
                   pl.BlockSpec((tm, LANES), row), pl.BlockSpec((1, LANES), const)],
        out_shape=[jax.ShapeDtypeStruct((m * SLAB, LANES), F32),
                   jax.ShapeDtypeStruct((m, LANES), jnp.int32),
                   jax.ShapeDtypeStruct((m, LANES), F32),
                   jax.ShapeDtypeStruct((1, LANES), F32)],
        scratch_shapes=[pltpu.VMEM((1, LANES), F32)],
        compiler_params=pltpu.CompilerParams(dimension_semantics=("arbitrary",),
                                             vmem_limit_bytes=VMEM_LIMIT),
        name="route",
    )(x, g, wr, br)


def _row_copy(src_ref, src_row, dst_ref, dst_row, sem):
    return pltpu.make_async_copy(
        src_ref.at[pl.ds(pl.multiple_of(src_row * SLAB, SLAB), SLAB), :],
        dst_ref.at[pl.ds(pl.multiple_of(dst_row * SLAB, SLAB), SLAB), :], sem)


def _wait_rows(src_ref, dst_ref, n_rows, sem):
    pltpu.make_async_copy(src_ref.at[pl.ds(0, n_rows * SLAB), :],
                          dst_ref.at[pl.ds(0, n_rows * SLAB), :], sem).wait()


def _dispatch_kernel(idx_ref, src_ref, dst_ref, sem, *, rows):
    def issue(r, c):
        for k in range(2):
            _row_copy(src_ref, r, dst_ref, idx_ref[0, 0, 2 * r + k], sem).start()
        return c

    lax.fori_loop(0, rows, issue, 0)
    for k in range(2):
        _wait_rows(src_ref, dst_ref, rows, sem)


def _dispatch(h_slab, dest, *, rows, cap_rows):
    n_blk = dest.shape[0]
    return pl.pallas_call(
        functools.partial(_dispatch_kernel, rows=rows),
        grid=(n_blk,),
        in_specs=[pl.BlockSpec((1, 1, 2 * rows), lambda i: (i, 0, 0), memory_space=pltpu.SMEM),
                  pl.BlockSpec((rows * SLAB, LANES), lambda i: (i, 0))],
        out_specs=pl.BlockSpec(memory_space=pl.ANY),
        out_shape=jax.ShapeDtypeStruct((cap_rows * SLAB, LANES), F32),
        scratch_shapes=[pltpu.SemaphoreType.DMA(())],
        compiler_params=pltpu.CompilerParams(dimension_semantics=("arbitrary",)),
        name="dispatch",
    )(dest, h_slab)


def _experts_kernel(tblk, texp, tn, xs_ref, wg_ref, wu_ref, wd_ref, o_ref, hb, acc):
    t = pl.program_id(0)
    j = pl.program_id(1)
    n = tn[t]
    tm = hb.shape[0]

    @pl.when(n > 0)
    def _():
        @pl.when(j == 0)
        def _():
            valid = lax.broadcasted_iota(jnp.int32, (tm, LANES), 0) < n
            for c in range(SLAB):
                xc = xs_ref[pl.ds(c, tm, stride=SLAB), :]
                hb[:, c * LANES:(c + 1) * LANES] = jnp.where(valid, xc, 0.0).astype(BF16)

        part = _swiglu_part(hb[...], wg_ref, wu_ref, wd_ref)

        @pl.when(j == 0)
        def _():
            acc[...] = part

        @pl.when(jnp.logical_and(j > 0, j < pl.num_programs(1) - 1))
        def _():
            acc[...] += part

        @pl.when(j == pl.num_programs(1) - 1)
        def _():
            _to_slab(o_ref, acc[...] + part, tm)


def _experts(xs_slab, tblk, texp, tn, wg, wu, wd, *, tm, tf):
    n_tiles = tblk.shape[0]
    n_j = D_FF // tf
    assert n_j >= 2
    jj = lambda t, j, tn: jnp.where(tn[t] > 0, j, n_j - 1)
    grid_spec = pltpu.PrefetchScalarGridSpec(
        num_scalar_prefetch=3,
        grid=(n_tiles, n_j),
        in_specs=[pl.BlockSpec((tm * SLAB, LANES), lambda t, j, tblk, texp, tn: (tblk[t], 0)),
                  pl.BlockSpec((None, D_MODEL, tf),
                               lambda t, j, tblk, texp, tn: (texp[t], 0, jj(t, j, tn))),
                  pl.BlockSpec((None, D_MODEL, tf),
                               lambda t, j, tblk, texp, tn: (texp[t], 0, jj(t, j, tn))),
                  pl.BlockSpec((None, tf, D_MODEL),
                               lambda t, j, tblk, texp, tn: (texp[t], jj(t, j, tn), 0))],
        out_specs=pl.BlockSpec((tm * SLAB, LANES), lambda t, j, tblk, texp, tn: (tblk[t], 0)),
        scratch_shapes=[pltpu.VMEM((tm, D_MODEL), BF16), pltpu.VMEM((tm, D_MODEL), F32)])
    return pl.pallas_call(
        _experts_kernel,
        grid_spec=grid_spec,
        out_shape=jax.ShapeDtypeStruct(xs_slab.shape, F32),
        compiler_params=pltpu.CompilerParams(dimension_semantics=("arbitrary", "arbitrary"),
                                             vmem_limit_bytes=VMEM_LIMIT),
        name="experts",
    )(tblk, texp, tn, xs_slab, wg, wu, wd)


def _combine_kernel(idx_ref, idx_next_ref, x_ref, mf_ref, gf_ref, ys_ref, op_ref, os_ref,
                    buf, ysum, sem, *, rows, n_prompt_tiles):
    i = pl.program_id(0)
    n_i = pl.num_programs(0)
    slot = i % 2

    def issue(ref, s):
        def body(r, c):
            for k in range(2):
                _row_copy(ys_ref, ref[0, 0, 2 * r + k], buf.at[s], k * rows + r, sem.at[s]).start()
            return c
        lax.fori_loop(0, rows, body, 0)

    def finish(s):
        _wait_rows(ys_ref, buf.at[s], 2 * rows, sem.at[s])
        p1 = mf_ref[:, 0:1]
        p2 = mf_ref[:, 1:2]
        for c in range(SLAB):
            cs = slice(c * LANES, (c + 1) * LANES)
            y1 = buf[s, pl.ds(c, rows, stride=SLAB), :]
            y2 = buf[s, pl.ds(rows * SLAB + c, rows, stride=SLAB), :]
            ysum[:, cs] = x_ref[:, cs] + p1 * y1 + p2 * y2
        res = _rms(ysum[...], gf_ref[...])

        @pl.when(i < n_prompt_tiles)
        def _():
            op_ref[...] = res

        @pl.when(i >= n_prompt_tiles)
        def _():
            os_ref[...] = res

    for s in range(2):
        @pl.when(slot == s)
        def _():
            @pl.when(i == 0)
            def _():
                issue(idx_ref, s)

            @pl.when(i + 1 < n_i)
            def _():
                issue(idx_next_ref, 1 - s)

            finish(s)


def _combine(x, mf, g_final, y_slab, dest, *, rows, m_prompt):
    m = x.shape[0]
    n_blk = dest.shape[0]
    assert m == n_blk * rows and m_prompt % rows == 0
    n_p = m_prompt // rows
    idx_spec = lambda f: pl.BlockSpec((1, 1, 2 * rows), f, memory_space=pltpu.SMEM)
    return pl.pallas_call(
        functools.partial(_combine_kernel, rows=rows, n_prompt_tiles=n_p),
        grid=(n_blk,),
        in_specs=[idx_spec(lambda i: (i, 0, 0)),
                  idx_spec(lambda i: (jnp.minimum(i + 1, n_blk - 1), 0, 0)),
                  pl.BlockSpec((rows, D_MODEL), lambda i: (i, 0)),
                  pl.BlockSpec((rows, LANES), lambda i: (i, 0)),
                  pl.BlockSpec((1, D_MODEL), lambda i: (0, 0)),
                  pl.BlockSpec(memory_space=pl.ANY)],
        out_specs=[pl.BlockSpec((rows, D_MODEL), lambda i: (jnp.minimum(i, n_p - 1), 0)),
                   pl.BlockSpec((rows, D_MODEL), lambda i: (jnp.maximum(i - n_p, 0), 0))],
        out_shape=[jax.ShapeDtypeStruct((m_prompt, D_MODEL), F32),
                   jax.ShapeDtypeStruct((m - m_prompt, D_MODEL), F32)],
        scratch_shapes=[pltpu.VMEM((2, 2 * rows * SLAB, LANES), F32),
                        pltpu.VMEM((rows, D_MODEL), F32),
                        pltpu.SemaphoreType.DMA((2,))],
        compiler_params=pltpu.CompilerParams(dimension_semantics=("arbitrary",),
                                             vmem_limit_bytes=VMEM_LIMIT),
        name="combine",
    )(dest, dest, x, mf, g_final, y_slab)


def _expert_tiles(counts, *, tm, cap, n_tiles):
    tiles = (counts + tm - 1) // tm
    ends = jnp.cumsum(tiles)
    t = jnp.arange(n_tiles, dtype=jnp.int32)
    tc = jnp.minimum(t, ends[-1] - 1)
    e = jnp.sum((tc[:, None] >= ends[None, :]).astype(jnp.int32), axis=1)
    k = tc - (ends - tiles)[e]
    tblk = e * (cap // tm) + k
    tn = jnp.where(t < ends[-1], jnp.minimum(counts[e] - k * tm, tm), 0)
    return tblk.astype(jnp.int32), e.astype(jnp.int32), tn.astype(jnp.int32)


def _moe_final(x, g_ffn, wr, br, wg, wu, wd, g_final, *, m_prompt):
    m = x.shape[0]
    tm, tf, out_rows, in_rows = 512, 896, 1024, 256
    cap = -(-m // tm) * tm
    h_slab, mi, mf, cnt = _route(x, g_ffn, wr, br, tm=tm, cap=cap)
    dest = mi[:, :2]
    counts = cnt[0, :N_EXPERTS].astype(jnp.int32)
    n_tiles = 2 * m // tm + N_EXPERTS
    tblk, texp, tn = _expert_tiles(counts, tm=tm, cap=cap, n_tiles=n_tiles)
    xs_slab = _dispatch(h_slab, dest.reshape(m // out_rows, 1, 2 * out_rows), rows=out_rows,
                        cap_rows=N_EXPERTS * cap)
    y_slab = _experts(xs_slab, tblk, texp, tn, wg, wu, wd, tm=tm, tf=tf)
    return _combine(x, mf, g_final, y_slab, dest.reshape(m // in_rows, 1, 2 * in_rows),
                    rows=in_rows, m_prompt=m_prompt)


def _row(v):
    return v.reshape(1, -1)


def kernel(x_prompt, x_sample, state_conv_a, state_conv_b, norm_mix_g, w_in, conv_a_w, conv_a_b, ln_a_g, ln_a_b, conv_b_w, ln_c_g, ln_c_b, w_spatial, b_spatial, w_proj_a, w_proj_b, w_proj_c, w_out, norm_ffn_g, ffn_w_gate, ffn_w_up, ffn_w_down, w_router, b_router, moe_w_gate, moe_w_up, moe_w_down, norm_final_g):
    depth = w_in.shape[0]
    n_p, t_p, _ = x_prompt.shape
    n_s, t_s, _ = x_sample.shape
    seqs_per_chunk = CHUNK // t_s

    zeros_a = jnp.zeros((n_p, A_WIDTH - 1, D_BR), F32)
    zeros_b = jnp.zeros((n_p, B_WIDTH - 1, D_BR), F32)

    xp, xs = x_prompt, x_sample
    outs = {k: [] for k in ("ap", "as", "bp", "bs", "vp", "vs")}
    for l in range(depth):
        bsp_p = jnp.repeat(b_spatial[l].T, LANES, axis=1)
        eye = jnp.eye(seqs_per_chunk, dtype=F32)
        wsp_s = jnp.einsum("ab,hts->hatbs", eye, w_spatial[l][:, :t_s, :t_s]).reshape(
            C_HEADS, CHUNK, CHUNK)
        bsp_s = jnp.tile(bsp_p[:t_s], (seqs_per_chunk, 1))
        w = dict(g=_row(norm_mix_g[l]), w_in=w_in[l].astype(BF16), conv_a_w=conv_a_w[l],
                 conv_a_b=_row(conv_a_b[l]), ln_a_g=_row(ln_a_g[l]), ln_a_b=_row(ln_a_b[l]),
                 conv_b_w=conv_b_w[l], ln_c_g=_row(ln_c_g[l]), ln_c_b=_row(ln_c_b[l]),
                 w_proj_a=w_proj_a[l].astype(BF16), w_proj_b=w_proj_b[l].astype(BF16),
                 w_proj_c=w_proj_c[l].astype(BF16), w_out=w_out[l].astype(BF16))
        xp, a_p, b_p, v_p = _mixer(xp, zeros_a, zeros_b, dict(w, wsp=w_spatial[l], bsp=bsp_p),
                                   nb=1, tt=512)
        xs, a_s, b_s, v_s = _mixer(xs, state_conv_a[l], state_conv_b[l],
                                   dict(w, wsp=wsp_s, bsp=bsp_s), nb=16, tt=t_s)
        for k, val in zip(("ap", "as", "bp", "bs", "vp", "vs"), (a_p, a_s, b_p, b_s, v_p, v_s)):
            outs[k].append(val)

        i = l // 2
        g_ffn = _row(norm_ffn_g[l])
        g_fin = _row(norm_final_g)
        xp = xp.reshape(n_p * t_p, D_MODEL)
        xs = xs.reshape(n_s * t_s, D_MODEL)
        if l % 2 == 0:
            args = (g_ffn, ffn_w_gate[i].astype(BF16), ffn_w_up[i].astype(BF16),
                    ffn_w_down[i].astype(BF16))
            xp = _ffn(xp, *args, tm=1024, tf=896)
            xs = _ffn(xs, *args, tm=1024, tf=896)
        else:
            assert l == depth - 1, "the routed layer applies the final norm"
            wr = jnp.pad(w_router[i], ((0, 0), (0, LANES - N_EXPERTS)))
            br = jnp.pad(_row(b_router[i]), ((0, 0), (0, LANES - N_EXPERTS)))
            xp, xs = _moe_final(jnp.concatenate([xp, xs], axis=0), g_ffn, wr, br,
                                moe_w_gate[i].astype(BF16), moe_w_up[i].astype(BF16),
                                moe_w_down[i].astype(BF16), g_fin, m_prompt=n_p * t_p)
        xp = xp.reshape(n_p, t_p, D_MODEL)
        xs = xs.reshape(n_s, t_s, D_MODEL)

    st = lambda k: jnp.stack(outs[k])
    return (xp, xs, st("ap"), st("as"), st("bp"), st("bs"), st("vp"), st("vs"))
```

```python
import functools

import jax
import jax.numpy as jnp
from jax import lax
from jax.experimental import pallas as pl
from jax.experimental.pallas import tpu as pltpu

D_MODEL = 1024
D_BR = 512
A_WIDTH = 31
B_WIDTH = 3
C_HEADS = 4
CHUNK = 128
D_FF = 3584
N_EXPERTS = 8
EPS = 1e-6

A_HIST = 32
B_HIST = 8
LANES = 128
SUBLANES = 8
CONV_ROWS = 32
VMEM_LIMIT = 56 * 1024 * 1024

BF16 = jnp.bfloat16
F32 = jnp.float32


def _rms(x, g):
    return x * lax.rsqrt(jnp.mean(x * x, -1, keepdims=True) + EPS) * g


def _layer_norm(x, g, b):
    mu = jnp.mean(x, -1, keepdims=True)
    xc = x - mu
    var = jnp.mean(xc * xc, -1, keepdims=True)
    return xc * lax.rsqrt(var + EPS) * g + b


def _sigmoid(x):
    return 0.5 * jnp.tanh(0.5 * x) + 0.5


def _dot(a, b):
    return jnp.dot(a, b, preferred_element_type=F32)


def _mixer_kernel(x_ref, pa_ref, pb_ref, g_ref, win_ref, caw_ref, cab_ref, lag_ref, lab_ref,
                  cbw_ref, lcg_ref, lcb_ref, wsp_ref, bsp_ref, wpa_ref, wpb_ref, wpc_ref,
                  wout_ref, xo_ref, na_ref, nb_ref, nv_ref, fa, fz, sc, sh, ca, wb, *, nb, tt, nv_rows):
    i = pl.program_id(1)
    rows = nb * tt

    @pl.when(i == 0)
    def _():
        fa[:, A_HIST - (A_WIDTH - 1):A_HIST, :] = pa_ref[...]
        fz[:, B_HIST - (B_WIDTH - 1):B_HIST, :] = pb_ref[...]

    x = x_ref[...]
    hb = _rms(x, g_ref[...]).astype(BF16)

    def proj(c0, c1):
        return _dot(hb, win_ref[:, c0:c1])

    a = proj(0, D_BR) * _sigmoid(proj(D_BR, 2 * D_BR))
    fa[:, A_HIST:A_HIST + tt, :] = a.reshape(nb, tt, D_BR)
    off_a = A_HIST - (A_WIDTH - 1)
    for j in range(SUBLANES):
        span = tt + SUBLANES * ((A_WIDTH - 1 - j) // SUBLANES)
        sh[j, :, 0:span, :] = fa[:, off_a + j:off_a + j + span, :]
    for k in range(A_WIDTH):
        wb[k] = jnp.broadcast_to(caw_ref[k:k + 1, :], (SUBLANES, D_BR))
    rb = min(tt, CONV_ROWS)
    sb = CONV_ROWS // rb
    n_rb = tt // rb

    def conv_step(u, carry):
        s0 = (u // n_rb) * sb
        r0 = (u % n_rb) * rb
        groups = range(0, rb, SUBLANES)
        acc = [jnp.broadcast_to(cab_ref[...], (sb, SUBLANES, D_BR)) for _ in groups]
        for k in range(A_WIDTH):
            q, j = divmod(k, SUBLANES)
            w_k = wb[k]
            for gi, g in enumerate(groups):
                lo = r0 + g + SUBLANES * q
                acc[gi] = acc[gi] + w_k * sh[j, s0:s0 + sb, lo:lo + SUBLANES, :]
        for gi, g in enumerate(groups):
            ca[s0:s0 + sb, r0 + g:r0 + g + SUBLANES, :] = acc[gi]
        return carry

    for u in range((nb // sb) * n_rb):
        conv_step(u, 0)
    a_ln = _layer_norm(ca[...].reshape(rows, D_BR), lag_ref[...], lab_ref[...])
    a_out = _dot((a_ln * _sigmoid(a_ln)).astype(BF16), wpa_ref[...])
    na_ref[...] = fa[:, tt + A_HIST - (A_WIDTH - 1):tt + A_HIST, :]
    fa[:, 0:A_HIST, :] = fa[:, tt:tt + A_HIST, :]

    c0 = 2 * D_BR
    z = proj(c0 + 2 * D_BR, c0 + 3 * D_BR) * proj(c0, c0 + D_BR)
    fz[:, B_HIST:B_HIST + tt, :] = z.reshape(nb, tt, D_BR)
    off_b = B_HIST - (B_WIDTH - 1)
    accb = jnp.zeros((nb, tt, D_BR), F32)
    for k in range(B_WIDTH):
        accb = accb + cbw_ref[k:k + 1, :] * fz[:, off_b + k:off_b + k + tt, :]
    b_out = _dot((proj(c0 + D_BR, c0 + 2 * D_BR) * accb.reshape(rows, D_BR)).astype(BF16),
                 wpb_ref[...])
    nb_ref[...] = fz[:, tt + B_HIST - (B_WIDTH - 1):tt + B_HIST, :]
    fz[:, 0:B_HIST, :] = fz[:, tt:tt + B_HIST, :]

    c0 = 5 * D_BR
    v = _layer_norm(proj(c0 + D_BR, c0 + 2 * D_BR), lcg_ref[...], lcb_ref[...])
    nv_ref[...] = v[rows - nv_rows * nb:, :].reshape(nb, nv_rows, D_BR)
    vb = v.astype(BF16)
    t_idx = lax.broadcasted_iota(jnp.int32, (CHUNK, CHUNK), 0)
    s_idx = lax.broadcasted_iota(jnp.int32, (CHUNK, CHUNK), 1)
    causal = s_idx <= t_idx
    for hd in range(C_HEADS):
        wm = jnp.where(causal, wsp_ref[hd], 0.0).astype(BF16)
        for c in range(rows // CHUNK):
            blk = (slice(c * CHUNK, (c + 1) * CHUNK), slice(hd * LANES, (hd + 1) * LANES))
            sc[blk] = _dot(wm, vb[blk])
    c_out = _dot((proj(c0, c0 + D_BR) * (sc[...].reshape(rows // CHUNK, CHUNK, D_BR)
                                        + bsp_ref[...]).reshape(rows, D_BR)).astype(BF16),
                 wpc_ref[...])

    c0 = 7 * D_BR
    merged = (_sigmoid(proj(c0, c0 + D_MODEL)) * a_out
              + _sigmoid(proj(c0 + D_MODEL, c0 + 2 * D_MODEL)) * b_out
              + _sigmoid(proj(c0 + 2 * D_MODEL, c0 + 3 * D_MODEL)) * c_out)
    xo_ref[...] = x + _dot(merged.astype(BF16), wout_ref[...])


def _const_spec(shape):
    nd = len(shape)
    return pl.BlockSpec(shape, lambda b, i: (0,) * nd, pipeline_mode=pl.Buffered(1))


def _drop_ref(kernel_fn, pos):
    def body(*refs):
        return kernel_fn(*refs[:pos], *refs[pos + 1:])
    return body


def _mixer(x, past_a, past_b, w, *, t, nb, tt, in_row, out_row, out_rows, out_buf=None):
    n = past_a.shape[0]
    rows = nb * tt
    assert n % nb == 0 and t % tt == 0 and rows % CHUNK == 0
    assert in_row % rows == 0 and out_row % rows == 0
    assert tt % SUBLANES == 0 and (tt % CONV_ROWS == 0 or
                                   (CONV_ROWS % tt == 0 and nb % (CONV_ROWS // tt) == 0))
    nv_rows = t - ((t - 1) // CHUNK) * CHUNK
    assert nv_rows <= tt
    n_t = t // tt
    consts = [w["g"], w["w_in"], w["conv_a_w"], w["conv_a_b"], w["ln_a_g"], w["ln_a_b"],
              w["conv_b_w"], w["ln_c_g"], w["ln_c_b"], w["wsp"], w["bsp"], w["w_proj_a"],
              w["w_proj_b"], w["w_proj_c"], w["w_out"]]
    seq_spec = lambda r, c: pl.BlockSpec((nb, r, c), lambda b, i: (b, 0, 0))
    row_spec = lambda first: pl.BlockSpec((rows, D_MODEL),
                                          lambda b, i: (first // rows + b * n_t + i, 0))
    body = functools.partial(_mixer_kernel, nb=nb, tt=tt, nv_rows=nv_rows)
    operands = [x, past_a, past_b, *consts]
    in_specs = ([row_spec(in_row), seq_spec(A_WIDTH - 1, D_BR), seq_spec(B_WIDTH - 1, D_BR)]
                + [_const_spec(c.shape) for c in consts])
    aliases = {}
    if out_buf is not None:
        assert out_buf.shape == (out_rows, D_MODEL)
        aliases = {len(operands): 0}
        body = _drop_ref(body, len(operands))
        operands.append(out_buf)
        in_specs.append(pl.BlockSpec(memory_space=pl.ANY))
    return pl.pallas_call(
        body,
        grid=(n // nb, n_t),
        in_specs=in_specs,
        out_specs=[row_spec(out_row),
                   seq_spec(A_WIDTH - 1, D_BR), seq_spec(B_WIDTH - 1, D_BR),
                   seq_spec(nv_rows, D_BR)],
        input_output_aliases=aliases,
        out_shape=[jax.ShapeDtypeStruct((out_rows, D_MODEL), F32),
                   jax.ShapeDtypeStruct((n, A_WIDTH - 1, D_BR), F32),
                   jax.ShapeDtypeStruct((n, B_WIDTH - 1, D_BR), F32),
                   jax.ShapeDtypeStruct((n, nv_rows, D_BR), F32)],
        scratch_shapes=[pltpu.VMEM((nb, A_HIST + tt, D_BR), F32),
                        pltpu.VMEM((nb, B_HIST + tt, D_BR), F32),
                        pltpu.VMEM((nb * tt, D_BR), F32),
                        pltpu.VMEM((SUBLANES, nb, tt + A_HIST - SUBLANES, D_BR), F32),
                        pltpu.VMEM((nb, tt, D_BR), F32),
                        pltpu.VMEM((A_WIDTH, SUBLANES, D_BR), F32)],
        compiler_params=pltpu.CompilerParams(
            dimension_semantics=("arbitrary", "arbitrary"), vmem_limit_bytes=VMEM_LIMIT),
        name="mixer",
    )(*operands)


def _swiglu_part(h_b, wg_ref, wu_ref, wd_ref):
    gate = _dot(h_b, wg_ref[...])
    act = gate * _sigmoid(gate) * _dot(h_b, wu_ref[...])
    return _dot(act.astype(BF16), wd_ref[...])


def _ffn_kernel(x_ref, g_ref, wg_ref, wu_ref, wd_ref, o_ref, hb, acc):
    j = pl.program_id(1)

    @pl.when(j == 0)
    def _():
        hb[...] = _rms(x_ref[...], g_ref[...]).astype(BF16)
        acc[...] = x_ref[...]

    acc[...] += _swiglu_part(hb[...], wg_ref, wu_ref, wd_ref)

    @pl.when(j == pl.num_programs(1) - 1)
    def _():
        o_ref[...] = acc[...]


def _ffn(x, g, wg, wu, wd, *, tm, tf):
    m, _ = x.shape
    assert m % tm == 0 and D_FF % tf == 0
    row = lambda i, j: (i, 0)
    return pl.pallas_call(
        _ffn_kernel,
        grid=(m // tm, D_FF // tf),
        in_specs=[pl.BlockSpec((tm, D_MODEL), row),
                  pl.BlockSpec((1, D_MODEL), lambda i, j: (0, 0)),
                  pl.BlockSpec((D_MODEL, tf), lambda i, j: (0, j)),
                  pl.BlockSpec((D_MODEL, tf), lambda i, j: (0, j)),
                  pl.BlockSpec((tf, D_MODEL), lambda i, j: (j, 0))],
        out_specs=pl.BlockSpec((tm, D_MODEL), row),
        out_shape=jax.ShapeDtypeStruct((m, D_MODEL), F32),
        scratch_shapes=[pltpu.VMEM((tm, D_MODEL), BF16), pltpu.VMEM((tm, D_MODEL), F32)],
        compiler_params=pltpu.CompilerParams(
            dimension_semantics=("arbitrary", "arbitrary"), vmem_limit_bytes=VMEM_LIMIT),
        name="ffn",
    )(x, g, wg, wu, wd)


SLAB = D_MODEL // LANES


def _to_slab(ref, val, rows):
    for c in range(SLAB):
        ref[pl.ds(c, rows, stride=SLAB), :] = val[:, c * LANES:(c + 1) * LANES]


def _route_kernel(x_ref, g_ref, wr_ref, br_ref, hs_ref, mi_ref, mf_ref, cnt_ref, carry, *, cap):
    i = pl.program_id(0)
    tm = x_ref.shape[0]

    @pl.when(i == 0)
    def _():
        carry[...] = jnp.zeros_like(carry)

    h = _rms(x_ref[...], g_ref[...])
    _to_slab(hs_ref, h, tm)

    col = lax.broadcasted_iota(jnp.int32, (tm, LANES), 1)
    logits = jnp.dot(h, wr_ref[...], preferred_element_type=F32,
                     precision=lax.Precision.HIGHEST) + br_ref[...]
    logits = jnp.where(col < N_EXPERTS, logits, -jnp.inf)
    m1 = jnp.max(logits, -1, keepdims=True)
    i1 = jnp.min(jnp.where(logits == m1, col, LANES), -1, keepdims=True)
    rest = jnp.where(col == i1, -jnp.inf, logits)
    m2 = jnp.max(rest, -1, keepdims=True)
    i2 = jnp.min(jnp.where(rest == m2, col, LANES), -1, keepdims=True)
    e2 = jnp.exp(m2 - m1)
    p1 = 1.0 / (1.0 + e2)
    p2 = e2 * p1

    chosen = jnp.logical_or(col == i1, col == i2).astype(F32)
    r_idx = lax.broadcasted_iota(jnp.int32, (tm, tm), 0)
    c_idx = lax.broadcasted_iota(jnp.int32, (tm, tm), 1)
    before = (c_idx < r_idx).astype(BF16)
    rank = carry[...] + _dot(before, chosen.astype(BF16))
    dest = col.astype(F32) * float(cap) + rank
    d1 = jnp.sum(jnp.where(col == i1, dest, 0.0), -1, keepdims=True).astype(jnp.int32)
    d2 = jnp.sum(jnp.where(col == i2, dest, 0.0), -1, keepdims=True).astype(jnp.int32)
    mi_ref[...] = jnp.where(col == 0, d1, jnp.where(col == 1, d2, 0))
    mf_ref[...] = jnp.where(col == 0, p1, jnp.where(col == 1, p2, 0.0))
    carry[...] += jnp.sum(chosen, 0, keepdims=True)
    cnt_ref[...] = carry[...]


def _route(x, g, wr, br, *, tm, cap):
    m = x.shape[0]
    assert m % tm == 0
    row = lambda i: (i, 0)
    const = lambda i: (0, 0)
    return pl.pallas_call(
        functools.partial(_route_kernel, cap=cap),
        grid=(m // tm,),
        in_specs=[pl.BlockSpec((tm, D_MODEL), row), pl.BlockSpec((1, D_MODEL), const),
                  pl.BlockSpec((D_MODEL, LANES), const), pl.BlockSpec((1, LANES), const)],
        out_specs=[pl.BlockSpec((tm * SLAB, LANES), row), pl.BlockSpec((tm, LANES), row),
                   pl.BlockSpec((tm, LANES), row), pl.BlockSpec((1, LANES), const)],
        out_shape=[jax.ShapeDtypeStruct((m * SLAB, LANES), F32),
                   jax.ShapeDtypeStruct((m, LANES), jnp.int32),
                   jax.ShapeDtypeStruct((m, LANES), F32),
                   jax.ShapeDtypeStruct((1, LANES), F32)],
        scratch_shapes=[pltpu.VMEM((1, LANES), F32)],
        compiler_params=pltpu.CompilerParams(dimension_semantics=("arbitrary",),
                                             vmem_limit_bytes=VMEM_LIMIT),
        name="route",
    )(x, g, wr, br)


def _row_copy(src_ref, src_row, dst_ref, dst_row, sem):
    return pltpu.make_async_copy(
        src_ref.at[pl.ds(pl.multiple_of(src_row * SLAB, SLAB), SLAB), :],
        dst_ref.at[pl.ds(pl.multiple_of(dst_row * SLAB, SLAB), SLAB), :], sem)


def _wait_rows(src_ref, dst_ref, n_rows, sem):
    pltpu.make_async_copy(src_ref.at[pl.ds(0, n_rows * SLAB), :],
                          dst_ref.at[pl.ds(0, n_rows * SLAB), :], sem).wait()


def _dispatch_kernel(idx_ref, src_ref, dst_ref, sem, *, rows):
    def issue(r, c):
        for k in range(2):
            _row_copy(src_ref, r, dst_ref, idx_ref[0, 0, 2 * r + k], sem).start(priority=k)
        return c

    lax.fori_loop(0, rows, issue, 0)
    for k in range(2):
        _wait_rows(src_ref, dst_ref, rows, sem)


def _dispatch(h_slab, dest, *, rows, cap_rows):
    n_blk = dest.shape[0]
    return pl.pallas_call(
        functools.partial(_dispatch_kernel, rows=rows),
        grid=(n_blk,),
        in_specs=[pl.BlockSpec((1, 1, 2 * rows), lambda i: (i, 0, 0), memory_space=pltpu.SMEM),
                  pl.BlockSpec((rows * SLAB, LANES), lambda i: (i, 0))],
        out_specs=pl.BlockSpec(memory_space=pl.ANY),
        out_shape=jax.ShapeDtypeStruct((cap_rows * SLAB, LANES), F32),
        scratch_shapes=[pltpu.SemaphoreType.DMA(())],
        compiler_params=pltpu.CompilerParams(dimension_semantics=("arbitrary",)),
        name="dispatch",
    )(dest, h_slab)


def _experts_kernel(tblk, texp, tn, xs_ref, wg_ref, wu_ref, wd_ref, o_ref, hb, acc):
    t = pl.program_id(0)
    j = pl.program_id(1)
    n = tn[t]
    tm = hb.shape[0]

    @pl.when(n > 0)
    def _():
        @pl.when(j == 0)
        def _():
            valid = lax.broadcasted_iota(jnp.int32, (tm, LANES), 0) < n
            for c in range(SLAB):
                xc = xs_ref[pl.ds(c, tm, stride=SLAB), :]
                hb[:, c * LANES:(c + 1) * LANES] = jnp.where(valid, xc, 0.0).astype(BF16)

        part = _swiglu_part(hb[...], wg_ref, wu_ref, wd_ref)

        @pl.when(j == 0)
        def _():
            acc[...] = part

        @pl.when(jnp.logical_and(j > 0, j < pl.num_programs(1) - 1))
        def _():
            acc[...] += part

        @pl.when(j == pl.num_programs(1) - 1)
        def _():
            _to_slab(o_ref, acc[...] + part, tm)


def _experts(xs_slab, tblk, texp, tn, wg, wu, wd, *, tm, tf):
    n_tiles = tblk.shape[0]
    n_j = D_FF // tf
    assert n_j >= 2
    jj = lambda t, j, tn: jnp.where(tn[t] > 0, j, n_j - 1)
    grid_spec = pltpu.PrefetchScalarGridSpec(
        num_scalar_prefetch=3,
        grid=(n_tiles, n_j),
        in_specs=[pl.BlockSpec((tm * SLAB, LANES), lambda t, j, tblk, texp, tn: (tblk[t], 0)),
                  pl.BlockSpec((None, D_MODEL, tf),
                               lambda t, j, tblk, texp, tn: (texp[t], 0, jj(t, j, tn))),
                  pl.BlockSpec((None, D_MODEL, tf),
                               lambda t, j, tblk, texp, tn: (texp[t], 0, jj(t, j, tn))),
                  pl.BlockSpec((None, tf, D_MODEL),
                               lambda t, j, tblk, texp, tn: (texp[t], jj(t, j, tn), 0))],
        out_specs=pl.BlockSpec((tm * SLAB, LANES), lambda t, j, tblk, texp, tn: (tblk[t], 0)),
        scratch_shapes=[pltpu.VMEM((tm, D_MODEL), BF16), pltpu.VMEM((tm, D_MODEL), F32)])
    return pl.pallas_call(
        _experts_kernel,
        grid_spec=grid_spec,
        out_shape=jax.ShapeDtypeStruct(xs_slab.shape, F32),
        compiler_params=pltpu.CompilerParams(dimension_semantics=("arbitrary", "arbitrary"),
                                             vmem_limit_bytes=VMEM_LIMIT),
        name="experts",
    )(tblk, texp, tn, xs_slab, wg, wu, wd)


def _combine_kernel(idx_ref, idx_next_ref, x_ref, mf_ref, gf_ref, ys_ref, op_ref, os_ref,
                    buf, ysum, sem, *, rows, n_prompt_tiles):
    i = pl.program_id(0)
    n_i = pl.num_programs(0)
    slot = i % 2

    def issue(ref, s):
        def body(r, c):
            for k in range(2):
                _row_copy(ys_ref, ref[0, 0, 2 * r + k], buf.at[s], k * rows + r,
                          sem.at[s]).start(priority=k)
            return c
        lax.fori_loop(0, rows, body, 0)

    def finish(s):
        _wait_rows(ys_ref, buf.at[s], 2 * rows, sem.at[s])
        p1 = mf_ref[:, 0:1]
        p2 = mf_ref[:, 1:2]
        for c in range(SLAB):
            cs = slice(c * LANES, (c + 1) * LANES)
            y1 = buf[s, pl.ds(c, rows, stride=SLAB), :]
            y2 = buf[s, pl.ds(rows * SLAB + c, rows, stride=SLAB), :]
            ysum[:, cs] = x_ref[:, cs] + p1 * y1 + p2 * y2
        res = _rms(ysum[...], gf_ref[...])

        @pl.when(i < n_prompt_tiles)
        def _():
            op_ref[...] = res

        @pl.when(i >= n_prompt_tiles)
        def _():
            os_ref[...] = res

    for s in range(2):
        @pl.when(slot == s)
        def _():
            @pl.when(i == 0)
            def _():
                issue(idx_ref, s)

            @pl.when(i + 1 < n_i)
            def _():
                issue(idx_next_ref, 1 - s)

            finish(s)


def _combine(x, mf, g_final, y_slab, dest, *, rows, m_prompt):
    m = x.shape[0]
    n_blk = dest.shape[0]
    assert m == n_blk * rows and m_prompt % rows == 0
    n_p = m_prompt // rows
    idx_spec = lambda f: pl.BlockSpec((1, 1, 2 * rows), f, memory_space=pltpu.SMEM)
    return pl.pallas_call(
        functools.partial(_combine_kernel, rows=rows, n_prompt_tiles=n_p),
        grid=(n_blk,),
        in_specs=[idx_spec(lambda i: (i, 0, 0)),
                  idx_spec(lambda i: (jnp.minimum(i + 1, n_blk - 1), 0, 0)),
                  pl.BlockSpec((rows, D_MODEL), lambda i: (i, 0)),
                  pl.BlockSpec((rows, LANES), lambda i: (i, 0)),
                  pl.BlockSpec((1, D_MODEL), lambda i: (0, 0)),
                  pl.BlockSpec(memory_space=pl.ANY)],
        out_specs=[pl.BlockSpec((rows, D_MODEL), lambda i: (jnp.minimum(i, n_p - 1), 0)),
                   pl.BlockSpec((rows, D_MODEL), lambda i: (jnp.maximum(i - n_p, 0), 0))],
        out_shape=[jax.ShapeDtypeStruct((m_prompt, D_MODEL), F32),
                   jax.ShapeDtypeStruct((m - m_prompt, D_MODEL), F32)],
        scratch_shapes=[pltpu.VMEM((2, 2 * rows * SLAB, LANES), F32),
                        pltpu.VMEM((rows, D_MODEL), F32),
                        pltpu.SemaphoreType.DMA((2,))],
        compiler_params=pltpu.CompilerParams(dimension_semantics=("arbitrary",),
                                             vmem_limit_bytes=VMEM_LIMIT),
        name="combine",
    )(dest, dest, x, mf, g_final, y_slab)


def _expert_tiles(counts, *, tm, cap, n_tiles):
    tiles = (counts + tm - 1) // tm
    ends = jnp.cumsum(tiles)
    t = jnp.arange(n_tiles, dtype=jnp.int32)
    tc = jnp.minimum(t, ends[-1] - 1)
    e = jnp.sum((tc[:, None] >= ends[None, :]).astype(jnp.int32), axis=1)
    k = tc - (ends - tiles)[e]
    tblk = e * (cap // tm) + k
    tn = jnp.where(t < ends[-1], jnp.minimum(counts[e] - k * tm, tm), 0)
    return tblk.astype(jnp.int32), e.astype(jnp.int32), tn.astype(jnp.int32)


def _moe_final(x, g_ffn, wr, br, wg, wu, wd, g_final, *, m_prompt):
    m = x.shape[0]
    route_rows, tm, tf, out_rows, in_rows = 512, 512, 896, 1024, 256
    cap = -(-m // tm) * tm
    h_slab, mi, mf, cnt = _route(x, g_ffn, wr, br, tm=route_rows, cap=cap)
    dest = mi[:, :2]
    counts = cnt[0, :N_EXPERTS].astype(jnp.int32)
    n_tiles = 2 * m // tm + N_EXPERTS
    tblk, texp, tn = _expert_tiles(counts, tm=tm, cap=cap, n_tiles=n_tiles)
    xs_slab = _dispatch(h_slab, dest.reshape(m // out_rows, 1, 2 * out_rows), rows=out_rows,
                        cap_rows=N_EXPERTS * cap)
    y_slab = _experts(xs_slab, tblk, texp, tn, wg, wu, wd, tm=tm, tf=tf)
    return _combine(x, mf, g_final, y_slab, dest.reshape(m // in_rows, 1, 2 * in_rows),
                    rows=in_rows, m_prompt=m_prompt)


def _row(v):
    return v.reshape(1, -1)


def kernel(x_prompt, x_sample, state_conv_a, state_conv_b, norm_mix_g, w_in, conv_a_w, conv_a_b, ln_a_g, ln_a_b, conv_b_w, ln_c_g, ln_c_b, w_spatial, b_spatial, w_proj_a, w_proj_b, w_proj_c, w_out, norm_ffn_g, ffn_w_gate, ffn_w_up, ffn_w_down, w_router, b_router, moe_w_gate, moe_w_up, moe_w_down, norm_final_g):
    depth = w_in.shape[0]
    n_p, t_p, _ = x_prompt.shape
    n_s, t_s, _ = x_sample.shape
    seqs_per_chunk = CHUNK // t_s

    zeros_a = jnp.zeros((n_p, A_WIDTH - 1, D_BR), F32)
    zeros_b = jnp.zeros((n_p, B_WIDTH - 1, D_BR), F32)

    m_p, m_s = n_p * t_p, n_s * t_s
    m = m_p + m_s
    x_p, row_p = x_prompt.reshape(m_p, D_MODEL), 0
    x_s, row_s = x_sample.reshape(m_s, D_MODEL), 0
    outs = {k: [] for k in ("ap", "as", "bp", "bs", "vp", "vs")}
    for l in range(depth):
        bsp_p = jnp.repeat(b_spatial[l].T, LANES, axis=1)
        eye = jnp.eye(seqs_per_chunk, dtype=F32)
        wsp_s = jnp.einsum("ab,hts->hatbs", eye, w_spatial[l][:, :t_s, :t_s]).reshape(
            C_HEADS, CHUNK, CHUNK)
        bsp_s = jnp.tile(bsp_p[:t_s], (seqs_per_chunk, 1))
        w = dict(g=_row(norm_mix_g[l]), w_in=w_in[l].astype(BF16), conv_a_w=conv_a_w[l],
                 conv_a_b=_row(conv_a_b[l]), ln_a_g=_row(ln_a_g[l]), ln_a_b=_row(ln_a_b[l]),
                 conv_b_w=conv_b_w[l], ln_c_g=_row(ln_c_g[l]), ln_c_b=_row(ln_c_b[l]),
                 w_proj_a=w_proj_a[l].astype(BF16), w_proj_b=w_proj_b[l].astype(BF16),
                 w_proj_c=w_proj_c[l].astype(BF16), w_out=w_out[l].astype(BF16))
        x, a_p, b_p, v_p = _mixer(x_p, zeros_a, zeros_b, dict(w, wsp=w_spatial[l], bsp=bsp_p),
                                  t=t_p, nb=1, tt=512, in_row=row_p, out_row=0, out_rows=m)
        x, a_s, b_s, v_s = _mixer(x_s, state_conv_a[l], state_conv_b[l],
                                  dict(w, wsp=wsp_s, bsp=bsp_s), t=t_s, nb=16, tt=t_s,
                                  in_row=row_s, out_row=m_p, out_rows=m, out_buf=x)
        for k, val in zip(("ap", "as", "bp", "bs", "vp", "vs"), (a_p, a_s, b_p, b_s, v_p, v_s)):
            outs[k].append(val)

        i = l // 2
        g_ffn = _row(norm_ffn_g[l])
        if l % 2 == 0:
            x = _ffn(x, g_ffn, ffn_w_gate[i].astype(BF16), ffn_w_up[i].astype(BF16),
                     ffn_w_down[i].astype(BF16), tm=1024, tf=896)
            x_p, row_p, x_s, row_s = x, 0, x, m_p
        else:
            assert l == depth - 1, "the routed layer applies the final norm"
            wr = jnp.pad(w_router[i], ((0, 0), (0, LANES - N_EXPERTS)))
            br = jnp.pad(_row(b_router[i]), ((0, 0), (0, LANES - N_EXPERTS)))
            y_p, y_s = _moe_final(x, g_ffn, wr, br, moe_w_gate[i].astype(BF16),
                                  moe_w_up[i].astype(BF16), moe_w_down[i].astype(BF16),
                                  _row(norm_final_g), m_prompt=m_p)
    y_p = y_p.reshape(n_p, t_p, D_MODEL)
    y_s = y_s.reshape(n_s, t_s, D_MODEL)

    st = lambda k: jnp.stack(outs[k])
    return (y_p, y_s, st("ap"), st("as"), st("bp"), st("bs"), st("vp"), st("vs"))
```

```python
import functools

import jax
import jax.numpy as jnp
from jax import lax
from jax.experimental import pallas as pl
from jax.experimental.pallas import tpu as pltpu

D_MODEL = 1024
D_BR = 512
A_WIDTH = 31
B_WIDTH = 3
C_HEADS = 4
CHUNK = 128
D_FF = 3584
N_EXPERTS = 8
EPS = 1e-6

A_HIST = 32
B_HIST = 8
LANES = 128
SUBLANES = 8
CONV_ROWS = 32
VMEM_LIMIT = 56 * 1024 * 1024

BF16 = jnp.bfloat16
F32 = jnp.float32


def _rms(x, g):
    return x * lax.rsqrt(jnp.mean(x * x, -1, keepdims=True) + EPS) * g


def _layer_norm(x, g, b):
    mu = jnp.mean(x, -1, keepdims=True)
    xc = x - mu
    var = jnp.mean(xc * xc, -1, keepdims=True)
    return xc * lax.rsqrt(var + EPS) * g + b


def _sigmoid(x):
    return 0.5 * jnp.tanh(0.5 * x) + 0.5


def _dot(a, b):
    return jnp.dot(a, b, preferred_element_type=F32)


def _dot_split(a, b):
    a_hi = a.astype(BF16)
    b_hi = b.astype(BF16)
    a_lo = (a - a_hi.astype(F32)).astype(BF16)
    b_lo = (b - b_hi.astype(F32)).astype(BF16)
    return _dot(a_hi, b_hi) + (_dot(a_hi, b_lo) + _dot(a_lo, b_hi))


def _mixer_kernel(x_ref, pa_ref, pb_ref, g_ref, win_ref, caw_ref, cab_ref, lag_ref, lab_ref,
                  cbw_ref, lcg_ref, lcb_ref, wsp_ref, bsp_ref, wpa_ref, wpb_ref, wpc_ref,
                  wout_ref, xo_ref, na_ref, nb_ref, nv_ref, fa, fz, sc, sh, ca, wb, *, nb, tt, nv_rows):
    i = pl.program_id(1)
    rows = nb * tt

    @pl.when(i == 0)
    def _():
        fa[:, A_HIST - (A_WIDTH - 1):A_HIST, :] = pa_ref[...]
        fz[:, B_HIST - (B_WIDTH - 1):B_HIST, :] = pb_ref[...]

    x = x_ref[...]
    hb = _rms(x, g_ref[...]).astype(BF16)

    def proj(c0, c1):
        return _dot(hb, win_ref[:, c0:c1])

    a = proj(0, D_BR) * _sigmoid(proj(D_BR, 2 * D_BR))
    fa[:, A_HIST:A_HIST + tt, :] = a.reshape(nb, tt, D_BR)
    off_a = A_HIST - (A_WIDTH - 1)
    for j in range(SUBLANES):
        span = tt + SUBLANES * ((A_WIDTH - 1 - j) // SUBLANES)
        sh[j, :, 0:span, :] = fa[:, off_a + j:off_a + j + span, :]
    for k in range(A_WIDTH):
        wb[k] = jnp.broadcast_to(caw_ref[k:k + 1, :], (SUBLANES, D_BR))
    rb = min(tt, CONV_ROWS)
    sb = CONV_ROWS // rb
    n_rb = tt // rb

    def conv_step(u, carry):
        s0 = (u // n_rb) * sb
        r0 = (u % n_rb) * rb
        groups = range(0, rb, SUBLANES)
        acc = [jnp.broadcast_to(cab_ref[...], (sb, SUBLANES, D_BR)) for _ in groups]
        for k in range(A_WIDTH):
            q, j = divmod(k, SUBLANES)
            w_k = wb[k]
            for gi, g in enumerate(groups):
                lo = r0 + g + SUBLANES * q
                acc[gi] = acc[gi] + w_k * sh[j, s0:s0 + sb, lo:lo + SUBLANES, :]
        for gi, g in enumerate(groups):
            ca[s0:s0 + sb, r0 + g:r0 + g + SUBLANES, :] = acc[gi]
        return carry

    for u in range((nb // sb) * n_rb):
        conv_step(u, 0)
    a_ln = _layer_norm(ca[...].reshape(rows, D_BR), lag_ref[...], lab_ref[...])
    a_out = _dot((a_ln * _sigmoid(a_ln)).astype(BF16), wpa_ref[...])
    na_ref[...] = fa[:, tt + A_HIST - (A_WIDTH - 1):tt + A_HIST, :]
    fa[:, 0:A_HIST, :] = fa[:, tt:tt + A_HIST, :]

    c0 = 2 * D_BR
    z = proj(c0 + 2 * D_BR, c0 + 3 * D_BR) * proj(c0, c0 + D_BR)
    fz[:, B_HIST:B_HIST + tt, :] = z.reshape(nb, tt, D_BR)
    off_b = B_HIST - (B_WIDTH - 1)
    accb = jnp.zeros((nb, tt, D_BR), F32)
    for k in range(B_WIDTH):
        accb = accb + cbw_ref[k:k + 1, :] * fz[:, off_b + k:off_b + k + tt, :]
    b_out = _dot((proj(c0 + D_BR, c0 + 2 * D_BR) * accb.reshape(rows, D_BR)).astype(BF16),
                 wpb_ref[...])
    nb_ref[...] = fz[:, tt + B_HIST - (B_WIDTH - 1):tt + B_HIST, :]
    fz[:, 0:B_HIST, :] = fz[:, tt:tt + B_HIST, :]

    c0 = 5 * D_BR
    v = _layer_norm(proj(c0 + D_BR, c0 + 2 * D_BR), lcg_ref[...], lcb_ref[...])
    nv_ref[...] = v[rows - nv_rows * nb:, :].reshape(nb, nv_rows, D_BR)
    vb = v.astype(BF16)
    t_idx = lax.broadcasted_iota(jnp.int32, (CHUNK, CHUNK), 0)
    s_idx = lax.broadcasted_iota(jnp.int32, (CHUNK, CHUNK), 1)
    causal = s_idx <= t_idx
    for hd in range(C_HEADS):
        wm = jnp.where(causal, wsp_ref[hd], 0.0).astype(BF16)
        for c in range(rows // CHUNK):
            blk = (slice(c * CHUNK, (c + 1) * CHUNK), slice(hd * LANES, (hd + 1) * LANES))
            sc[blk] = _dot(wm, vb[blk])
    c_out = _dot((proj(c0, c0 + D_BR) * (sc[...].reshape(rows // CHUNK, CHUNK, D_BR)
                                        + bsp_ref[...]).reshape(rows, D_BR)).astype(BF16),
                 wpc_ref[...])

    c0 = 7 * D_BR
    merged = (_sigmoid(proj(c0, c0 + D_MODEL)) * a_out
              + _sigmoid(proj(c0 + D_MODEL, c0 + 2 * D_MODEL)) * b_out
              + _sigmoid(proj(c0 + 2 * D_MODEL, c0 + 3 * D_MODEL)) * c_out)
    xo_ref[...] = x + _dot(merged.astype(BF16), wout_ref[...])


def _const_spec(shape):
    nd = len(shape)
    return pl.BlockSpec(shape, lambda b, i: (0,) * nd, pipeline_mode=pl.Buffered(1))


def _drop_ref(kernel_fn, pos):
    def body(*refs):
        return kernel_fn(*refs[:pos], *refs[pos + 1:])
    return body


def _mixer(x, past_a, past_b, w, *, t, nb, tt, in_row, out_row, out_rows, out_buf=None):
    n = past_a.shape[0]
    rows = nb * tt
    assert n % nb == 0 and t % tt == 0 and rows % CHUNK == 0
    assert in_row % rows == 0 and out_row % rows == 0
    assert tt % SUBLANES == 0 and (tt % CONV_ROWS == 0 or
                                   (CONV_ROWS % tt == 0 and nb % (CONV_ROWS // tt) == 0))
    nv_rows = t - ((t - 1) // CHUNK) * CHUNK
    assert nv_rows <= tt
    n_t = t // tt
    consts = [w["g"], w["w_in"], w["conv_a_w"], w["conv_a_b"], w["ln_a_g"], w["ln_a_b"],
              w["conv_b_w"], w["ln_c_g"], w["ln_c_b"], w["wsp"], w["bsp"], w["w_proj_a"],
              w["w_proj_b"], w["w_proj_c"], w["w_out"]]
    seq_spec = lambda r, c: pl.BlockSpec((nb, r, c), lambda b, i: (b, 0, 0))
    row_spec = lambda first: pl.BlockSpec((rows, D_MODEL),
                                          lambda b, i: (first // rows + b * n_t + i, 0))
    body = functools.partial(_mixer_kernel, nb=nb, tt=tt, nv_rows=nv_rows)
    operands = [x, past_a, past_b, *consts]
    in_specs = ([row_spec(in_row), seq_spec(A_WIDTH - 1, D_BR), seq_spec(B_WIDTH - 1, D_BR)]
                + [_const_spec(c.shape) for c in consts])
    aliases = {}
    if out_buf is not None:
        assert out_buf.shape == (out_rows, D_MODEL)
        aliases = {len(operands): 0}
        body = _drop_ref(body, len(operands))
        operands.append(out_buf)
        in_specs.append(pl.BlockSpec(memory_space=pl.ANY))
    return pl.pallas_call(
        body,
        grid=(n // nb, n_t),
        in_specs=in_specs,
        out_specs=[row_spec(out_row),
                   seq_spec(A_WIDTH - 1, D_BR), seq_spec(B_WIDTH - 1, D_BR),
                   seq_spec(nv_rows, D_BR)],
        input_output_aliases=aliases,
        out_shape=[jax.ShapeDtypeStruct((out_rows, D_MODEL), F32),
                   jax.ShapeDtypeStruct((n, A_WIDTH - 1, D_BR), F32),
                   jax.ShapeDtypeStruct((n, B_WIDTH - 1, D_BR), F32),
                   jax.ShapeDtypeStruct((n, nv_rows, D_BR), F32)],
        scratch_shapes=[pltpu.VMEM((nb, A_HIST + tt, D_BR), F32),
                        pltpu.VMEM((nb, B_HIST + tt, D_BR), F32),
                        pltpu.VMEM((nb * tt, D_BR), F32),
                        pltpu.VMEM((SUBLANES, nb, tt + A_HIST - SUBLANES, D_BR), F32),
                        pltpu.VMEM((nb, tt, D_BR), F32),
                        pltpu.VMEM((A_WIDTH, SUBLANES, D_BR), F32)],
        compiler_params=pltpu.CompilerParams(
            dimension_semantics=("arbitrary", "arbitrary"), vmem_limit_bytes=VMEM_LIMIT),
        name="mixer",
    )(*operands)


def _swiglu_part(h_b, wg_ref, wu_ref, wd_ref):
    gate = _dot(h_b, wg_ref[...])
    act = gate * _sigmoid(gate) * _dot(h_b, wu_ref[...])
    return _dot(act.astype(BF16), wd_ref[...])


def _ffn_kernel(x_ref, g_ref, wg_ref, wu_ref, wd_ref, o_ref, *, tf):
    x = x_ref[...]
    h_b = _rms(x, g_ref[...]).astype(BF16)
    y = x
    for c0 in range(0, D_FF, tf):
        y = y + _swiglu_part(h_b, wg_ref.at[:, c0:c0 + tf], wu_ref.at[:, c0:c0 + tf],
                             wd_ref.at[c0:c0 + tf, :])
    o_ref[...] = y


def _ffn(x, g, wg, wu, wd, *, tm, tf):
    m, _ = x.shape
    assert m % tm == 0 and D_FF % tf == 0
    const = lambda shape: pl.BlockSpec(shape, lambda i: (0, 0), pipeline_mode=pl.Buffered(1))
    return pl.pallas_call(
        functools.partial(_ffn_kernel, tf=tf),
        grid=(m // tm,),
        in_specs=[pl.BlockSpec((tm, D_MODEL), lambda i: (i, 0)),
                  const((1, D_MODEL)), const((D_MODEL, D_FF)), const((D_MODEL, D_FF)),
                  const((D_FF, D_MODEL))],
        out_specs=pl.BlockSpec((tm, D_MODEL), lambda i: (i, 0)),
        out_shape=jax.ShapeDtypeStruct((m, D_MODEL), F32),
        compiler_params=pltpu.CompilerParams(
            dimension_semantics=("arbitrary",), vmem_limit_bytes=VMEM_LIMIT),
        name="ffn",
    )(x, g, wg, wu, wd)


SLAB = D_MODEL // LANES


def _to_slab(ref, val, rows):
    for c in range(SLAB):
        ref[pl.ds(c, rows, stride=SLAB), :] = val[:, c * LANES:(c + 1) * LANES]


def _route_kernel(x_ref, g_ref, wr_ref, br_ref, hs_ref, mi_ref, mf_ref, cnt_ref, carry, *, cap):
    i = pl.program_id(0)
    tm = x_ref.shape[0]

    @pl.when(i == 0)
    def _():
        carry[...] = jnp.zeros_like(carry)

    h = _rms(x_ref[...], g_ref[...])
    _to_slab(hs_ref, h, tm)

    col = lax.broadcasted_iota(jnp.int32, (tm, LANES), 1)
    logits = _dot_split(h, wr_ref[...]) + br_ref[...]
    logits = jnp.where(col < N_EXPERTS, logits, -jnp.inf)
    m1 = jnp.max(logits, -1, keepdims=True)
    i1 = jnp.min(jnp.where(logits == m1, col, LANES), -1, keepdims=True)
    rest = jnp.where(col == i1, -jnp.inf, logits)
    m2 = jnp.max(rest, -1, keepdims=True)
    i2 = jnp.min(jnp.where(rest == m2, col, LANES), -1, keepdims=True)
    e2 = jnp.exp(m2 - m1)
    p1 = 1.0 / (1.0 + e2)
    p2 = e2 * p1

    chosen = jnp.logical_or(col == i1, col == i2).astype(F32)
    r_idx = lax.broadcasted_iota(jnp.int32, (tm, tm), 0)
    c_idx = lax.broadcasted_iota(jnp.int32, (tm, tm), 1)
    before = (c_idx < r_idx).astype(BF16)
    rank = carry[...] + _dot(before, chosen.astype(BF16))
    dest = col.astype(F32) * float(cap) + rank
    d1 = jnp.sum(jnp.where(col == i1, dest, 0.0), -1, keepdims=True).astype(jnp.int32)
    d2 = jnp.sum(jnp.where(col == i2, dest, 0.0), -1, keepdims=True).astype(jnp.int32)
    mi_ref[...] = jnp.where(col == 0, d1, jnp.where(col == 1, d2, 0))
    mf_ref[...] = jnp.where(col == 0, p1, jnp.where(col == 1, p2, 0.0))
    carry[...] += jnp.sum(chosen, 0, keepdims=True)
    cnt_ref[...] = carry[...]


def _route(x, g, wr, br, *, tm, cap):
    m = x.shape[0]
    assert m % tm == 0
    row = lambda i: (i, 0)
    const = lambda i: (0, 0)
    return pl.pallas_call(
        functools.partial(_route_kernel, cap=cap),
        grid=(m // tm,),
        in_specs=[pl.BlockSpec((tm, D_MODEL), row), pl.BlockSpec((1, D_MODEL), const),
                  pl.BlockSpec((D_MODEL, LANES), const), pl.BlockSpec((1, LANES), const)],
        out_specs=[pl.BlockSpec((tm * SLAB, LANES), row), pl.BlockSpec((tm, LANES), row),
                   pl.BlockSpec((tm, LANES), row), pl.BlockSpec((1, LANES), const)],
        out_shape=[jax.ShapeDtypeStruct((m * SLAB, LANES), F32),
                   jax.ShapeDtypeStruct((m, LANES), jnp.int32),
                   jax.ShapeDtypeStruct((m, LANES), F32),
                   jax.ShapeDtypeStruct((1, LANES), F32)],
        scratch_shapes=[pltpu.VMEM((1, LANES), F32)],
        compiler_params=pltpu.CompilerParams(dimension_semantics=("arbitrary",),
                                             vmem_limit_bytes=VMEM_LIMIT),
        name="route",
    )(x, g, wr, br)


def _row_copy(src_ref, src_row, dst_ref, dst_row, sem):
    return pltpu.make_async_copy(
        src_ref.at[pl.ds(pl.multiple_of(src_row * SLAB, SLAB), SLAB), :],
        dst_ref.at[pl.ds(pl.multiple_of(dst_row * SLAB, SLAB), SLAB), :], sem)


def _wait_rows(src_ref, dst_ref, n_rows, sem):
    pltpu.make_async_copy(src_ref.at[pl.ds(0, n_rows * SLAB), :],
                          dst_ref.at[pl.ds(0, n_rows * SLAB), :], sem).wait()


def _dispatch_kernel(idx_ref, src_ref, dst_ref, sem, *, rows):
    def issue(r, c):
        for k in range(2):
            _row_copy(src_ref, r, dst_ref, idx_ref[0, 0, 2 * r + k], sem).start(priority=k)
        return c

    lax.fori_loop(0, rows, issue, 0)
    for k in range(2):
        _wait_rows(src_ref, dst_ref, rows, sem)


def _dispatch(h_slab, dest, *, rows, cap_rows):
    n_blk = dest.shape[0]
    return pl.pallas_call(
        functools.partial(_dispatch_kernel, rows=rows),
        grid=(n_blk,),
        in_specs=[pl.BlockSpec((1, 1, 2 * rows), lambda i: (i, 0, 0), memory_space=pltpu.SMEM),
                  pl.BlockSpec((rows * SLAB, LANES), lambda i: (i, 0))],
        out_specs=pl.BlockSpec(memory_space=pl.ANY),
        out_shape=jax.ShapeDtypeStruct((cap_rows * SLAB, LANES), F32),
        scratch_shapes=[pltpu.SemaphoreType.DMA(())],
        compiler_params=pltpu.CompilerParams(dimension_semantics=("arbitrary",)),
        name="dispatch",
    )(dest, h_slab)


def _experts_kernel(tblk, texp, tn, xs_ref, wg_ref, wu_ref, wd_ref, o_ref, hb, *, tf):
    n = tn[pl.program_id(0)]
    tm = hb.shape[0]

    @pl.when(n > 0)
    def _():
        valid = lax.broadcasted_iota(jnp.int32, (tm, LANES), 0) < n
        for c in range(SLAB):
            xc = xs_ref[pl.ds(c, tm, stride=SLAB), :]
            hb[:, c * LANES:(c + 1) * LANES] = jnp.where(valid, xc, 0.0).astype(BF16)
        h_b = hb[...]
        y = _swiglu_part(h_b, wg_ref.at[:, 0:tf], wu_ref.at[:, 0:tf], wd_ref.at[0:tf, :])
        for c0 in range(tf, D_FF, tf):
            y = y + _swiglu_part(h_b, wg_ref.at[:, c0:c0 + tf], wu_ref.at[:, c0:c0 + tf],
                                 wd_ref.at[c0:c0 + tf, :])
        _to_slab(o_ref, y, tm)


def _experts(xs_slab, tblk, texp, tn, wg, wu, wd, *, tm, tf):
    n_tiles = tblk.shape[0]
    assert D_FF % tf == 0
    tile = lambda t, tblk, texp, tn: (tblk[t], 0)
    expert = lambda t, tblk, texp, tn: (texp[t], 0, 0)
    weight = lambda r, c: pl.BlockSpec((None, r, c), expert, pipeline_mode=pl.Buffered(1))
    grid_spec = pltpu.PrefetchScalarGridSpec(
        num_scalar_prefetch=3,
        grid=(n_tiles,),
        in_specs=[pl.BlockSpec((tm * SLAB, LANES), tile),
                  weight(D_MODEL, D_FF), weight(D_MODEL, D_FF), weight(D_FF, D_MODEL)],
        out_specs=pl.BlockSpec((tm * SLAB, LANES), tile),
        scratch_shapes=[pltpu.VMEM((tm, D_MODEL), BF16)])
    return pl.pallas_call(
        functools.partial(_experts_kernel, tf=tf),
        grid_spec=grid_spec,
        out_shape=jax.ShapeDtypeStruct(xs_slab.shape, F32),
        compiler_params=pltpu.CompilerParams(dimension_semantics=("arbitrary",),
                                             vmem_limit_bytes=VMEM_LIMIT),
        name="experts",
    )(tblk, texp, tn, xs_slab, wg, wu, wd)


def _combine_kernel(idx_ref, idx_next_ref, x_ref, mf_ref, gf_ref, ys_ref, op_ref, os_ref,
                    buf, ysum, sem, *, rows, n_prompt_tiles):
    i = pl.program_id(0)
    n_i = pl.num_programs(0)
    slot = i % 2

    def issue(ref, s):
        def body(r, c):
            for k in range(2):
                _row_copy(ys_ref, ref[0, 0, 2 * r + k], buf.at[s], k * rows + r,
                          sem.at[s]).start(priority=k)
            return c
        lax.fori_loop(0, rows, body, 0, unroll=4)

    def finish(s):
        _wait_rows(ys_ref, buf.at[s], 2 * rows, sem.at[s])
        p1 = mf_ref[:, 0:1]
        p2 = mf_ref[:, 1:2]
        for c in range(SLAB):
            cs = slice(c * LANES, (c + 1) * LANES)
            y1 = buf[s, pl.ds(c, rows, stride=SLAB), :]
            y2 = buf[s, pl.ds(rows * SLAB + c, rows, stride=SLAB), :]
            ysum[:, cs] = x_ref[:, cs] + p1 * y1 + p2 * y2
        res = _rms(ysum[...], gf_ref[...])

        @pl.when(i < n_prompt_tiles)
        def _():
            op_ref[...] = res

        @pl.when(i >= n_prompt_tiles)
        def _():
            os_ref[...] = res

    for s in range(2):
        @pl.when(slot == s)
        def _():
            @pl.when(i == 0)
            def _():
                issue(idx_ref, s)

            @pl.when(i + 1 < n_i)
            def _():
                issue(idx_next_ref, 1 - s)

            finish(s)


def _combine(x, mf, g_final, y_slab, dest, *, rows, m_prompt):
    m = x.shape[0]
    n_blk = dest.shape[0]
    assert m == n_blk * rows and m_prompt % rows == 0
    n_p = m_prompt // rows
    idx_spec = lambda f: pl.BlockSpec((1, 1, 2 * rows), f, memory_space=pltpu.SMEM)
    return pl.pallas_call(
        functools.partial(_combine_kernel, rows=rows, n_prompt_tiles=n_p),
        grid=(n_blk,),
        in_specs=[idx_spec(lambda i: (i, 0, 0)),
                  idx_spec(lambda i: (jnp.minimum(i + 1, n_blk - 1), 0, 0)),
                  pl.BlockSpec((rows, D_MODEL), lambda i: (i, 0)),
                  pl.BlockSpec((rows, LANES), lambda i: (i, 0)),
                  pl.BlockSpec((1, D_MODEL), lambda i: (0, 0)),
                  pl.BlockSpec(memory_space=pl.ANY)],
        out_specs=[pl.BlockSpec((rows, D_MODEL), lambda i: (jnp.minimum(i, n_p - 1), 0)),
                   pl.BlockSpec((rows, D_MODEL), lambda i: (jnp.maximum(i - n_p, 0), 0))],
        out_shape=[jax.ShapeDtypeStruct((m_prompt, D_MODEL), F32),
                   jax.ShapeDtypeStruct((m - m_prompt, D_MODEL), F32)],
        scratch_shapes=[pltpu.VMEM((2, 2 * rows * SLAB, LANES), F32),
                        pltpu.VMEM((rows, D_MODEL), F32),
                        pltpu.SemaphoreType.DMA((2,))],
        compiler_params=pltpu.CompilerParams(dimension_semantics=("arbitrary",),
                                             vmem_limit_bytes=VMEM_LIMIT),
        name="combine",
    )(dest, dest, x, mf, g_final, y_slab)


def _expert_tiles(counts, *, tm, cap, n_tiles):
    tiles = (counts + tm - 1) // tm
    ends = jnp.cumsum(tiles)
    t = jnp.arange(n_tiles, dtype=jnp.int32)
    tc = jnp.minimum(t, ends[-1] - 1)
    e = jnp.sum((tc[:, None] >= ends[None, :]).astype(jnp.int32), axis=1)
    k = tc - (ends - tiles)[e]
    tblk = e * (cap // tm) + k
    tn = jnp.where(t < ends[-1], jnp.minimum(counts[e] - k * tm, tm), 0)
    return tblk.astype(jnp.int32), e.astype(jnp.int32), tn.astype(jnp.int32)


def _moe_final(x, g_ffn, wr, br, wg, wu, wd, g_final, *, m_prompt):
    m = x.shape[0]
    route_rows, tm, tf, out_rows, in_rows = 512, 512, 1792, 1024, 256
    cap = -(-m // tm) * tm
    h_slab, mi, mf, cnt = _route(x, g_ffn, wr, br, tm=route_rows, cap=cap)
    dest = mi[:, :2]
    counts = cnt[0, :N_EXPERTS].astype(jnp.int32)
    n_tiles = 2 * m // tm + N_EXPERTS
    tblk, texp, tn = _expert_tiles(counts, tm=tm, cap=cap, n_tiles=n_tiles)
    xs_slab = _dispatch(h_slab, dest.reshape(m // out_rows, 1, 2 * out_rows), rows=out_rows,
                        cap_rows=N_EXPERTS * cap)
    y_slab = _experts(xs_slab, tblk, texp, tn, wg, wu, wd, tm=tm, tf=tf)
    return _combine(x, mf, g_final, y_slab, dest.reshape(m // in_rows, 1, 2 * in_rows),
                    rows=in_rows, m_prompt=m_prompt)


def _row(v):
    return v.reshape(1, -1)


def kernel(x_prompt, x_sample, state_conv_a, state_conv_b, norm_mix_g, w_in, conv_a_w, conv_a_b, ln_a_g, ln_a_b, conv_b_w, ln_c_g, ln_c_b, w_spatial, b_spatial, w_proj_a, w_proj_b, w_proj_c, w_out, norm_ffn_g, ffn_w_gate, ffn_w_up, ffn_w_down, w_router, b_router, moe_w_gate, moe_w_up, moe_w_down, norm_final_g):
    depth = w_in.shape[0]
    n_p, t_p, _ = x_prompt.shape
    n_s, t_s, _ = x_sample.shape
    seqs_per_chunk = CHUNK // t_s

    zeros_a = jnp.zeros((n_p, A_WIDTH - 1, D_BR), F32)
    zeros_b = jnp.zeros((n_p, B_WIDTH - 1, D_BR), F32)

    m_p, m_s = n_p * t_p, n_s * t_s
    m = m_p + m_s
    x_p, row_p = x_prompt.reshape(m_p, D_MODEL), 0
    x_s, row_s = x_sample.reshape(m_s, D_MODEL), 0
    outs = {k: [] for k in ("ap", "as", "bp", "bs", "vp", "vs")}
    for l in range(depth):
        bsp_p = jnp.repeat(b_spatial[l].T, LANES, axis=1)
        eye = jnp.eye(seqs_per_chunk, dtype=F32)
        wsp_s = jnp.einsum("ab,hts->hatbs", eye, w_spatial[l][:, :t_s, :t_s]).reshape(
            C_HEADS, CHUNK, CHUNK)
        bsp_s = jnp.tile(bsp_p[:t_s], (seqs_per_chunk, 1))
        w = dict(g=_row(norm_mix_g[l]), w_in=w_in[l].astype(BF16), conv_a_w=conv_a_w[l],
                 conv_a_b=_row(conv_a_b[l]), ln_a_g=_row(ln_a_g[l]), ln_a_b=_row(ln_a_b[l]),
                 conv_b_w=conv_b_w[l], ln_c_g=_row(ln_c_g[l]), ln_c_b=_row(ln_c_b[l]),
                 w_proj_a=w_proj_a[l].astype(BF16), w_proj_b=w_proj_b[l].astype(BF16),
                 w_proj_c=w_proj_c[l].astype(BF16), w_out=w_out[l].astype(BF16))
        x, a_p, b_p, v_p = _mixer(x_p, zeros_a, zeros_b, dict(w, wsp=w_spatial[l], bsp=bsp_p),
                                  t=t_p, nb=1, tt=512, in_row=row_p, out_row=0, out_rows=m)
        x, a_s, b_s, v_s = _mixer(x_s, state_conv_a[l], state_conv_b[l],
                                  dict(w, wsp=wsp_s, bsp=bsp_s), t=t_s, nb=16, tt=t_s,
                                  in_row=row_s, out_row=m_p, out_rows=m, out_buf=x)
        for k, val in zip(("ap", "as", "bp", "bs", "vp", "vs"), (a_p, a_s, b_p, b_s, v_p, v_s)):
            outs[k].append(val)

        i = l // 2
        g_ffn = _row(norm_ffn_g[l])
        if l % 2 == 0:
            x = _ffn(x, g_ffn, ffn_w_gate[i].astype(BF16), ffn_w_up[i].astype(BF16),
                     ffn_w_down[i].astype(BF16), tm=512, tf=1792)
            x_p, row_p, x_s, row_s = x, 0, x, m_p
        else:
            assert l == depth - 1, "the routed layer applies the final norm"
            wr = jnp.pad(w_router[i], ((0, 0), (0, LANES - N_EXPERTS)))
            br = jnp.pad(_row(b_router[i]), ((0, 0), (0, LANES - N_EXPERTS)))
            y_p, y_s = _moe_final(x, g_ffn, wr, br, moe_w_gate[i].astype(BF16),
                                  moe_w_up[i].astype(BF16), moe_w_down[i].astype(BF16),
                                  _row(norm_final_g), m_prompt=m_p)
    y_p = y_p.reshape(n_p, t_p, D_MODEL)
    y_s = y_s.reshape(n_s, t_s, D_MODEL)

    st = lambda k: jnp.stack(outs[k])
    return (y_p, y_s, st("ap"), st("as"), st("bp"), st("bs"), st("vp"), st("vs"))
```

```python
import functools

import jax
import jax.numpy as jnp
from jax import lax
from jax.experimental import pallas as pl
from jax.experimental.pallas import tpu as pltpu

D_MODEL = 1024
D_BR = 512
A_WIDTH = 31
B_WIDTH = 3
C_HEADS = 4
CHUNK = 128
D_FF = 3584
N_EXPERTS = 8
EPS = 1e-6

A_HIST = 32
B_HIST = 8
LANES = 128
SUBLANES = 8
CONV_ROWS = 32
VMEM_LIMIT = 56 * 1024 * 1024

BF16 = jnp.bfloat16
F32 = jnp.float32


def _rms(x, g):
    return x * lax.rsqrt(jnp.mean(x * x, -1, keepdims=True) + EPS) * g


def _layer_norm(x, g, b):
    mu = jnp.mean(x, -1, keepdims=True)
    xc = x - mu
    var = jnp.mean(xc * xc, -1, keepdims=True)
    return xc * lax.rsqrt(var + EPS) * g + b


def _sigmoid(x):
    return 0.5 * jnp.tanh(0.5 * x) + 0.5


def _dot(a, b):
    return jnp.dot(a, b, preferred_element_type=F32)


def _dot_split(a, b):
    a_hi = a.astype(BF16)
    b_hi = b.astype(BF16)
    a_lo = (a - a_hi.astype(F32)).astype(BF16)
    b_lo = (b - b_hi.astype(F32)).astype(BF16)
    return _dot(a_hi, b_hi) + (_dot(a_hi, b_lo) + _dot(a_lo, b_hi))


def _mixer_kernel(x_ref, pa_ref, pb_ref, g_ref, win_ref, caw_ref, cab_ref, lag_ref, lab_ref,
                  cbw_ref, lcg_ref, lcb_ref, wsp_ref, bsp_ref, wpa_ref, wpb_ref, wpc_ref,
                  wout_ref, xo_ref, na_ref, nb_ref, nv_ref, fa, fz, sc, sh, ca, wb, *, nb, tt, nv_rows):
    i = pl.program_id(1)
    rows = nb * tt

    @pl.when(i == 0)
    def _():
        fa[:, A_HIST - (A_WIDTH - 1):A_HIST, :] = pa_ref[...]
        fz[:, B_HIST - (B_WIDTH - 1):B_HIST, :] = pb_ref[...]

    x = x_ref[...]
    hb = _rms(x, g_ref[...]).astype(BF16)

    def proj(c0, c1):
        return _dot(hb, win_ref[:, c0:c1])

    a = proj(0, D_BR) * _sigmoid(proj(D_BR, 2 * D_BR))
    fa[:, A_HIST:A_HIST + tt, :] = a.reshape(nb, tt, D_BR)
    off_a = A_HIST - (A_WIDTH - 1)
    for j in range(SUBLANES):
        span = tt + SUBLANES * ((A_WIDTH - 1 - j) // SUBLANES)
        sh[j, :, 0:span, :] = fa[:, off_a + j:off_a + j + span, :]
    for k in range(A_WIDTH):
        wb[k] = jnp.broadcast_to(caw_ref[k:k + 1, :], (SUBLANES, D_BR))
    rb = min(tt, CONV_ROWS)
    sb = CONV_ROWS // rb
    n_rb = tt // rb

    def conv_step(u, carry):
        s0 = (u // n_rb) * sb
        r0 = (u % n_rb) * rb
        groups = range(0, rb, SUBLANES)
        acc = [jnp.broadcast_to(cab_ref[...], (sb, SUBLANES, D_BR)) for _ in groups]
        for k in range(A_WIDTH):
            q, j = divmod(k, SUBLANES)
            w_k = wb[k]
            for gi, g in enumerate(groups):
                lo = r0 + g + SUBLANES * q
                acc[gi] = acc[gi] + w_k * sh[j, s0:s0 + sb, lo:lo + SUBLANES, :]
        for gi, g in enumerate(groups):
            ca[s0:s0 + sb, r0 + g:r0 + g + SUBLANES, :] = acc[gi]
        return carry

    for u in range((nb // sb) * n_rb):
        conv_step(u, 0)
    a_ln = _layer_norm(ca[...].reshape(rows, D_BR), lag_ref[...], lab_ref[...])
    a_out = _dot((a_ln * _sigmoid(a_ln)).astype(BF16), wpa_ref[...])
    na_ref[...] = fa[:, tt + A_HIST - (A_WIDTH - 1):tt + A_HIST, :]
    fa[:, 0:A_HIST, :] = fa[:, tt:tt + A_HIST, :]

    c0 = 2 * D_BR
    z = proj(c0 + 2 * D_BR, c0 + 3 * D_BR) * proj(c0, c0 + D_BR)
    fz[:, B_HIST:B_HIST + tt, :] = z.reshape(nb, tt, D_BR)
    off_b = B_HIST - (B_WIDTH - 1)
    accb = jnp.zeros((nb, tt, D_BR), F32)
    for k in range(B_WIDTH):
        accb = accb + cbw_ref[k:k + 1, :] * fz[:, off_b + k:off_b + k + tt, :]
    b_out = _dot((proj(c0 + D_BR, c0 + 2 * D_BR) * accb.reshape(rows, D_BR)).astype(BF16),
                 wpb_ref[...])
    nb_ref[...] = fz[:, tt + B_HIST - (B_WIDTH - 1):tt + B_HIST, :]
    fz[:, 0:B_HIST, :] = fz[:, tt:tt + B_HIST, :]

    c0 = 5 * D_BR
    v = _layer_norm(proj(c0 + D_BR, c0 + 2 * D_BR), lcg_ref[...], lcb_ref[...])
    nv_ref[...] = v[rows - nv_rows * nb:, :].reshape(nb, nv_rows, D_BR)
    vb = v.astype(BF16)
    t_idx = lax.broadcasted_iota(jnp.int32, (CHUNK, CHUNK), 0)
    s_idx = lax.broadcasted_iota(jnp.int32, (CHUNK, CHUNK), 1)
    causal = s_idx <= t_idx
    for hd in range(C_HEADS):
        wm = jnp.where(causal, wsp_ref[hd], 0.0).astype(BF16)
        for c in range(rows // CHUNK):
            blk = (slice(c * CHUNK, (c + 1) * CHUNK), slice(hd * LANES, (hd + 1) * LANES))
            sc[blk] = _dot(wm, vb[blk])
    c_out = _dot((proj(c0, c0 + D_BR) * (sc[...].reshape(rows // CHUNK, CHUNK, D_BR)
                                        + bsp_ref[...]).reshape(rows, D_BR)).astype(BF16),
                 wpc_ref[...])

    c0 = 7 * D_BR
    merged = (_sigmoid(proj(c0, c0 + D_MODEL)) * a_out
              + _sigmoid(proj(c0 + D_MODEL, c0 + 2 * D_MODEL)) * b_out
              + _sigmoid(proj(c0 + 2 * D_MODEL, c0 + 3 * D_MODEL)) * c_out)
    xo_ref[...] = x + _dot(merged.astype(BF16), wout_ref[...])


def _const_spec(shape):
    nd = len(shape)
    return pl.BlockSpec(shape, lambda b, i: (0,) * nd, pipeline_mode=pl.Buffered(1))


def _drop_refs(kernel_fn, pos, count):
    def body(*refs):
        return kernel_fn(*refs[:pos], *refs[pos + count:])
    return body


def _mixer(x, past_a, past_b, w, *, t, nb, tt, in_row, out_row, out_rows, layer, depth,
           out_buf=None, state_bufs=None):
    n = past_a.shape[0]
    rows = nb * tt
    assert n % nb == 0 and t % tt == 0 and rows % CHUNK == 0
    assert in_row % rows == 0 and out_row % rows == 0
    assert tt % SUBLANES == 0 and (tt % CONV_ROWS == 0 or
                                   (CONV_ROWS % tt == 0 and nb % (CONV_ROWS // tt) == 0))
    nv_rows = t - ((t - 1) // CHUNK) * CHUNK
    assert nv_rows <= tt
    n_t = t // tt
    consts = [w["g"], w["w_in"], w["conv_a_w"], w["conv_a_b"], w["ln_a_g"], w["ln_a_b"],
              w["conv_b_w"], w["ln_c_g"], w["ln_c_b"], w["wsp"], w["bsp"], w["w_proj_a"],
              w["w_proj_b"], w["w_proj_c"], w["w_out"]]
    seq_spec = lambda r, c: pl.BlockSpec((nb, r, c), lambda b, i: (b, 0, 0))
    row_spec = lambda first: pl.BlockSpec((rows, D_MODEL),
                                          lambda b, i: (first // rows + b * n_t + i, 0))
    body = functools.partial(_mixer_kernel, nb=nb, tt=tt, nv_rows=nv_rows)
    operands = [x, past_a, past_b, *consts]
    in_specs = ([row_spec(in_row), seq_spec(A_WIDTH - 1, D_BR), seq_spec(B_WIDTH - 1, D_BR)]
                + [_const_spec(c.shape) for c in consts])
    state_rows = (A_WIDTH - 1, B_WIDTH - 1, nv_rows)
    out_shape = [jax.ShapeDtypeStruct((out_rows, D_MODEL), F32)] + [
        jax.ShapeDtypeStruct((depth, n, r, D_BR), F32) for r in state_rows]
    n_in = len(operands)
    aliases = {}
    for k, carrier in enumerate((out_buf, *(state_bufs or (None,) * 3))):
        if carrier is not None:
            assert carrier.shape == out_shape[k].shape
            aliases[len(operands)] = k
            operands.append(carrier)
            in_specs.append(pl.BlockSpec(memory_space=pl.ANY))
    body = _drop_refs(body, n_in, len(operands) - n_in)
    layer_spec = lambda r: pl.BlockSpec((None, nb, r, D_BR), lambda b, i: (layer, b, 0, 0))
    return pl.pallas_call(
        body,
        grid=(n // nb, n_t),
        in_specs=in_specs,
        out_specs=[row_spec(out_row)] + [layer_spec(r) for r in state_rows],
        input_output_aliases=aliases,
        out_shape=out_shape,
        scratch_shapes=[pltpu.VMEM((nb, A_HIST + tt, D_BR), F32),
                        pltpu.VMEM((nb, B_HIST + tt, D_BR), F32),
                        pltpu.VMEM((nb * tt, D_BR), F32),
                        pltpu.VMEM((SUBLANES, nb, tt + A_HIST - SUBLANES, D_BR), F32),
                        pltpu.VMEM((nb, tt, D_BR), F32),
                        pltpu.VMEM((A_WIDTH, SUBLANES, D_BR), F32)],
        compiler_params=pltpu.CompilerParams(
            dimension_semantics=("arbitrary", "arbitrary"), vmem_limit_bytes=VMEM_LIMIT),
        name="mixer",
    )(*operands)


def _swiglu_part(h_b, wg_ref, wu_ref, wd_ref):
    gate = _dot(h_b, wg_ref[...])
    act = gate * _sigmoid(gate) * _dot(h_b, wu_ref[...])
    return _dot(act.astype(BF16), wd_ref[...])


def _ffn_kernel(x_ref, g_ref, wg_ref, wu_ref, wd_ref, o_ref, *, tf):
    x = x_ref[...]
    h_b = _rms(x, g_ref[...]).astype(BF16)
    y = x
    for c0 in range(0, D_FF, tf):
        y = y + _swiglu_part(h_b, wg_ref.at[:, c0:c0 + tf], wu_ref.at[:, c0:c0 + tf],
                             wd_ref.at[c0:c0 + tf, :])
    o_ref[...] = y


def _ffn(x, g, wg, wu, wd, *, tm, tf):
    m, _ = x.shape
    assert m % tm == 0 and D_FF % tf == 0
    const = lambda shape: pl.BlockSpec(shape, lambda i: (0, 0), pipeline_mode=pl.Buffered(1))
    return pl.pallas_call(
        functools.partial(_ffn_kernel, tf=tf),
        grid=(m // tm,),
        in_specs=[pl.BlockSpec((tm, D_MODEL), lambda i: (i, 0)),
                  const((1, D_MODEL)), const((D_MODEL, D_FF)), const((D_MODEL, D_FF)),
                  const((D_FF, D_MODEL))],
        out_specs=pl.BlockSpec((tm, D_MODEL), lambda i: (i, 0)),
        out_shape=jax.ShapeDtypeStruct((m, D_MODEL), F32),
        compiler_params=pltpu.CompilerParams(
            dimension_semantics=("arbitrary",), vmem_limit_bytes=VMEM_LIMIT),
        name="ffn",
    )(x, g, wg, wu, wd)


SLAB = D_MODEL // LANES


def _to_slab(ref, val, rows):
    for c in range(SLAB):
        ref[pl.ds(c, rows, stride=SLAB), :] = val[:, c * LANES:(c + 1) * LANES]


def _route_kernel(x_ref, g_ref, wr_ref, br_ref, hs_ref, mi_ref, mf_ref, cnt_ref, carry, *, cap):
    i = pl.program_id(0)
    tm = x_ref.shape[0]

    @pl.when(i == 0)
    def _():
        carry[...] = jnp.zeros_like(carry)

    h = _rms(x_ref[...], g_ref[...])
    _to_slab(hs_ref, h, tm)

    col = lax.broadcasted_iota(jnp.int32, (tm, LANES), 1)
    logits = _dot_split(h, wr_ref[...]) + br_ref[...]
    logits = jnp.where(col < N_EXPERTS, logits, -jnp.inf)
    m1 = jnp.max(logits, -1, keepdims=True)
    i1 = jnp.min(jnp.where(logits == m1, col, LANES), -1, keepdims=True)
    rest = jnp.where(col == i1, -jnp.inf, logits)
    m2 = jnp.max(rest, -1, keepdims=True)
    i2 = jnp.min(jnp.where(rest == m2, col, LANES), -1, keepdims=True)
    e2 = jnp.exp(m2 - m1)
    p1 = 1.0 / (1.0 + e2)
    p2 = e2 * p1

    chosen = jnp.logical_or(col == i1, col == i2).astype(F32)
    r_idx = lax.broadcasted_iota(jnp.int32, (tm, tm), 0)
    c_idx = lax.broadcasted_iota(jnp.int32, (tm, tm), 1)
    before = (c_idx < r_idx).astype(BF16)
    rank = carry[...] + _dot(before, chosen.astype(BF16))
    dest = col.astype(F32) * float(cap) + rank
    d1 = jnp.sum(jnp.where(col == i1, dest, 0.0), -1, keepdims=True)
    d2 = jnp.sum(jnp.where(col == i2, dest, 0.0), -1, keepdims=True)
    by_choice = jnp.where(col == 0, d1, jnp.where(col == 1, d2, 0.0)).T
    mi_ref[...] = by_choice[0:SUBLANES, :].astype(jnp.int32)
    mf_ref[...] = jnp.where(col == 0, p1, jnp.where(col == 1, p2, 0.0))
    carry[...] += jnp.sum(chosen, 0, keepdims=True)
    cnt_ref[...] = carry[...]


def _route(x, g, wr, br, *, tm, cap):
    m = x.shape[0]
    assert m % tm == 0
    row = lambda i: (i, 0)
    const = lambda i: (0, 0)
    return pl.pallas_call(
        functools.partial(_route_kernel, cap=cap),
        grid=(m // tm,),
        in_specs=[pl.BlockSpec((tm, D_MODEL), row), pl.BlockSpec((1, D_MODEL), const),
                  pl.BlockSpec((D_MODEL, LANES), const), pl.BlockSpec((1, LANES), const)],
        out_specs=[pl.BlockSpec((tm * SLAB, LANES), row),
                   pl.BlockSpec((SUBLANES, tm), lambda i: (0, i)),
                   pl.BlockSpec((tm, LANES), row), pl.BlockSpec((1, LANES), const)],
        out_shape=[jax.ShapeDtypeStruct((m * SLAB, LANES), F32),
                   jax.ShapeDtypeStruct((SUBLANES, m), jnp.int32),
                   jax.ShapeDtypeStruct((m, LANES), F32),
                   jax.ShapeDtypeStruct((1, LANES), F32)],
        scratch_shapes=[pltpu.VMEM((1, LANES), F32)],
        compiler_params=pltpu.CompilerParams(dimension_semantics=("arbitrary",),
                                             vmem_limit_bytes=VMEM_LIMIT),
        name="route",
    )(x, g, wr, br)


def _row_copy(src_ref, src_row, dst_ref, dst_row, sem):
    return pltpu.make_async_copy(
        src_ref.at[pl.ds(pl.multiple_of(src_row * SLAB, SLAB), SLAB), :],
        dst_ref.at[pl.ds(pl.multiple_of(dst_row * SLAB, SLAB), SLAB), :], sem)


def _wait_rows(src_ref, dst_ref, n_rows, sem):
    pltpu.make_async_copy(src_ref.at[pl.ds(0, n_rows * SLAB), :],
                          dst_ref.at[pl.ds(0, n_rows * SLAB), :], sem).wait()


def _dispatch_kernel(idx1_ref, idx2_ref, src_ref, dst_ref, sem, *, rows):
    def issue(r, c):
        for k, idx_ref in enumerate((idx1_ref, idx2_ref)):
            _row_copy(src_ref, r, dst_ref, idx_ref[0, 0, r], sem).start(priority=k)
        return c

    lax.fori_loop(0, rows, issue, 0)
    for k in range(2):
        _wait_rows(src_ref, dst_ref, rows, sem)


def _idx_spec(rows, index_map):
    return pl.BlockSpec((1, 1, rows), index_map, memory_space=pltpu.SMEM)


def _dispatch(h_slab, dest1, dest2, *, rows, cap_rows):
    n_blk = dest1.shape[0] // rows
    blocked = lambda d: d.reshape(n_blk, 1, rows)
    return pl.pallas_call(
        functools.partial(_dispatch_kernel, rows=rows),
        grid=(n_blk,),
        in_specs=[_idx_spec(rows, lambda i: (i, 0, 0)), _idx_spec(rows, lambda i: (i, 0, 0)),
                  pl.BlockSpec((rows * SLAB, LANES), lambda i: (i, 0))],
        out_specs=pl.BlockSpec(memory_space=pl.ANY),
        out_shape=jax.ShapeDtypeStruct((cap_rows * SLAB, LANES), F32),
        scratch_shapes=[pltpu.SemaphoreType.DMA(())],
        compiler_params=pltpu.CompilerParams(dimension_semantics=("arbitrary",)),
        name="dispatch",
    )(blocked(dest1), blocked(dest2), h_slab)


def _experts_kernel(tblk, texp, tn, xs_ref, wg_ref, wu_ref, wd_ref, o_ref, hb, *, tf):
    n = tn[pl.program_id(0)]
    tm = hb.shape[0]

    @pl.when(n > 0)
    def _():
        valid = lax.broadcasted_iota(jnp.int32, (tm, LANES), 0) < n
        for c in range(SLAB):
            xc = xs_ref[pl.ds(c, tm, stride=SLAB), :]
            hb[:, c * LANES:(c + 1) * LANES] = jnp.where(valid, xc, 0.0).astype(BF16)
        h_b = hb[...]
        y = _swiglu_part(h_b, wg_ref.at[:, 0:tf], wu_ref.at[:, 0:tf], wd_ref.at[0:tf, :])
        for c0 in range(tf, D_FF, tf):
            y = y + _swiglu_part(h_b, wg_ref.at[:, c0:c0 + tf], wu_ref.at[:, c0:c0 + tf],
                                 wd_ref.at[c0:c0 + tf, :])
        _to_slab(o_ref, y, tm)


def _experts(xs_slab, tblk, texp, tn, wg, wu, wd, *, tm, tf):
    n_tiles = tblk.shape[0]
    assert D_FF % tf == 0
    tile = lambda t, tblk, texp, tn: (tblk[t], 0)
    expert = lambda t, tblk, texp, tn: (texp[t], 0, 0)
    weight = lambda r, c: pl.BlockSpec((None, r, c), expert, pipeline_mode=pl.Buffered(1))
    grid_spec = pltpu.PrefetchScalarGridSpec(
        num_scalar_prefetch=3,
        grid=(n_tiles,),
        in_specs=[pl.BlockSpec((tm * SLAB, LANES), tile),
                  weight(D_MODEL, D_FF), weight(D_MODEL, D_FF), weight(D_FF, D_MODEL)],
        out_specs=pl.BlockSpec((tm * SLAB, LANES), tile),
        scratch_shapes=[pltpu.VMEM((tm, D_MODEL), BF16)])
    return pl.pallas_call(
        functools.partial(_experts_kernel, tf=tf),
        grid_spec=grid_spec,
        out_shape=jax.ShapeDtypeStruct(xs_slab.shape, F32),
        compiler_params=pltpu.CompilerParams(dimension_semantics=("arbitrary",),
                                             vmem_limit_bytes=VMEM_LIMIT),
        name="experts",
    )(tblk, texp, tn, xs_slab, wg, wu, wd)


def _combine_kernel(idx1_ref, idx2_ref, idx1_next_ref, idx2_next_ref, x_ref, mf_ref, gf_ref,
                    ys_ref, op_ref, os_ref, buf, ysum, sem, *, rows, n_prompt_tiles):
    i = pl.program_id(0)
    n_i = pl.num_programs(0)
    slot = i % 2

    def issue(refs, s):
        def body(r, c):
            for k, ref in enumerate(refs):
                _row_copy(ys_ref, ref[0, 0, r], buf.at[s], k * rows + r,
                          sem.at[s]).start(priority=k)
            return c
        lax.fori_loop(0, rows, body, 0, unroll=4)

    def finish(s):
        _wait_rows(ys_ref, buf.at[s], 2 * rows, sem.at[s])
        p1 = mf_ref[:, 0:1]
        p2 = mf_ref[:, 1:2]
        for c in range(SLAB):
            cs = slice(c * LANES, (c + 1) * LANES)
            y1 = buf[s, pl.ds(c, rows, stride=SLAB), :]
            y2 = buf[s, pl.ds(rows * SLAB + c, rows, stride=SLAB), :]
            ysum[:, cs] = x_ref[:, cs] + p1 * y1 + p2 * y2
        res = _rms(ysum[...], gf_ref[...])

        @pl.when(i < n_prompt_tiles)
        def _():
            op_ref[...] = res

        @pl.when(i >= n_prompt_tiles)
        def _():
            os_ref[...] = res

    for s in range(2):
        @pl.when(slot == s)
        def _():
            @pl.when(i == 0)
            def _():
                issue((idx1_ref, idx2_ref), s)

            @pl.when(i + 1 < n_i)
            def _():
                issue((idx1_next_ref, idx2_next_ref), 1 - s)

            finish(s)


def _combine(x, mf, g_final, y_slab, dest1, dest2, *, rows, m_prompt):
    m = x.shape[0]
    n_blk = m // rows
    assert m == n_blk * rows and m_prompt % rows == 0
    n_p = m_prompt // rows
    blocked = lambda d: d.reshape(n_blk, 1, rows)
    this = lambda i: (i, 0, 0)
    ahead = lambda i: (jnp.minimum(i + 1, n_blk - 1), 0, 0)
    return pl.pallas_call(
        functools.partial(_combine_kernel, rows=rows, n_prompt_tiles=n_p),
        grid=(n_blk,),
        in_specs=[_idx_spec(rows, this), _idx_spec(rows, this),
                  _idx_spec(rows, ahead), _idx_spec(rows, ahead),
                  pl.BlockSpec((rows, D_MODEL), lambda i: (i, 0)),
                  pl.BlockSpec((rows, LANES), lambda i: (i, 0)),
                  pl.BlockSpec((1, D_MODEL), lambda i: (0, 0)),
                  pl.BlockSpec(memory_space=pl.ANY)],
        out_specs=[pl.BlockSpec((rows, D_MODEL), lambda i: (jnp.minimum(i, n_p - 1), 0)),
                   pl.BlockSpec((rows, D_MODEL), lambda i: (jnp.maximum(i - n_p, 0), 0))],
        out_shape=[jax.ShapeDtypeStruct((m_prompt, D_MODEL), F32),
                   jax.ShapeDtypeStruct((m - m_prompt, D_MODEL), F32)],
        scratch_shapes=[pltpu.VMEM((2, 2 * rows * SLAB, LANES), F32),
                        pltpu.VMEM((rows, D_MODEL), F32),
                        pltpu.SemaphoreType.DMA((2,))],
        compiler_params=pltpu.CompilerParams(dimension_semantics=("arbitrary",),
                                             vmem_limit_bytes=VMEM_LIMIT),
        name="combine",
    )(blocked(dest1), blocked(dest2), blocked(dest1), blocked(dest2), x, mf, g_final, y_slab)


def _expert_tiles(counts, *, tm, cap, n_tiles):
    tiles = (counts + tm - 1) // tm
    ends = jnp.cumsum(tiles)
    t = jnp.arange(n_tiles, dtype=jnp.int32)
    tc = jnp.minimum(t, ends[-1] - 1)
    e = jnp.sum((tc[:, None] >= ends[None, :]).astype(jnp.int32), axis=1)
    k = tc - (ends - tiles)[e]
    tblk = e * (cap // tm) + k
    tn = jnp.where(t < ends[-1], jnp.minimum(counts[e] - k * tm, tm), 0)
    return tblk.astype(jnp.int32), e.astype(jnp.int32), tn.astype(jnp.int32)


def _moe_final(x, g_ffn, wr, br, wg, wu, wd, g_final, *, m_prompt):
    m = x.shape[0]
    route_rows, tm, tf, out_rows, in_rows = 512, 512, 1792, 1024, 256
    cap = -(-m // tm) * tm
    h_slab, mi, mf, cnt = _route(x, g_ffn, wr, br, tm=route_rows, cap=cap)
    dest1, dest2 = mi[0], mi[1]
    counts = cnt[0, :N_EXPERTS].astype(jnp.int32)
    n_tiles = 2 * m // tm + N_EXPERTS
    tblk, texp, tn = _expert_tiles(counts, tm=tm, cap=cap, n_tiles=n_tiles)
    xs_slab = _dispatch(h_slab, dest1, dest2, rows=out_rows, cap_rows=N_EXPERTS * cap)
    y_slab = _experts(xs_slab, tblk, texp, tn, wg, wu, wd, tm=tm, tf=tf)
    return _combine(x, mf, g_final, y_slab, dest1, dest2, rows=in_rows, m_prompt=m_prompt)


def _row(v):
    return v.reshape(1, -1)


def kernel(x_prompt, x_sample, state_conv_a, state_conv_b, norm_mix_g, w_in, conv_a_w, conv_a_b, ln_a_g, ln_a_b, conv_b_w, ln_c_g, ln_c_b, w_spatial, b_spatial, w_proj_a, w_proj_b, w_proj_c, w_out, norm_ffn_g, ffn_w_gate, ffn_w_up, ffn_w_down, w_router, b_router, moe_w_gate, moe_w_up, moe_w_down, norm_final_g):
    depth = w_in.shape[0]
    n_p, t_p, _ = x_prompt.shape
    n_s, t_s, _ = x_sample.shape
    seqs_per_chunk = CHUNK // t_s

    zeros_a = jnp.zeros((n_p, A_WIDTH - 1, D_BR), F32)
    zeros_b = jnp.zeros((n_p, B_WIDTH - 1, D_BR), F32)

    m_p, m_s = n_p * t_p, n_s * t_s
    m = m_p + m_s
    x_p, row_p = x_prompt.reshape(m_p, D_MODEL), 0
    x_s, row_s = x_sample.reshape(m_s, D_MODEL), 0
    states_p = states_s = None
    for l in range(depth):
        bsp_p = jnp.repeat(b_spatial[l].T, LANES, axis=1)
        eye = jnp.eye(seqs_per_chunk, dtype=F32)
        wsp_s = jnp.einsum("ab,hts->hatbs", eye, w_spatial[l][:, :t_s, :t_s]).reshape(
            C_HEADS, CHUNK, CHUNK)
        bsp_s = jnp.tile(bsp_p[:t_s], (seqs_per_chunk, 1))
        w = dict(g=_row(norm_mix_g[l]), w_in=w_in[l].astype(BF16), conv_a_w=conv_a_w[l],
                 conv_a_b=_row(conv_a_b[l]), ln_a_g=_row(ln_a_g[l]), ln_a_b=_row(ln_a_b[l]),
                 conv_b_w=conv_b_w[l], ln_c_g=_row(ln_c_g[l]), ln_c_b=_row(ln_c_b[l]),
                 w_proj_a=w_proj_a[l].astype(BF16), w_proj_b=w_proj_b[l].astype(BF16),
                 w_proj_c=w_proj_c[l].astype(BF16), w_out=w_out[l].astype(BF16))
        x, *states_p = _mixer(x_p, zeros_a, zeros_b, dict(w, wsp=w_spatial[l], bsp=bsp_p),
                              t=t_p, nb=1, tt=512, in_row=row_p, out_row=0, out_rows=m,
                              layer=l, depth=depth, state_bufs=states_p)
        x, *states_s = _mixer(x_s, state_conv_a[l], state_conv_b[l],
                              dict(w, wsp=wsp_s, bsp=bsp_s), t=t_s, nb=16, tt=t_s,
                              in_row=row_s, out_row=m_p, out_rows=m, layer=l, depth=depth,
                              out_buf=x, state_bufs=states_s)

        i = l // 2
        g_ffn = _row(norm_ffn_g[l])
        if l % 2 == 0:
            x = _ffn(x, g_ffn, ffn_w_gate[i].astype(BF16), ffn_w_up[i].astype(BF16),
                     ffn_w_down[i].astype(BF16), tm=512, tf=1792)
            x_p, row_p, x_s, row_s = x, 0, x, m_p
        else:
            assert l == depth - 1, "the routed layer applies the final norm"
            wr = jnp.pad(w_router[i], ((0, 0), (0, LANES - N_EXPERTS)))
            br = jnp.pad(_row(b_router[i]), ((0, 0), (0, LANES - N_EXPERTS)))
            y_p, y_s = _moe_final(x, g_ffn, wr, br, moe_w_gate[i].astype(BF16),
                                  moe_w_up[i].astype(BF16), moe_w_down[i].astype(BF16),
                                  _row(norm_final_g), m_prompt=m_p)
    y_p = y_p.reshape(n_p, t_p, D_MODEL)
    y_s = y_s.reshape(n_s, t_s, D_MODEL)

    (a_p, b_p, v_p), (a_s, b_s, v_s) = states_p, states_s
    return (y_p, y_s, a_p, a_s, b_p, b_s, v_p, v_s)
```

```python
import functools

import jax
import jax.numpy as jnp
from jax import lax
from jax.experimental import pallas as pl
from jax.experimental.pallas import tpu as pltpu

D_MODEL = 1024
D_BR = 512
A_WIDTH = 31
B_WIDTH = 3
C_HEADS = 4
CHUNK = 128
D_FF = 3584
N_EXPERTS = 8
EPS = 1e-6

A_HIST = 32
B_HIST = 8
LANES = 128
SUBLANES = 8
CONV_ROWS = 32
VMEM_LIMIT = 56 * 1024 * 1024

BF16 = jnp.bfloat16
F32 = jnp.float32


def _rms(x, g):
    return x * lax.rsqrt(jnp.mean(x * x, -1, keepdims=True) + EPS) * g


def _layer_norm(x, g, b):
    mu = jnp.mean(x, -1, keepdims=True)
    xc = x - mu
    var = jnp.mean(xc * xc, -1, keepdims=True)
    return xc * lax.rsqrt(var + EPS) * g + b


def _sigmoid(x):
    return 0.5 * jnp.tanh(0.5 * x) + 0.5


def _dot(a, b):
    return jnp.dot(a, b, preferred_element_type=F32)


def _dot_split(a, b):
    a_hi = a.astype(BF16)
    b_hi = b.astype(BF16)
    a_lo = (a - a_hi.astype(F32)).astype(BF16)
    b_lo = (b - b_hi.astype(F32)).astype(BF16)
    return _dot(a_hi, b_hi) + (_dot(a_hi, b_lo) + _dot(a_lo, b_hi))


def _mixer_kernel(x_ref, pa_ref, pb_ref, g_ref, win_ref, caw_ref, cab_ref, lag_ref, lab_ref,
                  cbw_ref, lcg_ref, lcb_ref, wsp_ref, bsp_ref, wpa_ref, wpb_ref, wpc_ref,
                  wout_ref, xo_ref, na_ref, nb_ref, nv_ref, fa, fz, sc, sh, ca, wb, *, nb, tt, nv_rows):
    i = pl.program_id(1)
    rows = nb * tt

    @pl.when(i == 0)
    def _():
        fa[:, A_HIST - (A_WIDTH - 1):A_HIST, :] = pa_ref[...]
        fz[:, B_HIST - (B_WIDTH - 1):B_HIST, :] = pb_ref[...]

    x = x_ref[...]
    hb = _rms(x, g_ref[...]).astype(BF16)

    def proj(c0, c1):
        return _dot(hb, win_ref[:, c0:c1])

    a = proj(0, D_BR) * _sigmoid(proj(D_BR, 2 * D_BR))
    fa[:, A_HIST:A_HIST + tt, :] = a.reshape(nb, tt, D_BR)
    off_a = A_HIST - (A_WIDTH - 1)
    for j in range(SUBLANES):
        span = tt + SUBLANES * ((A_WIDTH - 1 - j) // SUBLANES)
        sh[j, :, 0:span, :] = fa[:, off_a + j:off_a + j + span, :]
    for k in range(A_WIDTH):
        wb[k] = jnp.broadcast_to(caw_ref[k:k + 1, :], (SUBLANES, D_BR))
    rb = min(tt, CONV_ROWS)
    sb = CONV_ROWS // rb
    n_rb = tt // rb

    def conv_step(u, carry):
        s0 = (u // n_rb) * sb
        r0 = (u % n_rb) * rb
        groups = range(0, rb, SUBLANES)
        acc = [jnp.broadcast_to(cab_ref[...], (sb, SUBLANES, D_BR)) for _ in groups]
        for k in range(A_WIDTH):
            q, j = divmod(k, SUBLANES)
            w_k = wb[k]
            for gi, g in enumerate(groups):
                lo = r0 + g + SUBLANES * q
                acc[gi] = acc[gi] + w_k * sh[j, s0:s0 + sb, lo:lo + SUBLANES, :]
        for gi, g in enumerate(groups):
            ca[s0:s0 + sb, r0 + g:r0 + g + SUBLANES, :] = acc[gi]
        return carry

    for u in range((nb // sb) * n_rb):
        conv_step(u, 0)
    a_ln = _layer_norm(ca[...].reshape(rows, D_BR), lag_ref[...], lab_ref[...])
    a_out = _dot((a_ln * _sigmoid(a_ln)).astype(BF16), wpa_ref[...])
    na_ref[...] = fa[:, tt + A_HIST - (A_WIDTH - 1):tt + A_HIST, :]
    fa[:, 0:A_HIST, :] = fa[:, tt:tt + A_HIST, :]

    c0 = 2 * D_BR
    z = proj(c0 + 2 * D_BR, c0 + 3 * D_BR) * proj(c0, c0 + D_BR)
    fz[:, B_HIST:B_HIST + tt, :] = z.reshape(nb, tt, D_BR)
    off_b = B_HIST - (B_WIDTH - 1)
    accb = jnp.zeros((nb, tt, D_BR), F32)
    for k in range(B_WIDTH):
        accb = accb + cbw_ref[k:k + 1, :] * fz[:, off_b + k:off_b + k + tt, :]
    b_out = _dot((proj(c0 + D_BR, c0 + 2 * D_BR) * accb.reshape(rows, D_BR)).astype(BF16),
                 wpb_ref[...])
    nb_ref[...] = fz[:, tt + B_HIST - (B_WIDTH - 1):tt + B_HIST, :]
    fz[:, 0:B_HIST, :] = fz[:, tt:tt + B_HIST, :]

    c0 = 5 * D_BR
    v = _layer_norm(proj(c0 + D_BR, c0 + 2 * D_BR), lcg_ref[...], lcb_ref[...])
    nv_ref[...] = v[rows - nv_rows * nb:, :].reshape(nb, nv_rows, D_BR)
    vb = v.astype(BF16)
    t_idx = lax.broadcasted_iota(jnp.int32, (CHUNK, CHUNK), 0)
    s_idx = lax.broadcasted_iota(jnp.int32, (CHUNK, CHUNK), 1)
    causal = s_idx <= t_idx
    for hd in range(C_HEADS):
        wm = jnp.where(causal, wsp_ref[hd], 0.0).astype(BF16)
        for c in range(rows // CHUNK):
            blk = (slice(c * CHUNK, (c + 1) * CHUNK), slice(hd * LANES, (hd + 1) * LANES))
            sc[blk] = _dot(wm, vb[blk])
    c_out = _dot((proj(c0, c0 + D_BR) * (sc[...].reshape(rows // CHUNK, CHUNK, D_BR)
                                        + bsp_ref[...]).reshape(rows, D_BR)).astype(BF16),
                 wpc_ref[...])

    c0 = 7 * D_BR
    merged = (_sigmoid(proj(c0, c0 + D_MODEL)) * a_out
              + _sigmoid(proj(c0 + D_MODEL, c0 + 2 * D_MODEL)) * b_out
              + _sigmoid(proj(c0 + 2 * D_MODEL, c0 + 3 * D_MODEL)) * c_out)
    xo_ref[...] = x + _dot(merged.astype(BF16), wout_ref[...])


def _const_spec(shape):
    nd = len(shape)
    return pl.BlockSpec(shape, lambda b, i: (0,) * nd, pipeline_mode=pl.Buffered(1))


def _drop_refs(kernel_fn, pos, count):
    def body(*refs):
        return kernel_fn(*refs[:pos], *refs[pos + count:])
    return body


def _mixer(x, past_a, past_b, w, *, t, nb, tt, in_row, out_row, out_rows, layer, depth,
           out_buf=None, state_bufs=None):
    n = past_a.shape[0]
    rows = nb * tt
    assert n % nb == 0 and t % tt == 0 and rows % CHUNK == 0
    assert in_row % rows == 0 and out_row % rows == 0
    assert tt % SUBLANES == 0 and (tt % CONV_ROWS == 0 or
                                   (CONV_ROWS % tt == 0 and nb % (CONV_ROWS // tt) == 0))
    nv_rows = t - ((t - 1) // CHUNK) * CHUNK
    assert nv_rows <= tt
    n_t = t // tt
    consts = [w["g"], w["w_in"], w["conv_a_w"], w["conv_a_b"], w["ln_a_g"], w["ln_a_b"],
              w["conv_b_w"], w["ln_c_g"], w["ln_c_b"], w["wsp"], w["bsp"], w["w_proj_a"],
              w["w_proj_b"], w["w_proj_c"], w["w_out"]]
    seq_spec = lambda r, c: pl.BlockSpec((nb, r, c), lambda b, i: (b, 0, 0))
    row_spec = lambda first: pl.BlockSpec((rows, D_MODEL),
                                          lambda b, i: (first // rows + b * n_t + i, 0))
    body = functools.partial(_mixer_kernel, nb=nb, tt=tt, nv_rows=nv_rows)
    operands = [x, past_a, past_b, *consts]
    in_specs = ([row_spec(in_row), seq_spec(A_WIDTH - 1, D_BR), seq_spec(B_WIDTH - 1, D_BR)]
                + [_const_spec(c.shape) for c in consts])
    state_rows = (A_WIDTH - 1, B_WIDTH - 1, nv_rows)
    out_shape = [jax.ShapeDtypeStruct((out_rows, D_MODEL), F32)] + [
        jax.ShapeDtypeStruct((depth, n, r, D_BR), F32) for r in state_rows]
    n_in = len(operands)
    aliases = {}
    for k, carrier in enumerate((out_buf, *(state_bufs or (None,) * 3))):
        if carrier is not None:
            assert carrier.shape == out_shape[k].shape
            aliases[len(operands)] = k
            operands.append(carrier)
            in_specs.append(pl.BlockSpec(memory_space=pl.ANY))
    body = _drop_refs(body, n_in, len(operands) - n_in)
    layer_spec = lambda r: pl.BlockSpec((None, nb, r, D_BR), lambda b, i: (layer, b, 0, 0))
    return pl.pallas_call(
        body,
        grid=(n // nb, n_t),
        in_specs=in_specs,
        out_specs=[row_spec(out_row)] + [layer_spec(r) for r in state_rows],
        input_output_aliases=aliases,
        out_shape=out_shape,
        scratch_shapes=[pltpu.VMEM((nb, A_HIST + tt, D_BR), F32),
                        pltpu.VMEM((nb, B_HIST + tt, D_BR), F32),
                        pltpu.VMEM((nb * tt, D_BR), F32),
                        pltpu.VMEM((SUBLANES, nb, tt + A_HIST - SUBLANES, D_BR), F32),
                        pltpu.VMEM((nb, tt, D_BR), F32),
                        pltpu.VMEM((A_WIDTH, SUBLANES, D_BR), F32)],
        compiler_params=pltpu.CompilerParams(
            dimension_semantics=("arbitrary", "arbitrary"), vmem_limit_bytes=VMEM_LIMIT),
        name="mixer",
    )(*operands)


def _swiglu_part(h_b, wg_ref, wu_ref, wd_ref):
    gate = _dot(h_b, wg_ref[...])
    act = gate * _sigmoid(gate) * _dot(h_b, wu_ref[...])
    return _dot(act.astype(BF16), wd_ref[...])


def _ffn_kernel(x_ref, g_ref, wg_ref, wu_ref, wd_ref, o_ref, *, tf):
    x = x_ref[...]
    h_b = _rms(x, g_ref[...]).astype(BF16)
    y = x
    for c0 in range(0, D_FF, tf):
        y = y + _swiglu_part(h_b, wg_ref.at[:, c0:c0 + tf], wu_ref.at[:, c0:c0 + tf],
                             wd_ref.at[c0:c0 + tf, :])
    o_ref[...] = y


def _ffn(x, g, wg, wu, wd, *, tm, tf):
    m, _ = x.shape
    assert m % tm == 0 and D_FF % tf == 0
    const = lambda shape: pl.BlockSpec(shape, lambda i: (0, 0), pipeline_mode=pl.Buffered(1))
    return pl.pallas_call(
        functools.partial(_ffn_kernel, tf=tf),
        grid=(m // tm,),
        in_specs=[pl.BlockSpec((tm, D_MODEL), lambda i: (i, 0)),
                  const((1, D_MODEL)), const((D_MODEL, D_FF)), const((D_MODEL, D_FF)),
                  const((D_FF, D_MODEL))],
        out_specs=pl.BlockSpec((tm, D_MODEL), lambda i: (i, 0)),
        out_shape=jax.ShapeDtypeStruct((m, D_MODEL), F32),
        compiler_params=pltpu.CompilerParams(
            dimension_semantics=("arbitrary",), vmem_limit_bytes=VMEM_LIMIT),
        name="ffn",
    )(x, g, wg, wu, wd)


SLAB = D_MODEL // LANES


def _to_slab(ref, val, rows):
    for c in range(SLAB):
        ref[pl.ds(c, rows, stride=SLAB), :] = val[:, c * LANES:(c + 1) * LANES]


def _route_kernel(x_ref, g_ref, wr_ref, br_ref, hs_ref, mi_ref, mf_ref, cnt_ref, carry, *, cap):
    i = pl.program_id(0)
    tm = x_ref.shape[0]

    @pl.when(i == 0)
    def _():
        carry[...] = jnp.zeros_like(carry)

    h = _rms(x_ref[...], g_ref[...])
    _to_slab(hs_ref, h, tm)

    col = lax.broadcasted_iota(jnp.int32, (tm, LANES), 1)
    logits = _dot_split(h, wr_ref[...]) + br_ref[...]
    logits = jnp.where(col < N_EXPERTS, logits, -jnp.inf)
    m1 = jnp.max(logits, -1, keepdims=True)
    i1 = jnp.min(jnp.where(logits == m1, col, LANES), -1, keepdims=True)
    rest = jnp.where(col == i1, -jnp.inf, logits)
    m2 = jnp.max(rest, -1, keepdims=True)
    i2 = jnp.min(jnp.where(rest == m2, col, LANES), -1, keepdims=True)
    e2 = jnp.exp(m2 - m1)
    p1 = 1.0 / (1.0 + e2)
    p2 = e2 * p1

    chosen = jnp.logical_or(col == i1, col == i2).astype(F32)
    r_idx = lax.broadcasted_iota(jnp.int32, (tm, tm), 0)
    c_idx = lax.broadcasted_iota(jnp.int32, (tm, tm), 1)
    before = (c_idx < r_idx).astype(BF16)
    rank = carry[...] + _dot(before, chosen.astype(BF16))
    dest = col.astype(F32) * float(cap) + rank
    d1 = jnp.sum(jnp.where(col == i1, dest, 0.0), -1, keepdims=True)
    d2 = jnp.sum(jnp.where(col == i2, dest, 0.0), -1, keepdims=True)
    by_choice = jnp.where(col == 0, d1, jnp.where(col == 1, d2, 0.0)).T
    mi_ref[...] = by_choice[0:SUBLANES, :].astype(jnp.int32)
    mf_ref[...] = jnp.where(col == 0, p1, jnp.where(col == 1, p2, 0.0))
    carry[...] += jnp.sum(chosen, 0, keepdims=True)
    cnt_ref[...] = carry[...]


def _route(x, g, wr, br, *, tm, cap):
    m = x.shape[0]
    assert m % tm == 0
    row = lambda i: (i, 0)
    const = lambda i: (0, 0)
    return pl.pallas_call(
        functools.partial(_route_kernel, cap=cap),
        grid=(m // tm,),
        in_specs=[pl.BlockSpec((tm, D_MODEL), row), pl.BlockSpec((1, D_MODEL), const),
                  pl.BlockSpec((D_MODEL, LANES), const), pl.BlockSpec((1, LANES), const)],
        out_specs=[pl.BlockSpec((tm * SLAB, LANES), row),
                   pl.BlockSpec((SUBLANES, tm), lambda i: (0, i)),
                   pl.BlockSpec((tm, LANES), row), pl.BlockSpec((1, LANES), const)],
        out_shape=[jax.ShapeDtypeStruct((m * SLAB, LANES), F32),
                   jax.ShapeDtypeStruct((SUBLANES, m), jnp.int32),
                   jax.ShapeDtypeStruct((m, LANES), F32),
                   jax.ShapeDtypeStruct((1, LANES), F32)],
        scratch_shapes=[pltpu.VMEM((1, LANES), F32)],
        compiler_params=pltpu.CompilerParams(dimension_semantics=("arbitrary",),
                                             vmem_limit_bytes=VMEM_LIMIT),
        name="route",
    )(x, g, wr, br)


def _row_copy(src_ref, src_row, dst_ref, dst_row, sem):
    return pltpu.make_async_copy(
        src_ref.at[pl.ds(pl.multiple_of(src_row * SLAB, SLAB), SLAB), :],
        dst_ref.at[pl.ds(pl.multiple_of(dst_row * SLAB, SLAB), SLAB), :], sem)


def _wait_rows(src_ref, dst_ref, n_rows, sem):
    pltpu.make_async_copy(src_ref.at[pl.ds(0, n_rows * SLAB), :],
                          dst_ref.at[pl.ds(0, n_rows * SLAB), :], sem).wait()


def _dispatch_kernel(idx1_ref, idx2_ref, src_ref, dst_ref, sem, *, rows):
    def issue(r, c):
        for k, idx_ref in enumerate((idx1_ref, idx2_ref)):
            _row_copy(src_ref, r, dst_ref, idx_ref[0, 0, r], sem).start(priority=k)
        return c

    lax.fori_loop(0, rows, issue, 0)
    for k in range(2):
        _wait_rows(src_ref, dst_ref, rows, sem)


def _idx_spec(rows, index_map):
    return pl.BlockSpec((1, 1, rows), index_map, memory_space=pltpu.SMEM)


def _dispatch(h_slab, dest1, dest2, *, rows, cap_rows):
    n_blk = dest1.shape[0] // rows
    blocked = lambda d: d.reshape(n_blk, 1, rows)
    return pl.pallas_call(
        functools.partial(_dispatch_kernel, rows=rows),
        grid=(n_blk,),
        in_specs=[_idx_spec(rows, lambda i: (i, 0, 0)), _idx_spec(rows, lambda i: (i, 0, 0)),
                  pl.BlockSpec((rows * SLAB, LANES), lambda i: (i, 0))],
        out_specs=pl.BlockSpec(memory_space=pl.ANY),
        out_shape=jax.ShapeDtypeStruct((cap_rows * SLAB, LANES), F32),
        scratch_shapes=[pltpu.SemaphoreType.DMA(())],
        compiler_params=pltpu.CompilerParams(dimension_semantics=("arbitrary",)),
        name="dispatch",
    )(blocked(dest1), blocked(dest2), h_slab)


W_ROWS, W_COLS = 256, 512
W_SLOTS = 8


def _load_expert_weights(e, srcs, dsts, stage, sem):
    jobs = []
    for src, dst in zip(srcs, dsts):
        n_r, n_c = dst.shape
        for r0 in range(0, n_r, W_ROWS):
            for c0 in range(0, n_c, W_COLS):
                jobs.append((src.at[e, pl.ds(r0, W_ROWS), pl.ds(c0, W_COLS)],
                             dst.at[r0:r0 + W_ROWS, c0:c0 + W_COLS]))
    copies = [pltpu.make_async_copy(src, stage.at[i % W_SLOTS], sem.at[i % W_SLOTS])
              for i, (src, _) in enumerate(jobs)]
    ahead = W_SLOTS - 1
    for cp in copies[:ahead]:
        cp.start()
    for i, (cp, (_, dst)) in enumerate(zip(copies, jobs)):
        cp.wait()
        dst[...] = stage[i % W_SLOTS].astype(BF16)
        if i + ahead < len(copies):
            copies[i + ahead].start()


def _experts_kernel(tblk, texp, tn, tnew, xs_ref, wg_hbm, wu_hbm, wd_hbm, o_ref,
                    hb, wg_ref, wu_ref, wd_ref, stage, sem, *, tf):
    t = pl.program_id(0)
    n = tn[t]
    tm = hb.shape[0]

    @pl.when(tnew[t] == 1)
    def _():
        _load_expert_weights(texp[t], (wg_hbm, wu_hbm, wd_hbm), (wg_ref, wu_ref, wd_ref),
                             stage, sem)

    @pl.when(n > 0)
    def _():
        valid = lax.broadcasted_iota(jnp.int32, (tm, LANES), 0) < n
        for c in range(SLAB):
            xc = xs_ref[pl.ds(c, tm, stride=SLAB), :]
            hb[:, c * LANES:(c + 1) * LANES] = jnp.where(valid, xc, 0.0).astype(BF16)
        h_b = hb[...]
        y = _swiglu_part(h_b, wg_ref.at[:, 0:tf], wu_ref.at[:, 0:tf], wd_ref.at[0:tf, :])
        for c0 in range(tf, D_FF, tf):
            y = y + _swiglu_part(h_b, wg_ref.at[:, c0:c0 + tf], wu_ref.at[:, c0:c0 + tf],
                                 wd_ref.at[c0:c0 + tf, :])
        _to_slab(o_ref, y, tm)


def _experts(xs_slab, tblk, texp, tn, tnew, wg, wu, wd, *, tm, tf):
    n_tiles = tblk.shape[0]
    assert D_FF % tf == 0 and D_MODEL % W_ROWS == 0 and D_FF % W_ROWS == 0
    assert D_MODEL % W_COLS == 0 and D_FF % W_COLS == 0
    tile = lambda t, tblk, texp, tn, tnew: (tblk[t], 0)
    hbm = pl.BlockSpec(memory_space=pl.ANY)
    grid_spec = pltpu.PrefetchScalarGridSpec(
        num_scalar_prefetch=4,
        grid=(n_tiles,),
        in_specs=[pl.BlockSpec((tm * SLAB, LANES), tile), hbm, hbm, hbm],
        out_specs=pl.BlockSpec((tm * SLAB, LANES), tile),
        scratch_shapes=[pltpu.VMEM((tm, D_MODEL), BF16),
                        pltpu.VMEM((D_MODEL, D_FF), BF16), pltpu.VMEM((D_MODEL, D_FF), BF16),
                        pltpu.VMEM((D_FF, D_MODEL), BF16),
                        pltpu.VMEM((W_SLOTS, W_ROWS, W_COLS), F32),
                        pltpu.SemaphoreType.DMA((W_SLOTS,))])
    return pl.pallas_call(
        functools.partial(_experts_kernel, tf=tf),
        grid_spec=grid_spec,
        out_shape=jax.ShapeDtypeStruct(xs_slab.shape, F32),
        compiler_params=pltpu.CompilerParams(dimension_semantics=("arbitrary",),
                                             vmem_limit_bytes=VMEM_LIMIT),
        name="experts",
    )(tblk, texp, tn, tnew, xs_slab, wg, wu, wd)


def _combine_kernel(idx1_ref, idx2_ref, idx1_next_ref, idx2_next_ref, x_ref, mf_ref, gf_ref,
                    ys_ref, op_ref, os_ref, buf, ysum, sem, *, rows, n_prompt_tiles):
    i = pl.program_id(0)
    n_i = pl.num_programs(0)
    slot = i % 2

    def issue(refs, s):
        def body(r, c):
            for k, ref in enumerate(refs):
                _row_copy(ys_ref, ref[0, 0, r], buf.at[s], k * rows + r,
                          sem.at[s]).start(priority=k)
            return c
        lax.fori_loop(0, rows, body, 0, unroll=4)

    def finish(s):
        _wait_rows(ys_ref, buf.at[s], 2 * rows, sem.at[s])
        p1 = mf_ref[:, 0:1]
        p2 = mf_ref[:, 1:2]
        for c in range(SLAB):
            cs = slice(c * LANES, (c + 1) * LANES)
            y1 = buf[s, pl.ds(c, rows, stride=SLAB), :]
            y2 = buf[s, pl.ds(rows * SLAB + c, rows, stride=SLAB), :]
            ysum[:, cs] = x_ref[:, cs] + p1 * y1 + p2 * y2
        res = _rms(ysum[...], gf_ref[...])

        @pl.when(i < n_prompt_tiles)
        def _():
            op_ref[...] = res

        @pl.when(i >= n_prompt_tiles)
        def _():
            os_ref[...] = res

    for s in range(2):
        @pl.when(slot == s)
        def _():
            @pl.when(i == 0)
            def _():
                issue((idx1_ref, idx2_ref), s)

            @pl.when(i + 1 < n_i)
            def _():
                issue((idx1_next_ref, idx2_next_ref), 1 - s)

            finish(s)


def _combine(x, mf, g_final, y_slab, dest1, dest2, *, rows, m_prompt):
    m = x.shape[0]
    n_blk = m // rows
    assert m == n_blk * rows and m_prompt % rows == 0
    n_p = m_prompt // rows
    blocked = lambda d: d.reshape(n_blk, 1, rows)
    this = lambda i: (i, 0, 0)
    ahead = lambda i: (jnp.minimum(i + 1, n_blk - 1), 0, 0)
    return pl.pallas_call(
        functools.partial(_combine_kernel, rows=rows, n_prompt_tiles=n_p),
        grid=(n_blk,),
        in_specs=[_idx_spec(rows, this), _idx_spec(rows, this),
                  _idx_spec(rows, ahead), _idx_spec(rows, ahead),
                  pl.BlockSpec((rows, D_MODEL), lambda i: (i, 0)),
                  pl.BlockSpec((rows, LANES), lambda i: (i, 0)),
                  pl.BlockSpec((1, D_MODEL), lambda i: (0, 0)),
                  pl.BlockSpec(memory_space=pl.ANY)],
        out_specs=[pl.BlockSpec((rows, D_MODEL), lambda i: (jnp.minimum(i, n_p - 1), 0)),
                   pl.BlockSpec((rows, D_MODEL), lambda i: (jnp.maximum(i - n_p, 0), 0))],
        out_shape=[jax.ShapeDtypeStruct((m_prompt, D_MODEL), F32),
                   jax.ShapeDtypeStruct((m - m_prompt, D_MODEL), F32)],
        scratch_shapes=[pltpu.VMEM((2, 2 * rows * SLAB, LANES), F32),
                        pltpu.VMEM((rows, D_MODEL), F32),
                        pltpu.SemaphoreType.DMA((2,))],
        compiler_params=pltpu.CompilerParams(dimension_semantics=("arbitrary",),
                                             vmem_limit_bytes=VMEM_LIMIT),
        name="combine",
    )(blocked(dest1), blocked(dest2), blocked(dest1), blocked(dest2), x, mf, g_final, y_slab)


def _expert_tiles(counts, *, tm, cap, n_tiles):
    tiles = (counts + tm - 1) // tm
    ends = jnp.cumsum(tiles)
    t = jnp.arange(n_tiles, dtype=jnp.int32)
    tc = jnp.minimum(t, ends[-1] - 1)
    e = jnp.sum((tc[:, None] >= ends[None, :]).astype(jnp.int32), axis=1)
    k = tc - (ends - tiles)[e]
    tblk = e * (cap // tm) + k
    tn = jnp.where(t < ends[-1], jnp.minimum(counts[e] - k * tm, tm), 0)
    tnew = jnp.logical_and(t < ends[-1], k == 0)
    i32 = lambda v: v.astype(jnp.int32)
    return i32(tblk), i32(e), i32(tn), i32(tnew)


def _moe_final(x, g_ffn, wr, br, wg, wu, wd, g_final, *, m_prompt):
    m = x.shape[0]
    route_rows, tm, tf, out_rows, in_rows = 512, 512, 1792, 1024, 256
    cap = -(-m // tm) * tm
    h_slab, mi, mf, cnt = _route(x, g_ffn, wr, br, tm=route_rows, cap=cap)
    dest1, dest2 = mi[0], mi[1]
    counts = cnt[0, :N_EXPERTS].astype(jnp.int32)
    n_tiles = 2 * m // tm + N_EXPERTS
    tables = _expert_tiles(counts, tm=tm, cap=cap, n_tiles=n_tiles)
    xs_slab = _dispatch(h_slab, dest1, dest2, rows=out_rows, cap_rows=N_EXPERTS * cap)
    y_slab = _experts(xs_slab, *tables, wg, wu, wd, tm=tm, tf=tf)
    return _combine(x, mf, g_final, y_slab, dest1, dest2, rows=in_rows, m_prompt=m_prompt)


def _row(v):
    return v.reshape(1, -1)


def kernel(x_prompt, x_sample, state_conv_a, state_conv_b, norm_mix_g, w_in, conv_a_w, conv_a_b, ln_a_g, ln_a_b, conv_b_w, ln_c_g, ln_c_b, w_spatial, b_spatial, w_proj_a, w_proj_b, w_proj_c, w_out, norm_ffn_g, ffn_w_gate, ffn_w_up, ffn_w_down, w_router, b_router, moe_w_gate, moe_w_up, moe_w_down, norm_final_g):
    depth = w_in.shape[0]
    n_p, t_p, _ = x_prompt.shape
    n_s, t_s, _ = x_sample.shape
    seqs_per_chunk = CHUNK // t_s

    zeros_a = jnp.zeros((n_p, A_WIDTH - 1, D_BR), F32)
    zeros_b = jnp.zeros((n_p, B_WIDTH - 1, D_BR), F32)

    m_p, m_s = n_p * t_p, n_s * t_s
    m = m_p + m_s
    x_p, row_p = x_prompt.reshape(m_p, D_MODEL), 0
    x_s, row_s = x_sample.reshape(m_s, D_MODEL), 0
    states_p = states_s = None
    for l in range(depth):
        bsp_p = jnp.repeat(b_spatial[l].T, LANES, axis=1)
        eye = jnp.eye(seqs_per_chunk, dtype=F32)
        wsp_s = jnp.einsum("ab,hts->hatbs", eye, w_spatial[l][:, :t_s, :t_s]).reshape(
            C_HEADS, CHUNK, CHUNK)
        bsp_s = jnp.tile(bsp_p[:t_s], (seqs_per_chunk, 1))
        w = dict(g=_row(norm_mix_g[l]), w_in=w_in[l].astype(BF16), conv_a_w=conv_a_w[l],
                 conv_a_b=_row(conv_a_b[l]), ln_a_g=_row(ln_a_g[l]), ln_a_b=_row(ln_a_b[l]),
                 conv_b_w=conv_b_w[l], ln_c_g=_row(ln_c_g[l]), ln_c_b=_row(ln_c_b[l]),
                 w_proj_a=w_proj_a[l].astype(BF16), w_proj_b=w_proj_b[l].astype(BF16),
                 w_proj_c=w_proj_c[l].astype(BF16), w_out=w_out[l].astype(BF16))
        x, *states_p = _mixer(x_p, zeros_a, zeros_b, dict(w, wsp=w_spatial[l], bsp=bsp_p),
                              t=t_p, nb=1, tt=512, in_row=row_p, out_row=0, out_rows=m,
                              layer=l, depth=depth, state_bufs=states_p)
        x, *states_s = _mixer(x_s, state_conv_a[l], state_conv_b[l],
                              dict(w, wsp=wsp_s, bsp=bsp_s), t=t_s, nb=16, tt=t_s,
                              in_row=row_s, out_row=m_p, out_rows=m, layer=l, depth=depth,
                              out_buf=x, state_bufs=states_s)

        i = l // 2
        g_ffn = _row(norm_ffn_g[l])
        if l % 2 == 0:
            x = _ffn(x, g_ffn, ffn_w_gate[i].astype(BF16), ffn_w_up[i].astype(BF16),
                     ffn_w_down[i].astype(BF16), tm=512, tf=1792)
            x_p, row_p, x_s, row_s = x, 0, x, m_p
        else:
            assert l == depth - 1, "the routed layer applies the final norm"
            wr = jnp.pad(w_router[i], ((0, 0), (0, LANES - N_EXPERTS)))
            br = jnp.pad(_row(b_router[i]), ((0, 0), (0, LANES - N_EXPERTS)))
            y_p, y_s = _moe_final(x, g_ffn, wr, br, moe_w_gate[i], moe_w_up[i], moe_w_down[i],
                                  _row(norm_final_g), m_prompt=m_p)
    y_p = y_p.reshape(n_p, t_p, D_MODEL)
    y_s = y_s.reshape(n_s, t_s, D_MODEL)

    (a_p, b_p, v_p), (a_s, b_s, v_s) = states_p, states_s
    return (y_p, y_s, a_p, a_s, b_p, b_s, v_p, v_s)
```

```python
import functools

import jax
import jax.numpy as jnp
from jax import lax
from jax.experimental import pallas as pl
from jax.experimental.pallas import tpu as pltpu

D_MODEL = 1024
D_BR = 512
A_WIDTH = 31
B_WIDTH = 3
C_HEADS = 4
CHUNK = 128
D_FF = 3584
N_EXPERTS = 8
EPS = 1e-6

A_HIST = 32
B_HIST = 8
LANES = 128
SUBLANES = 8
CONV_ROWS = 32
VMEM_LIMIT = 56 * 1024 * 1024

BF16 = jnp.bfloat16
F32 = jnp.float32


def _rms(x, g):
    return x * lax.rsqrt(jnp.mean(x * x, -1, keepdims=True) + EPS) * g


def _layer_norm(x, g, b):
    mu = jnp.mean(x, -1, keepdims=True)
    xc = x - mu
    var = jnp.mean(xc * xc, -1, keepdims=True)
    return xc * lax.rsqrt(var + EPS) * g + b


def _sigmoid(x):
    return 0.5 * jnp.tanh(0.5 * x) + 0.5


def _dot(a, b):
    return jnp.dot(a, b, preferred_element_type=F32)


def _dot_split(a, b):
    a_hi = a.astype(BF16)
    b_hi = b.astype(BF16)
    a_lo = (a - a_hi.astype(F32)).astype(BF16)
    b_lo = (b - b_hi.astype(F32)).astype(BF16)
    return _dot(a_hi, b_hi) + (_dot(a_hi, b_lo) + _dot(a_lo, b_hi))


def _mixer_kernel(x_ref, pa_ref, pb_ref, g_ref, win_ref, caw_ref, cab_ref, lag_ref, lab_ref,
                  cbw_ref, lcg_ref, lcb_ref, wsp_ref, bsp_ref, wpa_ref, wpb_ref, wpc_ref,
                  wout_ref, xo_ref, na_ref, nb_ref, nv_ref, fa, fz, sc, sh, ca, wb, *, nb, tt, nv_rows):
    i = pl.program_id(1)
    rows = nb * tt

    @pl.when(i == 0)
    def _():
        fa[:, A_HIST - (A_WIDTH - 1):A_HIST, :] = pa_ref[...]
        fz[:, B_HIST - (B_WIDTH - 1):B_HIST, :] = pb_ref[...]

    x = x_ref[...]
    hb = _rms(x, g_ref[...]).astype(BF16)

    def proj(c0, c1):
        return _dot(hb, win_ref[:, c0:c1])

    a = proj(0, D_BR) * _sigmoid(proj(D_BR, 2 * D_BR))
    fa[:, A_HIST:A_HIST + tt, :] = a.reshape(nb, tt, D_BR)
    off_a = A_HIST - (A_WIDTH - 1)
    for j in range(SUBLANES):
        span = tt + SUBLANES * ((A_WIDTH - 1 - j) // SUBLANES)
        sh[j, :, 0:span, :] = fa[:, off_a + j:off_a + j + span, :]
    for k in range(A_WIDTH):
        wb[k] = jnp.broadcast_to(caw_ref[k:k + 1, :], (SUBLANES, D_BR))
    rb = min(tt, CONV_ROWS)
    sb = CONV_ROWS // rb
    n_rb = tt // rb

    def conv_step(u, carry):
        s0 = (u // n_rb) * sb
        r0 = (u % n_rb) * rb
        groups = range(0, rb, SUBLANES)
        acc = [jnp.broadcast_to(cab_ref[...], (sb, SUBLANES, D_BR)) for _ in groups]
        for k in range(A_WIDTH):
            q, j = divmod(k, SUBLANES)
            w_k = wb[k]
            for gi, g in enumerate(groups):
                lo = r0 + g + SUBLANES * q
                acc[gi] = acc[gi] + w_k * sh[j, s0:s0 + sb, lo:lo + SUBLANES, :]
        for gi, g in enumerate(groups):
            ca[s0:s0 + sb, r0 + g:r0 + g + SUBLANES, :] = acc[gi]
        return carry

    for u in range((nb // sb) * n_rb):
        conv_step(u, 0)
    a_ln = _layer_norm(ca[...].reshape(rows, D_BR), lag_ref[...], lab_ref[...])
    a_out = _dot((a_ln * _sigmoid(a_ln)).astype(BF16), wpa_ref[...])
    na_ref[...] = fa[:, tt + A_HIST - (A_WIDTH - 1):tt + A_HIST, :]
    fa[:, 0:A_HIST, :] = fa[:, tt:tt + A_HIST, :]

    c0 = 2 * D_BR
    z = proj(c0 + 2 * D_BR, c0 + 3 * D_BR) * proj(c0, c0 + D_BR)
    fz[:, B_HIST:B_HIST + tt, :] = z.reshape(nb, tt, D_BR)
    off_b = B_HIST - (B_WIDTH - 1)
    accb = jnp.zeros((nb, tt, D_BR), F32)
    for k in range(B_WIDTH):
        accb = accb + cbw_ref[k:k + 1, :] * fz[:, off_b + k:off_b + k + tt, :]
    b_out = _dot((proj(c0 + D_BR, c0 + 2 * D_BR) * accb.reshape(rows, D_BR)).astype(BF16),
                 wpb_ref[...])
    nb_ref[...] = fz[:, tt + B_HIST - (B_WIDTH - 1):tt + B_HIST, :]
    fz[:, 0:B_HIST, :] = fz[:, tt:tt + B_HIST, :]

    c0 = 5 * D_BR
    v = _layer_norm(proj(c0 + D_BR, c0 + 2 * D_BR), lcg_ref[...], lcb_ref[...])
    nv_ref[...] = v[rows - nv_rows * nb:, :].reshape(nb, nv_rows, D_BR)
    vb = v.astype(BF16)
    t_idx = lax.broadcasted_iota(jnp.int32, (CHUNK, CHUNK), 0)
    s_idx = lax.broadcasted_iota(jnp.int32, (CHUNK, CHUNK), 1)
    causal = s_idx <= t_idx
    for hd in range(C_HEADS):
        wm = jnp.where(causal, wsp_ref[hd], 0.0).astype(BF16)
        for c in range(rows // CHUNK):
            blk = (slice(c * CHUNK, (c + 1) * CHUNK), slice(hd * LANES, (hd + 1) * LANES))
            sc[blk] = _dot(wm, vb[blk])
    c_out = _dot((proj(c0, c0 + D_BR) * (sc[...].reshape(rows // CHUNK, CHUNK, D_BR)
                                        + bsp_ref[...]).reshape(rows, D_BR)).astype(BF16),
                 wpc_ref[...])

    c0 = 7 * D_BR
    merged = (_sigmoid(proj(c0, c0 + D_MODEL)) * a_out
              + _sigmoid(proj(c0 + D_MODEL, c0 + 2 * D_MODEL)) * b_out
              + _sigmoid(proj(c0 + 2 * D_MODEL, c0 + 3 * D_MODEL)) * c_out)
    xo_ref[...] = x + _dot(merged.astype(BF16), wout_ref[...])


def _const_spec(shape):
    nd = len(shape)
    return pl.BlockSpec(shape, lambda b, i: (0,) * nd, pipeline_mode=pl.Buffered(1))


def _drop_refs(kernel_fn, pos, count):
    def body(*refs):
        return kernel_fn(*refs[:pos], *refs[pos + count:])
    return body


def _mixer(x, past_a, past_b, w, *, t, nb, tt, in_row, out_row, out_rows, layer, depth,
           past_layer, out_buf=None, state_bufs=None):
    n = past_a.shape[1]
    rows = nb * tt
    assert n % nb == 0 and t % tt == 0 and rows % CHUNK == 0
    assert in_row % rows == 0 and out_row % rows == 0
    assert tt % SUBLANES == 0 and (tt % CONV_ROWS == 0 or
                                   (CONV_ROWS % tt == 0 and nb % (CONV_ROWS // tt) == 0))
    nv_rows = t - ((t - 1) // CHUNK) * CHUNK
    assert nv_rows <= tt
    n_t = t // tt
    past_spec = lambda r: pl.BlockSpec((None, nb, r, D_BR), lambda b, i: (past_layer, b, 0, 0))
    row_spec = lambda first: pl.BlockSpec((rows, D_MODEL),
                                          lambda b, i: (first // rows + b * n_t + i, 0))
    const_spec = lambda c, stacked: (
        pl.BlockSpec((None,) + c.shape[1:], lambda b, i: (layer, 0, 0),
                     pipeline_mode=pl.Buffered(1)) if stacked else _const_spec(c.shape))
    stacked = ("w_in", "w_proj_a", "w_proj_b", "w_proj_c", "w_out")
    names = ("g", "w_in", "conv_a_w", "conv_a_b", "ln_a_g", "ln_a_b", "conv_b_w", "ln_c_g",
             "ln_c_b", "wsp", "bsp", "w_proj_a", "w_proj_b", "w_proj_c", "w_out")
    body = functools.partial(_mixer_kernel, nb=nb, tt=tt, nv_rows=nv_rows)
    operands = [x, past_a, past_b] + [w[k] for k in names]
    in_specs = ([row_spec(in_row), past_spec(A_WIDTH - 1), past_spec(B_WIDTH - 1)]
                + [const_spec(w[k], k in stacked) for k in names])
    state_rows = (A_WIDTH - 1, B_WIDTH - 1, nv_rows)
    out_shape = [jax.ShapeDtypeStruct((out_rows, D_MODEL), F32)] + [
        jax.ShapeDtypeStruct((depth, n, r, D_BR), F32) for r in state_rows]
    n_in = len(operands)
    aliases = {}
    for k, carrier in enumerate((out_buf, *(state_bufs or (None,) * 3))):
        if carrier is not None:
            assert carrier.shape == out_shape[k].shape
            aliases[len(operands)] = k
            operands.append(carrier)
            in_specs.append(pl.BlockSpec(memory_space=pl.ANY))
    body = _drop_refs(body, n_in, len(operands) - n_in)
    layer_spec = lambda r: pl.BlockSpec((None, nb, r, D_BR), lambda b, i: (layer, b, 0, 0))
    return pl.pallas_call(
        body,
        grid=(n // nb, n_t),
        in_specs=in_specs,
        out_specs=[row_spec(out_row)] + [layer_spec(r) for r in state_rows],
        input_output_aliases=aliases,
        out_shape=out_shape,
        scratch_shapes=[pltpu.VMEM((nb, A_HIST + tt, D_BR), F32),
                        pltpu.VMEM((nb, B_HIST + tt, D_BR), F32),
                        pltpu.VMEM((nb * tt, D_BR), F32),
                        pltpu.VMEM((SUBLANES, nb, tt + A_HIST - SUBLANES, D_BR), F32),
                        pltpu.VMEM((nb, tt, D_BR), F32),
                        pltpu.VMEM((A_WIDTH, SUBLANES, D_BR), F32)],
        compiler_params=pltpu.CompilerParams(
            dimension_semantics=("arbitrary", "arbitrary"), vmem_limit_bytes=VMEM_LIMIT),
        name="mixer",
    )(*operands)


def _swiglu_part(h_b, wg_ref, wu_ref, wd_ref):
    gate = _dot(h_b, wg_ref[...])
    act = gate * _sigmoid(gate) * _dot(h_b, wu_ref[...])
    return _dot(act.astype(BF16), wd_ref[...])


def _ffn_kernel(x_ref, g_ref, wg_ref, wu_ref, wd_ref, o_ref, *, tf):
    x = x_ref[...]
    h_b = _rms(x, g_ref[...]).astype(BF16)
    y = x
    for c0 in range(0, D_FF, tf):
        y = y + _swiglu_part(h_b, wg_ref.at[:, c0:c0 + tf], wu_ref.at[:, c0:c0 + tf],
                             wd_ref.at[c0:c0 + tf, :])
    o_ref[...] = y


def _ffn(x, g, wg, wu, wd, *, tm, tf):
    m, _ = x.shape
    assert m % tm == 0 and D_FF % tf == 0
    const = lambda shape: pl.BlockSpec(shape, lambda i: (0, 0), pipeline_mode=pl.Buffered(1))
    return pl.pallas_call(
        functools.partial(_ffn_kernel, tf=tf),
        grid=(m // tm,),
        in_specs=[pl.BlockSpec((tm, D_MODEL), lambda i: (i, 0)),
                  const((1, D_MODEL)), const((D_MODEL, D_FF)), const((D_MODEL, D_FF)),
                  const((D_FF, D_MODEL))],
        out_specs=pl.BlockSpec((tm, D_MODEL), lambda i: (i, 0)),
        out_shape=jax.ShapeDtypeStruct((m, D_MODEL), F32),
        compiler_params=pltpu.CompilerParams(
            dimension_semantics=("arbitrary",), vmem_limit_bytes=VMEM_LIMIT),
        name="ffn",
    )(x, g, wg, wu, wd)


SLAB = D_MODEL // LANES


def _to_slab(ref, val, rows):
    for c in range(SLAB):
        ref[pl.ds(c, rows, stride=SLAB), :] = val[:, c * LANES:(c + 1) * LANES]


def _route_kernel(x_ref, g_ref, wr_ref, br_ref, hs_ref, mi_ref, mf_ref, cnt_ref, carry, *, cap):
    i = pl.program_id(0)
    tm = x_ref.shape[0]

    @pl.when(i == 0)
    def _():
        carry[...] = jnp.zeros_like(carry)

    h = _rms(x_ref[...], g_ref[...])
    _to_slab(hs_ref, h, tm)

    col = lax.broadcasted_iota(jnp.int32, (tm, LANES), 1)
    logits = _dot_split(h, wr_ref[...]) + br_ref[...]
    logits = jnp.where(col < N_EXPERTS, logits, -jnp.inf)
    m1 = jnp.max(logits, -1, keepdims=True)
    i1 = jnp.min(jnp.where(logits == m1, col, LANES), -1, keepdims=True)
    rest = jnp.where(col == i1, -jnp.inf, logits)
    m2 = jnp.max(rest, -1, keepdims=True)
    i2 = jnp.min(jnp.where(rest == m2, col, LANES), -1, keepdims=True)
    e2 = jnp.exp(m2 - m1)
    p1 = 1.0 / (1.0 + e2)
    p2 = e2 * p1

    chosen = jnp.logical_or(col == i1, col == i2).astype(F32)
    r_idx = lax.broadcasted_iota(jnp.int32, (tm, tm), 0)
    c_idx = lax.broadcasted_iota(jnp.int32, (tm, tm), 1)
    before = (c_idx < r_idx).astype(BF16)
    rank = carry[...] + _dot(before, chosen.astype(BF16))
    dest = col.astype(F32) * float(cap) + rank
    d1 = jnp.sum(jnp.where(col == i1, dest, 0.0), -1, keepdims=True)
    d2 = jnp.sum(jnp.where(col == i2, dest, 0.0), -1, keepdims=True)
    by_choice = jnp.where(col == 0, d1, jnp.where(col == 1, d2, 0.0)).T
    mi_ref[...] = by_choice[0:SUBLANES, :].astype(jnp.int32)
    mf_ref[...] = jnp.where(col == 0, p1, jnp.where(col == 1, p2, 0.0))
    carry[...] += jnp.sum(chosen, 0, keepdims=True)
    cnt_ref[...] = carry[...]


def _route(x, g, wr, br, *, tm, cap):
    m = x.shape[0]
    assert m % tm == 0
    row = lambda i: (i, 0)
    const = lambda i: (0, 0)
    return pl.pallas_call(
        functools.partial(_route_kernel, cap=cap),
        grid=(m // tm,),
        in_specs=[pl.BlockSpec((tm, D_MODEL), row), pl.BlockSpec((1, D_MODEL), const),
                  pl.BlockSpec((D_MODEL, LANES), const), pl.BlockSpec((1, LANES), const)],
        out_specs=[pl.BlockSpec((tm * SLAB, LANES), row),
                   pl.BlockSpec((SUBLANES, tm), lambda i: (0, i)),
                   pl.BlockSpec((tm, LANES), row), pl.BlockSpec((1, LANES), const)],
        out_shape=[jax.ShapeDtypeStruct((m * SLAB, LANES), F32),
                   jax.ShapeDtypeStruct((SUBLANES, m), jnp.int32),
                   jax.ShapeDtypeStruct((m, LANES), F32),
                   jax.ShapeDtypeStruct((1, LANES), F32)],
        scratch_shapes=[pltpu.VMEM((1, LANES), F32)],
        compiler_params=pltpu.CompilerParams(dimension_semantics=("arbitrary",),
                                             vmem_limit_bytes=VMEM_LIMIT),
        name="route",
    )(x, g, wr, br)


def _row_copy(src_ref, src_row, dst_ref, dst_row, sem):
    return pltpu.make_async_copy(
        src_ref.at[pl.ds(pl.multiple_of(src_row * SLAB, SLAB), SLAB), :],
        dst_ref.at[pl.ds(pl.multiple_of(dst_row * SLAB, SLAB), SLAB), :], sem)


def _wait_rows(src_ref, dst_ref, n_rows, sem):
    pltpu.make_async_copy(src_ref.at[pl.ds(0, n_rows * SLAB), :],
                          dst_ref.at[pl.ds(0, n_rows * SLAB), :], sem).wait()


def _dispatch_kernel(idx1_ref, idx2_ref, src_ref, dst_ref, sem, *, rows):
    def issue(r, c):
        for k, idx_ref in enumerate((idx1_ref, idx2_ref)):
            _row_copy(src_ref, r, dst_ref, idx_ref[0, 0, r], sem).start(priority=k)
        return c

    lax.fori_loop(0, rows, issue, 0)
    for k in range(2):
        _wait_rows(src_ref, dst_ref, rows, sem)


def _idx_spec(rows, index_map):
    return pl.BlockSpec((1, 1, rows), index_map, memory_space=pltpu.SMEM)


def _dispatch(h_slab, dest1, dest2, *, rows, cap_rows):
    n_blk = dest1.shape[0] // rows
    blocked = lambda d: d.reshape(n_blk, 1, rows)
    return pl.pallas_call(
        functools.partial(_dispatch_kernel, rows=rows),
        grid=(n_blk,),
        in_specs=[_idx_spec(rows, lambda i: (i, 0, 0)), _idx_spec(rows, lambda i: (i, 0, 0)),
                  pl.BlockSpec((rows * SLAB, LANES), lambda i: (i, 0))],
        out_specs=pl.BlockSpec(memory_space=pl.ANY),
        out_shape=jax.ShapeDtypeStruct((cap_rows * SLAB, LANES), F32),
        scratch_shapes=[pltpu.SemaphoreType.DMA(())],
        compiler_params=pltpu.CompilerParams(dimension_semantics=("arbitrary",)),
        name="dispatch",
    )(blocked(dest1), blocked(dest2), h_slab)


W_ROWS, W_COLS = 256, 512
W_SLOTS = 8


def _load_expert_weights(e, srcs, dsts, stage, sem):
    jobs = []
    for src, dst in zip(srcs, dsts):
        n_r, n_c = dst.shape
        for r0 in range(0, n_r, W_ROWS):
            for c0 in range(0, n_c, W_COLS):
                jobs.append((src.at[e, pl.ds(r0, W_ROWS), pl.ds(c0, W_COLS)],
                             dst.at[r0:r0 + W_ROWS, c0:c0 + W_COLS]))
    copies = [pltpu.make_async_copy(src, stage.at[i % W_SLOTS], sem.at[i % W_SLOTS])
              for i, (src, _) in enumerate(jobs)]
    ahead = W_SLOTS - 1
    for cp in copies[:ahead]:
        cp.start()
    for i, (cp, (_, dst)) in enumerate(zip(copies, jobs)):
        cp.wait()
        dst[...] = stage[i % W_SLOTS].astype(BF16)
        if i + ahead < len(copies):
            copies[i + ahead].start()


def _experts_kernel(tblk, texp, tn, tnew, xs_ref, wg_hbm, wu_hbm, wd_hbm, o_ref,
                    hb, wg_ref, wu_ref, wd_ref, stage, sem, *, tf):
    t = pl.program_id(0)
    n = tn[t]
    tm = hb.shape[0]

    @pl.when(tnew[t] == 1)
    def _():
        _load_expert_weights(texp[t], (wg_hbm, wu_hbm, wd_hbm), (wg_ref, wu_ref, wd_ref),
                             stage, sem)

    @pl.when(n > 0)
    def _():
        valid = lax.broadcasted_iota(jnp.int32, (tm, LANES), 0) < n
        for c in range(SLAB):
            xc = xs_ref[pl.ds(c, tm, stride=SLAB), :]
            hb[:, c * LANES:(c + 1) * LANES] = jnp.where(valid, xc, 0.0).astype(BF16)
        h_b = hb[...]
        y = _swiglu_part(h_b, wg_ref.at[:, 0:tf], wu_ref.at[:, 0:tf], wd_ref.at[0:tf, :])
        for c0 in range(tf, D_FF, tf):
            y = y + _swiglu_part(h_b, wg_ref.at[:, c0:c0 + tf], wu_ref.at[:, c0:c0 + tf],
                                 wd_ref.at[c0:c0 + tf, :])
        _to_slab(o_ref, y, tm)


def _experts(xs_slab, tblk, texp, tn, tnew, wg, wu, wd, *, tm, tf):
    n_tiles = tblk.shape[0]
    assert D_FF % tf == 0 and D_MODEL % W_ROWS == 0 and D_FF % W_ROWS == 0
    assert D_MODEL % W_COLS == 0 and D_FF % W_COLS == 0
    tile = lambda t, tblk, texp, tn, tnew: (tblk[t], 0)
    hbm = pl.BlockSpec(memory_space=pl.ANY)
    grid_spec = pltpu.PrefetchScalarGridSpec(
        num_scalar_prefetch=4,
        grid=(n_tiles,),
        in_specs=[pl.BlockSpec((tm * SLAB, LANES), tile), hbm, hbm, hbm],
        out_specs=pl.BlockSpec((tm * SLAB, LANES), tile),
        scratch_shapes=[pltpu.VMEM((tm, D_MODEL), BF16),
                        pltpu.VMEM((D_MODEL, D_FF), BF16), pltpu.VMEM((D_MODEL, D_FF), BF16),
                        pltpu.VMEM((D_FF, D_MODEL), BF16),
                        pltpu.VMEM((W_SLOTS, W_ROWS, W_COLS), F32),
                        pltpu.SemaphoreType.DMA((W_SLOTS,))])
    return pl.pallas_call(
        functools.partial(_experts_kernel, tf=tf),
        grid_spec=grid_spec,
        out_shape=jax.ShapeDtypeStruct(xs_slab.shape, F32),
        compiler_params=pltpu.CompilerParams(dimension_semantics=("arbitrary",),
                                             vmem_limit_bytes=VMEM_LIMIT),
        name="experts",
    )(tblk, texp, tn, tnew, xs_slab, wg, wu, wd)


def _combine_kernel(idx1_ref, idx2_ref, idx1_next_ref, idx2_next_ref, x_ref, mf_ref, gf_ref,
                    ys_ref, op_ref, os_ref, buf, ysum, sem, *, rows, n_prompt_tiles):
    i = pl.program_id(0)
    n_i = pl.num_programs(0)
    slot = i % 2

    def issue(refs, s):
        def body(r, c):
            for k, ref in enumerate(refs):
                _row_copy(ys_ref, ref[0, 0, r], buf.at[s], k * rows + r,
                          sem.at[s]).start(priority=k)
            return c
        lax.fori_loop(0, rows, body, 0, unroll=4)

    def finish(s):
        _wait_rows(ys_ref, buf.at[s], 2 * rows, sem.at[s])
        p1 = mf_ref[:, 0:1]
        p2 = mf_ref[:, 1:2]
        for c in range(SLAB):
            cs = slice(c * LANES, (c + 1) * LANES)
            y1 = buf[s, pl.ds(c, rows, stride=SLAB), :]
            y2 = buf[s, pl.ds(rows * SLAB + c, rows, stride=SLAB), :]
            ysum[:, cs] = x_ref[:, cs] + p1 * y1 + p2 * y2
        res = _rms(ysum[...], gf_ref[...])

        @pl.when(i < n_prompt_tiles)
        def _():
            op_ref[...] = res

        @pl.when(i >= n_prompt_tiles)
        def _():
            os_ref[...] = res

    for s in range(2):
        @pl.when(slot == s)
        def _():
            @pl.when(i == 0)
            def _():
                issue((idx1_ref, idx2_ref), s)

            @pl.when(i + 1 < n_i)
            def _():
                issue((idx1_next_ref, idx2_next_ref), 1 - s)

            finish(s)


def _combine(x, mf, g_final, y_slab, dest1, dest2, *, rows, m_prompt):
    m = x.shape[0]
    n_blk = m // rows
    assert m == n_blk * rows and m_prompt % rows == 0
    n_p = m_prompt // rows
    blocked = lambda d: d.reshape(n_blk, 1, rows)
    this = lambda i: (i, 0, 0)
    ahead = lambda i: (jnp.minimum(i + 1, n_blk - 1), 0, 0)
    return pl.pallas_call(
        functools.partial(_combine_kernel, rows=rows, n_prompt_tiles=n_p),
        grid=(n_blk,),
        in_specs=[_idx_spec(rows, this), _idx_spec(rows, this),
                  _idx_spec(rows, ahead), _idx_spec(rows, ahead),
                  pl.BlockSpec((rows, D_MODEL), lambda i: (i, 0)),
                  pl.BlockSpec((rows, LANES), lambda i: (i, 0)),
                  pl.BlockSpec((1, D_MODEL), lambda i: (0, 0)),
                  pl.BlockSpec(memory_space=pl.ANY)],
        out_specs=[pl.BlockSpec((rows, D_MODEL), lambda i: (jnp.minimum(i, n_p - 1), 0)),
                   pl.BlockSpec((rows, D_MODEL), lambda i: (jnp.maximum(i - n_p, 0), 0))],
        out_shape=[jax.ShapeDtypeStruct((m_prompt, D_MODEL), F32),
                   jax.ShapeDtypeStruct((m - m_prompt, D_MODEL), F32)],
        scratch_shapes=[pltpu.VMEM((2, 2 * rows * SLAB, LANES), F32),
                        pltpu.VMEM((rows, D_MODEL), F32),
                        pltpu.SemaphoreType.DMA((2,))],
        compiler_params=pltpu.CompilerParams(dimension_semantics=("arbitrary",),
                                             vmem_limit_bytes=VMEM_LIMIT),
        name="combine",
    )(blocked(dest1), blocked(dest2), blocked(dest1), blocked(dest2), x, mf, g_final, y_slab)


def _expert_tiles(counts, *, tm, cap, n_tiles):
    tiles = (counts + tm - 1) // tm
    ends = jnp.cumsum(tiles)
    t = jnp.arange(n_tiles, dtype=jnp.int32)
    tc = jnp.minimum(t, ends[-1] - 1)
    e = jnp.sum((tc[:, None] >= ends[None, :]).astype(jnp.int32), axis=1)
    k = tc - (ends - tiles)[e]
    tblk = e * (cap // tm) + k
    tn = jnp.where(t < ends[-1], jnp.minimum(counts[e] - k * tm, tm), 0)
    tnew = jnp.logical_and(t < ends[-1], k == 0)
    i32 = lambda v: v.astype(jnp.int32)
    return i32(tblk), i32(e), i32(tn), i32(tnew)


def _moe_final(x, g_ffn, wr, br, wg, wu, wd, g_final, *, m_prompt):
    m = x.shape[0]
    route_rows, tm, tf, out_rows, in_rows = 512, 512, 1792, 1024, 256
    cap = -(-m // tm) * tm
    h_slab, mi, mf, cnt = _route(x, g_ffn, wr, br, tm=route_rows, cap=cap)
    dest1, dest2 = mi[0], mi[1]
    counts = cnt[0, :N_EXPERTS].astype(jnp.int32)
    n_tiles = 2 * m // tm + N_EXPERTS
    tables = _expert_tiles(counts, tm=tm, cap=cap, n_tiles=n_tiles)
    xs_slab = _dispatch(h_slab, dest1, dest2, rows=out_rows, cap_rows=N_EXPERTS * cap)
    y_slab = _experts(xs_slab, *tables, wg, wu, wd, tm=tm, tf=tf)
    return _combine(x, mf, g_final, y_slab, dest1, dest2, rows=in_rows, m_prompt=m_prompt)


def _row(v):
    return v.reshape(1, -1)


def kernel(x_prompt, x_sample, state_conv_a, state_conv_b, norm_mix_g, w_in, conv_a_w, conv_a_b, ln_a_g, ln_a_b, conv_b_w, ln_c_g, ln_c_b, w_spatial, b_spatial, w_proj_a, w_proj_b, w_proj_c, w_out, norm_ffn_g, ffn_w_gate, ffn_w_up, ffn_w_down, w_router, b_router, moe_w_gate, moe_w_up, moe_w_down, norm_final_g):
    depth = w_in.shape[0]
    n_p, t_p, _ = x_prompt.shape
    n_s, t_s, _ = x_sample.shape
    seqs_per_chunk = CHUNK // t_s

    zeros_a = jnp.zeros((1, n_p, A_WIDTH - 1, D_BR), F32)
    zeros_b = jnp.zeros((1, n_p, B_WIDTH - 1, D_BR), F32)
    matrices = dict(w_in=w_in.astype(BF16), w_proj_a=w_proj_a.astype(BF16),
                    w_proj_b=w_proj_b.astype(BF16), w_proj_c=w_proj_c.astype(BF16),
                    w_out=w_out.astype(BF16))

    m_p, m_s = n_p * t_p, n_s * t_s
    m = m_p + m_s
    x_p, row_p = x_prompt.reshape(m_p, D_MODEL), 0
    x_s, row_s = x_sample.reshape(m_s, D_MODEL), 0
    states_p = states_s = None
    for l in range(depth):
        bsp_p = jnp.repeat(b_spatial[l].T, LANES, axis=1)
        eye = jnp.eye(seqs_per_chunk, dtype=F32)
        wsp_s = jnp.einsum("ab,hts->hatbs", eye, w_spatial[l][:, :t_s, :t_s]).reshape(
            C_HEADS, CHUNK, CHUNK)
        bsp_s = jnp.tile(bsp_p[:t_s], (seqs_per_chunk, 1))
        w = dict(matrices, g=_row(norm_mix_g[l]), conv_a_w=conv_a_w[l],
                 conv_a_b=_row(conv_a_b[l]), ln_a_g=_row(ln_a_g[l]), ln_a_b=_row(ln_a_b[l]),
                 conv_b_w=conv_b_w[l], ln_c_g=_row(ln_c_g[l]), ln_c_b=_row(ln_c_b[l]))
        x, *states_p = _mixer(x_p, zeros_a, zeros_b, dict(w, wsp=w_spatial[l], bsp=bsp_p),
                              t=t_p, nb=1, tt=512, in_row=row_p, out_row=0, out_rows=m,
                              layer=l, depth=depth, past_layer=0, state_bufs=states_p)
        x, *states_s = _mixer(x_s, state_conv_a, state_conv_b,
                              dict(w, wsp=wsp_s, bsp=bsp_s), t=t_s, nb=32, tt=t_s,
                              in_row=row_s, out_row=m_p, out_rows=m, layer=l, depth=depth,
                              past_layer=l, out_buf=x, state_bufs=states_s)

        i = l // 2
        g_ffn = _row(norm_ffn_g[l])
        if l % 2 == 0:
            x = _ffn(x, g_ffn, ffn_w_gate[i].astype(BF16), ffn_w_up[i].astype(BF16),
                     ffn_w_down[i].astype(BF16), tm=512, tf=1792)
            x_p, row_p, x_s, row_s = x, 0, x, m_p
        else:
            assert l == depth - 1, "the routed layer applies the final norm"
            wr = jnp.pad(w_router[i], ((0, 0), (0, LANES - N_EXPERTS)))
            br = jnp.pad(_row(b_router[i]), ((0, 0), (0, LANES - N_EXPERTS)))
            y_p, y_s = _moe_final(x, g_ffn, wr, br, moe_w_gate[i], moe_w_up[i], moe_w_down[i],
                                  _row(norm_final_g), m_prompt=m_p)
    y_p = y_p.reshape(n_p, t_p, D_MODEL)
    y_s = y_s.reshape(n_s, t_s, D_MODEL)

    (a_p, b_p, v_p), (a_s, b_s, v_s) = states_p, states_s
    return (y_p, y_s, a_p, a_s, b_p, b_s, v_p, v_s)
```

```python
import functools

import jax
import jax.numpy as jnp
from jax import lax
from jax.experimental import pallas as pl
from jax.experimental.pallas import tpu as pltpu

D_MODEL = 1024
D_BR = 512
A_WIDTH = 31
B_WIDTH = 3
C_HEADS = 4
CHUNK = 128
D_FF = 3584
N_EXPERTS = 8
EPS = 1e-6

A_HIST = 32
B_HIST = 8
LANES = 128
SUBLANES = 8
CONV_ROWS = 32
VMEM_LIMIT = 56 * 1024 * 1024

MIXER_ROWS = 512
SAMPLE_SEQS = 32
MLP_ROWS = 512
MLP_COLS = 1792
ROUTE_ROWS = 512
DISPATCH_ROWS = 1024
COMBINE_ROWS = 256

BF16 = jnp.bfloat16
F32 = jnp.float32


def _rms(x, g):
    return x * lax.rsqrt(jnp.mean(x * x, -1, keepdims=True) + EPS) * g


def _layer_norm(x, g, b):
    mu = jnp.mean(x, -1, keepdims=True)
    xc = x - mu
    var = jnp.mean(xc * xc, -1, keepdims=True)
    return xc * lax.rsqrt(var + EPS) * g + b


def _sigmoid(x):
    return 0.5 * jnp.tanh(0.5 * x) + 0.5


def _dot(a, b):
    return jnp.dot(a, b, preferred_element_type=F32)


def _dot_split(a, b):
    a_hi = a.astype(BF16)
    b_hi = b.astype(BF16)
    a_lo = (a - a_hi.astype(F32)).astype(BF16)
    b_lo = (b - b_hi.astype(F32)).astype(BF16)
    return _dot(a_hi, b_hi) + (_dot(a_hi, b_lo) + _dot(a_lo, b_hi))


def _mixer_kernel(x_ref, pa_ref, pb_ref, g_ref, win_ref, caw_ref, cab_ref, lag_ref, lab_ref,
                  cbw_ref, lcg_ref, lcb_ref, wsp_ref, bsp_ref, wpa_ref, wpb_ref, wpc_ref,
                  wout_ref, xo_ref, na_ref, nb_ref, nv_ref, fa, fz, sc, sh, ca, wb, *, nb, tt, nv_rows):
    i = pl.program_id(1)
    rows = nb * tt

    @pl.when(i == 0)
    def _():
        fa[:, A_HIST - (A_WIDTH - 1):A_HIST, :] = pa_ref[...]
        fz[:, B_HIST - (B_WIDTH - 1):B_HIST, :] = pb_ref[...]

    x = x_ref[...]
    hb = _rms(x, g_ref[...]).astype(BF16)

    def proj(c0, c1):
        return _dot(hb, win_ref[:, c0:c1])

    a = proj(0, D_BR) * _sigmoid(proj(D_BR, 2 * D_BR))
    fa[:, A_HIST:A_HIST + tt, :] = a.reshape(nb, tt, D_BR)
    off_a = A_HIST - (A_WIDTH - 1)
    for j in range(SUBLANES):
        span = tt + SUBLANES * ((A_WIDTH - 1 - j) // SUBLANES)
        sh[j, :, 0:span, :] = fa[:, off_a + j:off_a + j + span, :]
    for k in range(A_WIDTH):
        wb[k] = jnp.broadcast_to(caw_ref[k:k + 1, :], (SUBLANES, D_BR))
    rb = min(tt, CONV_ROWS)
    sb = CONV_ROWS // rb
    n_rb = tt // rb

    def conv_step(u, carry):
        s0 = (u // n_rb) * sb
        r0 = (u % n_rb) * rb
        groups = range(0, rb, SUBLANES)
        acc = [jnp.broadcast_to(cab_ref[...], (sb, SUBLANES, D_BR)) for _ in groups]
        for k in range(A_WIDTH):
            q, j = divmod(k, SUBLANES)
            w_k = wb[k]
            for gi, g in enumerate(groups):
                lo = r0 + g + SUBLANES * q
                acc[gi] = acc[gi] + w_k * sh[j, s0:s0 + sb, lo:lo + SUBLANES, :]
        for gi, g in enumerate(groups):
            ca[s0:s0 + sb, r0 + g:r0 + g + SUBLANES, :] = acc[gi]
        return carry

    for u in range((nb // sb) * n_rb):
        conv_step(u, 0)
    a_ln = _layer_norm(ca[...].reshape(rows, D_BR), lag_ref[...], lab_ref[...])
    a_out = _dot((a_ln * _sigmoid(a_ln)).astype(BF16), wpa_ref[...])
    na_ref[...] = fa[:, tt + A_HIST - (A_WIDTH - 1):tt + A_HIST, :]
    fa[:, 0:A_HIST, :] = fa[:, tt:tt + A_HIST, :]

    c0 = 2 * D_BR
    z = proj(c0 + 2 * D_BR, c0 + 3 * D_BR) * proj(c0, c0 + D_BR)
    fz[:, B_HIST:B_HIST + tt, :] = z.reshape(nb, tt, D_BR)
    off_b = B_HIST - (B_WIDTH - 1)
    accb = jnp.zeros((nb, tt, D_BR), F32)
    for k in range(B_WIDTH):
        accb = accb + cbw_ref[k:k + 1, :] * fz[:, off_b + k:off_b + k + tt, :]
    b_out = _dot((proj(c0 + D_BR, c0 + 2 * D_BR) * accb.reshape(rows, D_BR)).astype(BF16),
                 wpb_ref[...])
    nb_ref[...] = fz[:, tt + B_HIST - (B_WIDTH - 1):tt + B_HIST, :]
    fz[:, 0:B_HIST, :] = fz[:, tt:tt + B_HIST, :]

    c0 = 5 * D_BR
    v = _layer_norm(proj(c0 + D_BR, c0 + 2 * D_BR), lcg_ref[...], lcb_ref[...])
    nv_ref[...] = v[rows - nv_rows * nb:, :].reshape(nb, nv_rows, D_BR)
    vb = v.astype(BF16)
    t_idx = lax.broadcasted_iota(jnp.int32, (CHUNK, CHUNK), 0)
    s_idx = lax.broadcasted_iota(jnp.int32, (CHUNK, CHUNK), 1)
    causal = s_idx <= t_idx
    for hd in range(C_HEADS):
        wm = jnp.where(causal, wsp_ref[hd], 0.0).astype(BF16)
        for c in range(rows // CHUNK):
            blk = (slice(c * CHUNK, (c + 1) * CHUNK), slice(hd * LANES, (hd + 1) * LANES))
            sc[blk] = _dot(wm, vb[blk])
    c_out = _dot((proj(c0, c0 + D_BR) * (sc[...].reshape(rows // CHUNK, CHUNK, D_BR)
                                        + bsp_ref[...]).reshape(rows, D_BR)).astype(BF16),
                 wpc_ref[...])

    c0 = 7 * D_BR
    merged = (_sigmoid(proj(c0, c0 + D_MODEL)) * a_out
              + _sigmoid(proj(c0 + D_MODEL, c0 + 2 * D_MODEL)) * b_out
              + _sigmoid(proj(c0 + 2 * D_MODEL, c0 + 3 * D_MODEL)) * c_out)
    xo_ref[...] = x + _dot(merged.astype(BF16), wout_ref[...])


def _const_spec(shape):
    nd = len(shape)
    return pl.BlockSpec(shape, lambda b, i: (0,) * nd, pipeline_mode=pl.Buffered(1))


def _drop_refs(kernel_fn, pos, count):
    def body(*refs):
        return kernel_fn(*refs[:pos], *refs[pos + count:])
    return body


def _mixer(x, past_a, past_b, w, *, t, nb, tt, in_row, out_row, out_rows, layer, depth,
           past_layer, out_buf=None, state_bufs=None):
    n = past_a.shape[1]
    rows = nb * tt
    assert n % nb == 0 and t % tt == 0 and rows % CHUNK == 0
    assert in_row % rows == 0 and out_row % rows == 0
    assert tt % SUBLANES == 0 and (tt % CONV_ROWS == 0 or
                                   (CONV_ROWS % tt == 0 and nb % (CONV_ROWS // tt) == 0))
    nv_rows = t - ((t - 1) // CHUNK) * CHUNK
    assert nv_rows <= tt
    n_t = t // tt
    past_spec = lambda r: pl.BlockSpec((None, nb, r, D_BR), lambda b, i: (past_layer, b, 0, 0))
    row_spec = lambda first: pl.BlockSpec((rows, D_MODEL),
                                          lambda b, i: (first // rows + b * n_t + i, 0))
    const_spec = lambda c, stacked: (
        pl.BlockSpec((None,) + c.shape[1:], lambda b, i: (layer, 0, 0),
                     pipeline_mode=pl.Buffered(1)) if stacked else _const_spec(c.shape))
    stacked = ("w_in", "w_proj_a", "w_proj_b", "w_proj_c", "w_out")
    names = ("g", "w_in", "conv_a_w", "conv_a_b", "ln_a_g", "ln_a_b", "conv_b_w", "ln_c_g",
             "ln_c_b", "wsp", "bsp", "w_proj_a", "w_proj_b", "w_proj_c", "w_out")
    body = functools.partial(_mixer_kernel, nb=nb, tt=tt, nv_rows=nv_rows)
    operands = [x, past_a, past_b] + [w[k] for k in names]
    in_specs = ([row_spec(in_row), past_spec(A_WIDTH - 1), past_spec(B_WIDTH - 1)]
                + [const_spec(w[k], k in stacked) for k in names])
    state_rows = (A_WIDTH - 1, B_WIDTH - 1, nv_rows)
    out_shape = [jax.ShapeDtypeStruct((out_rows, D_MODEL), F32)] + [
        jax.ShapeDtypeStruct((depth, n, r, D_BR), F32) for r in state_rows]
    n_in = len(operands)
    aliases = {}
    for k, carrier in enumerate((out_buf, *(state_bufs or (None,) * 3))):
        if carrier is not None:
            assert carrier.shape == out_shape[k].shape
            aliases[len(operands)] = k
            operands.append(carrier)
            in_specs.append(pl.BlockSpec(memory_space=pl.ANY))
    body = _drop_refs(body, n_in, len(operands) - n_in)
    layer_spec = lambda r: pl.BlockSpec((None, nb, r, D_BR), lambda b, i: (layer, b, 0, 0))
    return pl.pallas_call(
        body,
        grid=(n // nb, n_t),
        in_specs=in_specs,
        out_specs=[row_spec(out_row)] + [layer_spec(r) for r in state_rows],
        input_output_aliases=aliases,
        out_shape=out_shape,
        scratch_shapes=[pltpu.VMEM((nb, A_HIST + tt, D_BR), F32),
                        pltpu.VMEM((nb, B_HIST + tt, D_BR), F32),
                        pltpu.VMEM((nb * tt, D_BR), F32),
                        pltpu.VMEM((SUBLANES, nb, tt + A_HIST - SUBLANES, D_BR), F32),
                        pltpu.VMEM((nb, tt, D_BR), F32),
                        pltpu.VMEM((A_WIDTH, SUBLANES, D_BR), F32)],
        compiler_params=pltpu.CompilerParams(
            dimension_semantics=("arbitrary", "arbitrary"), vmem_limit_bytes=VMEM_LIMIT),
        name="mixer",
    )(*operands)


def _swiglu_part(h_b, wg_ref, wu_ref, wd_ref):
    gate = _dot(h_b, wg_ref[...])
    act = gate * _sigmoid(gate) * _dot(h_b, wu_ref[...])
    return _dot(act.astype(BF16), wd_ref[...])


def _ffn_kernel(x_ref, g_ref, wg_hbm, wu_hbm, wd_hbm, o_ref, wg_ref, wu_ref, wd_ref, stage, sem,
                *, tf, which):
    @pl.when(pl.program_id(0) == 0)
    def _():
        _load_expert_weights(which, (wg_hbm, wu_hbm, wd_hbm), (wg_ref, wu_ref, wd_ref), stage, sem)

    x = x_ref[...]
    h_b = _rms(x, g_ref[...]).astype(BF16)
    y = x
    for c0 in range(0, D_FF, tf):
        y = y + _swiglu_part(h_b, wg_ref.at[:, c0:c0 + tf], wu_ref.at[:, c0:c0 + tf],
                             wd_ref.at[c0:c0 + tf, :])
    o_ref[...] = y


def _ffn(x, g, wg, wu, wd, *, which, tm, tf):
    m, _ = x.shape
    assert m % tm == 0 and D_FF % tf == 0
    hbm = pl.BlockSpec(memory_space=pl.ANY)
    return pl.pallas_call(
        functools.partial(_ffn_kernel, tf=tf, which=which),
        grid=(m // tm,),
        in_specs=[pl.BlockSpec((tm, D_MODEL), lambda i: (i, 0)),
                  pl.BlockSpec((1, D_MODEL), lambda i: (0, 0)), hbm, hbm, hbm],
        out_specs=pl.BlockSpec((tm, D_MODEL), lambda i: (i, 0)),
        out_shape=jax.ShapeDtypeStruct((m, D_MODEL), F32),
        scratch_shapes=_weight_scratch(),
        compiler_params=pltpu.CompilerParams(
            dimension_semantics=("arbitrary",), vmem_limit_bytes=VMEM_LIMIT),
        name="ffn",
    )(x, g, wg, wu, wd)


SLAB = D_MODEL // LANES


def _to_slab(ref, val, rows):
    for c in range(SLAB):
        ref[pl.ds(c, rows, stride=SLAB), :] = val[:, c * LANES:(c + 1) * LANES]


def _route_kernel(x_ref, g_ref, wr_ref, br_ref, hs_ref, mi_ref, mf_ref, cnt_ref, carry, *, cap):
    i = pl.program_id(0)
    tm = x_ref.shape[0]

    @pl.when(i == 0)
    def _():
        carry[...] = jnp.zeros_like(carry)

    h = _rms(x_ref[...], g_ref[...])
    _to_slab(hs_ref, h, tm)

    col = lax.broadcasted_iota(jnp.int32, (tm, LANES), 1)
    logits = _dot_split(h, wr_ref[...]) + br_ref[...]
    logits = jnp.where(col < N_EXPERTS, logits, -jnp.inf)
    m1 = jnp.max(logits, -1, keepdims=True)
    i1 = jnp.min(jnp.where(logits == m1, col, LANES), -1, keepdims=True)
    rest = jnp.where(col == i1, -jnp.inf, logits)
    m2 = jnp.max(rest, -1, keepdims=True)
    i2 = jnp.min(jnp.where(rest == m2, col, LANES), -1, keepdims=True)
    e2 = jnp.exp(m2 - m1)
    p1 = 1.0 / (1.0 + e2)
    p2 = e2 * p1

    chosen = jnp.logical_or(col == i1, col == i2).astype(F32)
    r_idx = lax.broadcasted_iota(jnp.int32, (tm, tm), 0)
    c_idx = lax.broadcasted_iota(jnp.int32, (tm, tm), 1)
    before = (c_idx < r_idx).astype(BF16)
    rank = carry[...] + _dot(before, chosen.astype(BF16))
    dest = col.astype(F32) * float(cap) + rank
    d1 = jnp.sum(jnp.where(col == i1, dest, 0.0), -1, keepdims=True)
    d2 = jnp.sum(jnp.where(col == i2, dest, 0.0), -1, keepdims=True)
    by_choice = jnp.where(col == 0, d1, jnp.where(col == 1, d2, 0.0)).T
    mi_ref[...] = by_choice[0:SUBLANES, :].astype(jnp.int32)
    mf_ref[...] = jnp.where(col == 0, p1, jnp.where(col == 1, p2, 0.0))
    carry[...] += jnp.sum(chosen, 0, keepdims=True)
    cnt_ref[...] = carry[...]


def _route(x, g, wr, br, *, tm, cap):
    m = x.shape[0]
    assert m % tm == 0
    row = lambda i: (i, 0)
    const = lambda i: (0, 0)
    return pl.pallas_call(
        functools.partial(_route_kernel, cap=cap),
        grid=(m // tm,),
        in_specs=[pl.BlockSpec((tm, D_MODEL), row), pl.BlockSpec((1, D_MODEL), const),
                  pl.BlockSpec((D_MODEL, LANES), const), pl.BlockSpec((1, LANES), const)],
        out_specs=[pl.BlockSpec((tm * SLAB, LANES), row),
                   pl.BlockSpec((SUBLANES, tm), lambda i: (0, i)),
                   pl.BlockSpec((tm, LANES), row), pl.BlockSpec((1, LANES), const)],
        out_shape=[jax.ShapeDtypeStruct((m * SLAB, LANES), F32),
                   jax.ShapeDtypeStruct((SUBLANES, m), jnp.int32),
                   jax.ShapeDtypeStruct((m, LANES), F32),
                   jax.ShapeDtypeStruct((1, LANES), F32)],
        scratch_shapes=[pltpu.VMEM((1, LANES), F32)],
        compiler_params=pltpu.CompilerParams(dimension_semantics=("arbitrary",),
                                             vmem_limit_bytes=VMEM_LIMIT),
        name="route",
    )(x, g, wr, br)


def _row_copy(src_ref, src_row, dst_ref, dst_row, sem):
    return pltpu.make_async_copy(
        src_ref.at[pl.ds(pl.multiple_of(src_row * SLAB, SLAB), SLAB), :],
        dst_ref.at[pl.ds(pl.multiple_of(dst_row * SLAB, SLAB), SLAB), :], sem)


def _wait_rows(src_ref, dst_ref, n_rows, sem):
    pltpu.make_async_copy(src_ref.at[pl.ds(0, n_rows * SLAB), :],
                          dst_ref.at[pl.ds(0, n_rows * SLAB), :], sem).wait()


def _dispatch_kernel(idx1_ref, idx2_ref, src_ref, dst_ref, sem, *, rows):
    def issue(r, c):
        for k, idx_ref in enumerate((idx1_ref, idx2_ref)):
            _row_copy(src_ref, r, dst_ref, idx_ref[0, 0, r], sem).start(priority=k)
        return c

    lax.fori_loop(0, rows, issue, 0)
    for k in range(2):
        _wait_rows(src_ref, dst_ref, rows, sem)


def _idx_spec(rows, index_map):
    return pl.BlockSpec((1, 1, rows), index_map, memory_space=pltpu.SMEM)


def _dispatch(h_slab, dest1, dest2, *, rows, cap_rows):
    n_blk = dest1.shape[0] // rows
    blocked = lambda d: d.reshape(n_blk, 1, rows)
    return pl.pallas_call(
        functools.partial(_dispatch_kernel, rows=rows),
        grid=(n_blk,),
        in_specs=[_idx_spec(rows, lambda i: (i, 0, 0)), _idx_spec(rows, lambda i: (i, 0, 0)),
                  pl.BlockSpec((rows * SLAB, LANES), lambda i: (i, 0))],
        out_specs=pl.BlockSpec(memory_space=pl.ANY),
        out_shape=jax.ShapeDtypeStruct((cap_rows * SLAB, LANES), F32),
        scratch_shapes=[pltpu.SemaphoreType.DMA(())],
        compiler_params=pltpu.CompilerParams(dimension_semantics=("arbitrary",)),
        name="dispatch",
    )(blocked(dest1), blocked(dest2), h_slab)


W_ROWS, W_COLS = 256, 512
W_SLOTS = 8


def _load_expert_weights(e, srcs, dsts, stage, sem):
    jobs = []
    for src, dst in zip(srcs, dsts):
        n_r, n_c = dst.shape
        for r0 in range(0, n_r, W_ROWS):
            for c0 in range(0, n_c, W_COLS):
                jobs.append((src.at[e, pl.ds(r0, W_ROWS), pl.ds(c0, W_COLS)],
                             dst.at[r0:r0 + W_ROWS, c0:c0 + W_COLS]))
    copies = [pltpu.make_async_copy(src, stage.at[i % W_SLOTS], sem.at[i % W_SLOTS])
              for i, (src, _) in enumerate(jobs)]
    ahead = W_SLOTS - 1
    for cp in copies[:ahead]:
        cp.start()
    for i, (cp, (_, dst)) in enumerate(zip(copies, jobs)):
        cp.wait()
        dst[...] = stage[i % W_SLOTS].astype(BF16)
        if i + ahead < len(copies):
            copies[i + ahead].start()


def _weight_scratch():
    assert D_MODEL % W_ROWS == 0 and D_FF % W_ROWS == 0
    assert D_MODEL % W_COLS == 0 and D_FF % W_COLS == 0
    return [pltpu.VMEM((D_MODEL, D_FF), BF16), pltpu.VMEM((D_MODEL, D_FF), BF16),
            pltpu.VMEM((D_FF, D_MODEL), BF16), pltpu.VMEM((W_SLOTS, W_ROWS, W_COLS), F32),
            pltpu.SemaphoreType.DMA((W_SLOTS,))]


def _experts_kernel(tblk, texp, tn, tnew, xs_ref, wg_hbm, wu_hbm, wd_hbm, o_ref,
                    hb, wg_ref, wu_ref, wd_ref, stage, sem, *, tf):
    t = pl.program_id(0)
    n = tn[t]
    tm = hb.shape[0]

    @pl.when(tnew[t] == 1)
    def _():
        _load_expert_weights(texp[t], (wg_hbm, wu_hbm, wd_hbm), (wg_ref, wu_ref, wd_ref),
                             stage, sem)

    def run(rows):
        valid = lax.broadcasted_iota(jnp.int32, (rows, LANES), 0) < n
        for c in range(SLAB):
            xc = xs_ref[pl.ds(c, rows, stride=SLAB), :]
            hb[0:rows, c * LANES:(c + 1) * LANES] = jnp.where(valid, xc, 0.0).astype(BF16)
        h_b = hb[0:rows, :]
        y = _swiglu_part(h_b, wg_ref.at[:, 0:tf], wu_ref.at[:, 0:tf], wd_ref.at[0:tf, :])
        for c0 in range(tf, D_FF, tf):
            y = y + _swiglu_part(h_b, wg_ref.at[:, c0:c0 + tf], wu_ref.at[:, c0:c0 + tf],
                                 wd_ref.at[c0:c0 + tf, :])
        _to_slab(o_ref, y, rows)

    pl.when(n > tm // 2)(lambda: run(tm))
    pl.when(jnp.logical_and(n > 0, n <= tm // 2))(lambda: run(tm // 2))


def _experts(xs_slab, tblk, texp, tn, tnew, wg, wu, wd, *, tm, tf):
    n_tiles = tblk.shape[0]
    assert D_FF % tf == 0 and tm % (2 * SUBLANES) == 0
    tile = lambda t, tblk, texp, tn, tnew: (tblk[t], 0)
    hbm = pl.BlockSpec(memory_space=pl.ANY)
    grid_spec = pltpu.PrefetchScalarGridSpec(
        num_scalar_prefetch=4,
        grid=(n_tiles,),
        in_specs=[pl.BlockSpec((tm * SLAB, LANES), tile), hbm, hbm, hbm],
        out_specs=pl.BlockSpec((tm * SLAB, LANES), tile),
        scratch_shapes=[pltpu.VMEM((tm, D_MODEL), BF16)] + _weight_scratch())
    return pl.pallas_call(
        functools.partial(_experts_kernel, tf=tf),
        grid_spec=grid_spec,
        out_shape=jax.ShapeDtypeStruct(xs_slab.shape, F32),
        compiler_params=pltpu.CompilerParams(dimension_semantics=("arbitrary",),
                                             vmem_limit_bytes=VMEM_LIMIT),
        name="experts",
    )(tblk, texp, tn, tnew, xs_slab, wg, wu, wd)


def _combine_kernel(idx1_ref, idx2_ref, idx1_next_ref, idx2_next_ref, x_ref, mf_ref, gf_ref,
                    ys_ref, op_ref, os_ref, buf, ysum, sem, *, rows, n_prompt_tiles):
    i = pl.program_id(0)
    n_i = pl.num_programs(0)
    slot = i % 2

    def issue(refs, s):
        def body(r, c):
            for k, ref in enumerate(refs):
                _row_copy(ys_ref, ref[0, 0, r], buf.at[s], k * rows + r,
                          sem.at[s]).start(priority=k)
            return c
        lax.fori_loop(0, rows, body, 0, unroll=4)

    def finish(s):
        _wait_rows(ys_ref, buf.at[s], 2 * rows, sem.at[s])
        p1 = mf_ref[:, 0:1]
        p2 = mf_ref[:, 1:2]
        for c in range(SLAB):
            cs = slice(c * LANES, (c + 1) * LANES)
            y1 = buf[s, pl.ds(c, rows, stride=SLAB), :]
            y2 = buf[s, pl.ds(rows * SLAB + c, rows, stride=SLAB), :]
            ysum[:, cs] = x_ref[:, cs] + p1 * y1 + p2 * y2
        res = _rms(ysum[...], gf_ref[...])

        @pl.when(i < n_prompt_tiles)
        def _():
            op_ref[...] = res

        @pl.when(i >= n_prompt_tiles)
        def _():
            os_ref[...] = res

    for s in range(2):
        @pl.when(slot == s)
        def _():
            @pl.when(i == 0)
            def _():
                issue((idx1_ref, idx2_ref), s)

            @pl.when(i + 1 < n_i)
            def _():
                issue((idx1_next_ref, idx2_next_ref), 1 - s)

            finish(s)


def _combine(x, mf, g_final, y_slab, dest1, dest2, *, rows, m_prompt):
    m = x.shape[0]
    n_blk = m // rows
    assert m == n_blk * rows and m_prompt % rows == 0
    n_p = m_prompt // rows
    blocked = lambda d: d.reshape(n_blk, 1, rows)
    this = lambda i: (i, 0, 0)
    ahead = lambda i: (jnp.minimum(i + 1, n_blk - 1), 0, 0)
    return pl.pallas_call(
        functools.partial(_combine_kernel, rows=rows, n_prompt_tiles=n_p),
        grid=(n_blk,),
        in_specs=[_idx_spec(rows, this), _idx_spec(rows, this),
                  _idx_spec(rows, ahead), _idx_spec(rows, ahead),
                  pl.BlockSpec((rows, D_MODEL), lambda i: (i, 0)),
                  pl.BlockSpec((rows, LANES), lambda i: (i, 0)),
                  pl.BlockSpec((1, D_MODEL), lambda i: (0, 0)),
                  pl.BlockSpec(memory_space=pl.ANY)],
        out_specs=[pl.BlockSpec((rows, D_MODEL), lambda i: (jnp.minimum(i, n_p - 1), 0)),
                   pl.BlockSpec((rows, D_MODEL), lambda i: (jnp.maximum(i - n_p, 0), 0))],
        out_shape=[jax.ShapeDtypeStruct((m_prompt, D_MODEL), F32),
                   jax.ShapeDtypeStruct((m - m_prompt, D_MODEL), F32)],
        scratch_shapes=[pltpu.VMEM((2, 2 * rows * SLAB, LANES), F32),
                        pltpu.VMEM((rows, D_MODEL), F32),
                        pltpu.SemaphoreType.DMA((2,))],
        compiler_params=pltpu.CompilerParams(dimension_semantics=("arbitrary",),
                                             vmem_limit_bytes=VMEM_LIMIT),
        name="combine",
    )(blocked(dest1), blocked(dest2), blocked(dest1), blocked(dest2), x, mf, g_final, y_slab)


def _expert_tiles(counts, *, tm, cap, n_tiles):
    tiles = (counts + tm - 1) // tm
    ends = jnp.cumsum(tiles)
    t = jnp.arange(n_tiles, dtype=jnp.int32)
    tc = jnp.minimum(t, ends[-1] - 1)
    e = jnp.sum((tc[:, None] >= ends[None, :]).astype(jnp.int32), axis=1)
    k = tc - (ends - tiles)[e]
    tblk = e * (cap // tm) + k
    tn = jnp.where(t < ends[-1], jnp.minimum(counts[e] - k * tm, tm), 0)
    tnew = jnp.logical_and(t < ends[-1], k == 0)
    i32 = lambda v: v.astype(jnp.int32)
    return i32(tblk), i32(e), i32(tn), i32(tnew)


def _moe_final(x, g_ffn, wr, br, wg, wu, wd, g_final, *, m_prompt):
    m = x.shape[0]
    tm, tf, out_rows, in_rows = MLP_ROWS, MLP_COLS, DISPATCH_ROWS, COMBINE_ROWS
    cap = -(-m // tm) * tm
    h_slab, mi, mf, cnt = _route(x, g_ffn, wr, br, tm=ROUTE_ROWS, cap=cap)
    dest1, dest2 = mi[0], mi[1]
    counts = cnt[0, :N_EXPERTS].astype(jnp.int32)
    n_tiles = 2 * m // tm + N_EXPERTS
    tables = _expert_tiles(counts, tm=tm, cap=cap, n_tiles=n_tiles)
    xs_slab = _dispatch(h_slab, dest1, dest2, rows=out_rows, cap_rows=N_EXPERTS * cap)
    y_slab = _experts(xs_slab, *tables, wg, wu, wd, tm=tm, tf=tf)
    return _combine(x, mf, g_final, y_slab, dest1, dest2, rows=in_rows, m_prompt=m_prompt)


def _row(v):
    return v.reshape(1, -1)


def kernel(x_prompt, x_sample, state_conv_a, state_conv_b, norm_mix_g, w_in, conv_a_w, conv_a_b, ln_a_g, ln_a_b, conv_b_w, ln_c_g, ln_c_b, w_spatial, b_spatial, w_proj_a, w_proj_b, w_proj_c, w_out, norm_ffn_g, ffn_w_gate, ffn_w_up, ffn_w_down, w_router, b_router, moe_w_gate, moe_w_up, moe_w_down, norm_final_g):
    depth = w_in.shape[0]
    n_p, t_p, _ = x_prompt.shape
    n_s, t_s, _ = x_sample.shape
    seqs_per_chunk = CHUNK // t_s

    zeros_a = jnp.zeros((1, n_p, A_WIDTH - 1, D_BR), F32)
    zeros_b = jnp.zeros((1, n_p, B_WIDTH - 1, D_BR), F32)
    matrices = dict(w_in=w_in.astype(BF16), w_proj_a=w_proj_a.astype(BF16),
                    w_proj_b=w_proj_b.astype(BF16), w_proj_c=w_proj_c.astype(BF16),
                    w_out=w_out.astype(BF16))

    m_p, m_s = n_p * t_p, n_s * t_s
    m = m_p + m_s
    x_p, row_p = x_prompt.reshape(m_p, D_MODEL), 0
    x_s, row_s = x_sample.reshape(m_s, D_MODEL), 0
    states_p = states_s = None
    for l in range(depth):
        bsp_p = jnp.repeat(b_spatial[l].T, LANES, axis=1)
        eye = jnp.eye(seqs_per_chunk, dtype=F32)
        wsp_s = jnp.einsum("ab,hts->hatbs", eye, w_spatial[l][:, :t_s, :t_s]).reshape(
            C_HEADS, CHUNK, CHUNK)
        bsp_s = jnp.tile(bsp_p[:t_s], (seqs_per_chunk, 1))
        w = dict(matrices, g=_row(norm_mix_g[l]), conv_a_w=conv_a_w[l],
                 conv_a_b=_row(conv_a_b[l]), ln_a_g=_row(ln_a_g[l]), ln_a_b=_row(ln_a_b[l]),
                 conv_b_w=conv_b_w[l], ln_c_g=_row(ln_c_g[l]), ln_c_b=_row(ln_c_b[l]))
        x, *states_p = _mixer(x_p, zeros_a, zeros_b, dict(w, wsp=w_spatial[l], bsp=bsp_p),
                              t=t_p, nb=1, tt=MIXER_ROWS, in_row=row_p, out_row=0, out_rows=m,
                              layer=l, depth=depth, past_layer=0, state_bufs=states_p)
        x, *states_s = _mixer(x_s, state_conv_a, state_conv_b,
                              dict(w, wsp=wsp_s, bsp=bsp_s), t=t_s, nb=SAMPLE_SEQS, tt=t_s,
                              in_row=row_s, out_row=m_p, out_rows=m, layer=l, depth=depth,
                              past_layer=l, out_buf=x, state_bufs=states_s)

        i = l // 2
        g_ffn = _row(norm_ffn_g[l])
        if l % 2 == 0:
            x = _ffn(x, g_ffn, ffn_w_gate, ffn_w_up, ffn_w_down, which=i,
                     tm=MLP_ROWS, tf=MLP_COLS)
            x_p, row_p, x_s, row_s = x, 0, x, m_p
        else:
            assert l == depth - 1, "the routed layer applies the final norm"
            wr = jnp.pad(w_router[i], ((0, 0), (0, LANES - N_EXPERTS)))
            br = jnp.pad(_row(b_router[i]), ((0, 0), (0, LANES - N_EXPERTS)))
            y_p, y_s = _moe_final(x, g_ffn, wr, br, moe_w_gate[i], moe_w_up[i], moe_w_down[i],
                                  _row(norm_final_g), m_prompt=m_p)
    y_p = y_p.reshape(n_p, t_p, D_MODEL)
    y_s = y_s.reshape(n_s, t_s, D_MODEL)

    (a_p, b_p, v_p), (a_s, b_s, v_s) = states_p, states_s
    return (y_p, y_s, a_p, a_s, b_p, b_s, v_p, v_s)
```

```python
import functools

import jax
import jax.numpy as jnp
from jax import lax
from jax.experimental import pallas as pl
from jax.experimental.pallas import tpu as pltpu

D_MODEL = 1024
D_BR = 512
A_WIDTH = 31
B_WIDTH = 3
C_HEADS = 4
CHUNK = 128
D_FF = 3584
N_EXPERTS = 8
EPS = 1e-6

A_HIST = 32
B_HIST = 8
LANES = 128
SUBLANES = 8
CONV_ROWS = 32
VMEM_LIMIT = 56 * 1024 * 1024

MIXER_ROWS = 512
SAMPLE_SEQS = 32
MLP_ROWS = 512
MLP_COLS = 1792
ROUTE_ROWS = 512
DISPATCH_ROWS = 1024
COMBINE_ROWS = 512

BF16 = jnp.bfloat16
F32 = jnp.float32


def _rms(x, g):
    return x * lax.rsqrt(jnp.mean(x * x, -1, keepdims=True) + EPS) * g


def _layer_norm(x, g, b):
    mu = jnp.mean(x, -1, keepdims=True)
    xc = x - mu
    var = jnp.mean(xc * xc, -1, keepdims=True)
    return xc * lax.rsqrt(var + EPS) * g + b


def _sigmoid(x):
    return 0.5 * jnp.tanh(0.5 * x) + 0.5


def _dot(a, b):
    return jnp.dot(a, b, preferred_element_type=F32)


def _dot_split(a, b):
    a_hi = a.astype(BF16)
    b_hi = b.astype(BF16)
    a_lo = (a - a_hi.astype(F32)).astype(BF16)
    b_lo = (b - b_hi.astype(F32)).astype(BF16)
    return _dot(a_hi, b_hi) + (_dot(a_hi, b_lo) + _dot(a_lo, b_hi))


def _mixer_kernel(x_ref, pa_ref, pb_ref, g_ref, win_ref, caw_ref, cab_ref, lag_ref, lab_ref,
                  cbw_ref, lcg_ref, lcb_ref, wsp_ref, bsp_ref, wpa_ref, wpb_ref, wpc_ref,
                  wout_ref, xo_ref, na_ref, nb_ref, nv_ref, fa, fz, sc, sh, ca, wb, *, nb, tt, nv_rows):
    i = pl.program_id(1)
    rows = nb * tt

    @pl.when(i == 0)
    def _():
        fa[:, A_HIST - (A_WIDTH - 1):A_HIST, :] = pa_ref[...]
        fz[:, B_HIST - (B_WIDTH - 1):B_HIST, :] = pb_ref[...]

    x = x_ref[...]
    hb = _rms(x, g_ref[...]).astype(BF16)

    def proj(c0, c1):
        return _dot(hb, win_ref[:, c0:c1])

    a = proj(0, D_BR) * _sigmoid(proj(D_BR, 2 * D_BR))
    fa[:, A_HIST:A_HIST + tt, :] = a.reshape(nb, tt, D_BR)
    off_a = A_HIST - (A_WIDTH - 1)
    for j in range(SUBLANES):
        span = tt + SUBLANES * ((A_WIDTH - 1 - j) // SUBLANES)
        sh[j, :, 0:span, :] = fa[:, off_a + j:off_a + j + span, :]
    for k in range(A_WIDTH):
        wb[k] = jnp.broadcast_to(caw_ref[k:k + 1, :], (SUBLANES, D_BR))
    rb = min(tt, CONV_ROWS)
    sb = CONV_ROWS // rb
    n_rb = tt // rb

    def conv_step(u, carry):
        s0 = (u // n_rb) * sb
        r0 = (u % n_rb) * rb
        groups = range(0, rb, SUBLANES)
        acc = [jnp.broadcast_to(cab_ref[...], (sb, SUBLANES, D_BR)) for _ in groups]
        for k in range(A_WIDTH):
            q, j = divmod(k, SUBLANES)
            w_k = wb[k]
            for gi, g in enumerate(groups):
                lo = r0 + g + SUBLANES * q
                acc[gi] = acc[gi] + w_k * sh[j, s0:s0 + sb, lo:lo + SUBLANES, :]
        for gi, g in enumerate(groups):
            ca[s0:s0 + sb, r0 + g:r0 + g + SUBLANES, :] = acc[gi]
        return carry

    for u in range((nb // sb) * n_rb):
        conv_step(u, 0)
    a_ln = _layer_norm(ca[...].reshape(rows, D_BR), lag_ref[...], lab_ref[...])
    a_out = _dot((a_ln * _sigmoid(a_ln)).astype(BF16), wpa_ref[...])
    na_ref[...] = fa[:, tt + A_HIST - (A_WIDTH - 1):tt + A_HIST, :]
    fa[:, 0:A_HIST, :] = fa[:, tt:tt + A_HIST, :]

    c0 = 2 * D_BR
    z = proj(c0 + 2 * D_BR, c0 + 3 * D_BR) * proj(c0, c0 + D_BR)
    fz[:, B_HIST:B_HIST + tt, :] = z.reshape(nb, tt, D_BR)
    off_b = B_HIST - (B_WIDTH - 1)
    accb = jnp.zeros((nb, tt, D_BR), F32)
    for k in range(B_WIDTH):
        accb = accb + cbw_ref[k:k + 1, :] * fz[:, off_b + k:off_b + k + tt, :]
    b_out = _dot((proj(c0 + D_BR, c0 + 2 * D_BR) * accb.reshape(rows, D_BR)).astype(BF16),
                 wpb_ref[...])
    nb_ref[...] = fz[:, tt + B_HIST - (B_WIDTH - 1):tt + B_HIST, :]
    fz[:, 0:B_HIST, :] = fz[:, tt:tt + B_HIST, :]

    c0 = 5 * D_BR
    v = _layer_norm(proj(c0 + D_BR, c0 + 2 * D_BR), lcg_ref[...], lcb_ref[...])
    nv_ref[...] = v[rows - nv_rows * nb:, :].reshape(nb, nv_rows, D_BR)
    vb = v.astype(BF16)
    t_idx = lax.broadcasted_iota(jnp.int32, (CHUNK, CHUNK), 0)
    s_idx = lax.broadcasted_iota(jnp.int32, (CHUNK, CHUNK), 1)
    causal = s_idx <= t_idx
    for hd in range(C_HEADS):
        wm = jnp.where(causal, wsp_ref[hd], 0.0).astype(BF16)
        for c in range(rows // CHUNK):
            blk = (slice(c * CHUNK, (c + 1) * CHUNK), slice(hd * LANES, (hd + 1) * LANES))
            sc[blk] = _dot(wm, vb[blk])
    c_out = _dot((proj(c0, c0 + D_BR) * (sc[...].reshape(rows // CHUNK, CHUNK, D_BR)
                                        + bsp_ref[...]).reshape(rows, D_BR)).astype(BF16),
                 wpc_ref[...])

    c0 = 7 * D_BR
    merged = (_sigmoid(proj(c0, c0 + D_MODEL)) * a_out
              + _sigmoid(proj(c0 + D_MODEL, c0 + 2 * D_MODEL)) * b_out
              + _sigmoid(proj(c0 + 2 * D_MODEL, c0 + 3 * D_MODEL)) * c_out)
    xo_ref[...] = x + _dot(merged.astype(BF16), wout_ref[...])


def _const_spec(shape):
    nd = len(shape)
    return pl.BlockSpec(shape, lambda b, i: (0,) * nd, pipeline_mode=pl.Buffered(1))


def _drop_refs(kernel_fn, pos, count):
    def body(*refs):
        return kernel_fn(*refs[:pos], *refs[pos + count:])
    return body


def _mixer(x, past_a, past_b, w, *, t, nb, tt, in_row, out_row, out_rows, layer, depth,
           past_layer, out_buf=None, state_bufs=None):
    n = past_a.shape[1]
    rows = nb * tt
    assert n % nb == 0 and t % tt == 0 and rows % CHUNK == 0
    assert in_row % rows == 0 and out_row % rows == 0
    assert tt % SUBLANES == 0 and (tt % CONV_ROWS == 0 or
                                   (CONV_ROWS % tt == 0 and nb % (CONV_ROWS // tt) == 0))
    nv_rows = t - ((t - 1) // CHUNK) * CHUNK
    assert nv_rows <= tt
    n_t = t // tt
    past_spec = lambda r: pl.BlockSpec((None, nb, r, D_BR), lambda b, i: (past_layer, b, 0, 0))
    row_spec = lambda first: pl.BlockSpec((rows, D_MODEL),
                                          lambda b, i: (first // rows + b * n_t + i, 0))
    const_spec = lambda c, stacked: (
        pl.BlockSpec((None,) + c.shape[1:], lambda b, i: (layer, 0, 0),
                     pipeline_mode=pl.Buffered(1)) if stacked else _const_spec(c.shape))
    stacked = ("w_in", "w_proj_a", "w_proj_b", "w_proj_c", "w_out")
    names = ("g", "w_in", "conv_a_w", "conv_a_b", "ln_a_g", "ln_a_b", "conv_b_w", "ln_c_g",
             "ln_c_b", "wsp", "bsp", "w_proj_a", "w_proj_b", "w_proj_c", "w_out")
    body = functools.partial(_mixer_kernel, nb=nb, tt=tt, nv_rows=nv_rows)
    operands = [x, past_a, past_b] + [w[k] for k in names]
    in_specs = ([row_spec(in_row), past_spec(A_WIDTH - 1), past_spec(B_WIDTH - 1)]
                + [const_spec(w[k], k in stacked) for k in names])
    state_rows = (A_WIDTH - 1, B_WIDTH - 1, nv_rows)
    out_shape = [jax.ShapeDtypeStruct((out_rows, D_MODEL), F32)] + [
        jax.ShapeDtypeStruct((depth, n, r, D_BR), F32) for r in state_rows]
    n_in = len(operands)
    aliases = {}
    for k, carrier in enumerate((out_buf, *(state_bufs or (None,) * 3))):
        if carrier is not None:
            assert carrier.shape == out_shape[k].shape
            aliases[len(operands)] = k
            operands.append(carrier)
            in_specs.append(pl.BlockSpec(memory_space=pl.ANY))
    body = _drop_refs(body, n_in, len(operands) - n_in)
    layer_spec = lambda r: pl.BlockSpec((None, nb, r, D_BR), lambda b, i: (layer, b, 0, 0))
    return pl.pallas_call(
        body,
        grid=(n // nb, n_t),
        in_specs=in_specs,
        out_specs=[row_spec(out_row)] + [layer_spec(r) for r in state_rows],
        input_output_aliases=aliases,
        out_shape=out_shape,
        scratch_shapes=[pltpu.VMEM((nb, A_HIST + tt, D_BR), F32),
                        pltpu.VMEM((nb, B_HIST + tt, D_BR), F32),
                        pltpu.VMEM((nb * tt, D_BR), F32),
                        pltpu.VMEM((SUBLANES, nb, tt + A_HIST - SUBLANES, D_BR), F32),
                        pltpu.VMEM((nb, tt, D_BR), F32),
                        pltpu.VMEM((A_WIDTH, SUBLANES, D_BR), F32)],
        compiler_params=pltpu.CompilerParams(
            dimension_semantics=("arbitrary", "arbitrary"), vmem_limit_bytes=VMEM_LIMIT),
        name="mixer",
    )(*operands)


def _swiglu_part(h_b, wg_ref, wu_ref, wd_ref):
    gate = _dot(h_b, wg_ref[...])
    act = gate * _sigmoid(gate) * _dot(h_b, wu_ref[...])
    return _dot(act.astype(BF16), wd_ref[...])


def _ffn_kernel(x_ref, g_ref, wg_hbm, wu_hbm, wd_hbm, o_ref, wg_ref, wu_ref, wd_ref, stage, sem,
                *, tf, which):
    @pl.when(pl.program_id(0) == 0)
    def _():
        _load_expert_weights(which, (wg_hbm, wu_hbm, wd_hbm), (wg_ref, wu_ref, wd_ref), stage, sem)

    x = x_ref[...]
    h_b = _rms(x, g_ref[...]).astype(BF16)
    y = x
    for c0 in range(0, D_FF, tf):
        y = y + _swiglu_part(h_b, wg_ref.at[:, c0:c0 + tf], wu_ref.at[:, c0:c0 + tf],
                             wd_ref.at[c0:c0 + tf, :])
    o_ref[...] = y


def _ffn(x, g, wg, wu, wd, *, which, tm, tf):
    m, _ = x.shape
    assert m % tm == 0 and D_FF % tf == 0
    hbm = pl.BlockSpec(memory_space=pl.ANY)
    return pl.pallas_call(
        functools.partial(_ffn_kernel, tf=tf, which=which),
        grid=(m // tm,),
        in_specs=[pl.BlockSpec((tm, D_MODEL), lambda i: (i, 0)),
                  pl.BlockSpec((1, D_MODEL), lambda i: (0, 0)), hbm, hbm, hbm],
        out_specs=pl.BlockSpec((tm, D_MODEL), lambda i: (i, 0)),
        out_shape=jax.ShapeDtypeStruct((m, D_MODEL), F32),
        scratch_shapes=_weight_scratch(),
        compiler_params=pltpu.CompilerParams(
            dimension_semantics=("arbitrary",), vmem_limit_bytes=VMEM_LIMIT),
        name="ffn",
    )(x, g, wg, wu, wd)


SLAB = D_MODEL // LANES


def _to_slab(ref, val, rows):
    for c in range(SLAB):
        ref[pl.ds(c, rows, stride=SLAB), :] = val[:, c * LANES:(c + 1) * LANES]


def _route_kernel(x_ref, g_ref, wr_ref, br_ref, hs_ref, mi_ref, mf_ref, cnt_ref, carry, *, cap):
    i = pl.program_id(0)
    tm = x_ref.shape[0]

    @pl.when(i == 0)
    def _():
        carry[...] = jnp.zeros_like(carry)

    h = _rms(x_ref[...], g_ref[...])
    _to_slab(hs_ref, h, tm)

    col = lax.broadcasted_iota(jnp.int32, (tm, LANES), 1)
    logits = _dot_split(h, wr_ref[...]) + br_ref[...]
    logits = jnp.where(col < N_EXPERTS, logits, -jnp.inf)
    m1 = jnp.max(logits, -1, keepdims=True)
    i1 = jnp.min(jnp.where(logits == m1, col, LANES), -1, keepdims=True)
    rest = jnp.where(col == i1, -jnp.inf, logits)
    m2 = jnp.max(rest, -1, keepdims=True)
    i2 = jnp.min(jnp.where(rest == m2, col, LANES), -1, keepdims=True)
    e2 = jnp.exp(m2 - m1)
    p1 = 1.0 / (1.0 + e2)
    p2 = e2 * p1

    chosen = jnp.logical_or(col == i1, col == i2).astype(F32)
    r_idx = lax.broadcasted_iota(jnp.int32, (tm, tm), 0)
    c_idx = lax.broadcasted_iota(jnp.int32, (tm, tm), 1)
    before = (c_idx < r_idx).astype(BF16)
    rank = carry[...] + _dot(before, chosen.astype(BF16))
    dest = col.astype(F32) * float(cap) + rank
    d1 = jnp.sum(jnp.where(col == i1, dest, 0.0), -1, keepdims=True)
    d2 = jnp.sum(jnp.where(col == i2, dest, 0.0), -1, keepdims=True)
    by_choice = jnp.where(col == 0, d1, jnp.where(col == 1, d2, 0.0)).T
    mi_ref[...] = by_choice[0:SUBLANES, :].astype(jnp.int32)
    mf_ref[...] = jnp.where(col == 0, p1, jnp.where(col == 1, p2, 0.0))
    carry[...] += jnp.sum(chosen, 0, keepdims=True)
    cnt_ref[...] = carry[...]


def _route(x, g, wr, br, *, tm, cap):
    m = x.shape[0]
    assert m % tm == 0
    row = lambda i: (i, 0)
    const = lambda i: (0, 0)
    return pl.pallas_call(
        functools.partial(_route_kernel, cap=cap),
        grid=(m // tm,),
        in_specs=[pl.BlockSpec((tm, D_MODEL), row), pl.BlockSpec((1, D_MODEL), const),
                  pl.BlockSpec((D_MODEL, LANES), const), pl.BlockSpec((1, LANES), const)],
        out_specs=[pl.BlockSpec((tm * SLAB, LANES), row),
                   pl.BlockSpec((SUBLANES, tm), lambda i: (0, i)),
                   pl.BlockSpec((tm, LANES), row), pl.BlockSpec((1, LANES), const)],
        out_shape=[jax.ShapeDtypeStruct((m * SLAB, LANES), F32),
                   jax.ShapeDtypeStruct((SUBLANES, m), jnp.int32),
                   jax.ShapeDtypeStruct((m, LANES), F32),
                   jax.ShapeDtypeStruct((1, LANES), F32)],
        scratch_shapes=[pltpu.VMEM((1, LANES), F32)],
        compiler_params=pltpu.CompilerParams(dimension_semantics=("arbitrary",),
                                             vmem_limit_bytes=VMEM_LIMIT),
        name="route",
    )(x, g, wr, br)


def _row_copy(src_ref, src_row, dst_ref, dst_row, sem):
    return pltpu.make_async_copy(
        src_ref.at[pl.ds(pl.multiple_of(src_row * SLAB, SLAB), SLAB), :],
        dst_ref.at[pl.ds(pl.multiple_of(dst_row * SLAB, SLAB), SLAB), :], sem)


def _wait_rows(src_ref, dst_ref, n_rows, sem):
    pltpu.make_async_copy(src_ref.at[pl.ds(0, n_rows * SLAB), :],
                          dst_ref.at[pl.ds(0, n_rows * SLAB), :], sem).wait()


def _dispatch_kernel(idx1_ref, idx2_ref, src_ref, dst_ref, sem, *, rows):
    def issue(r, c):
        for k, idx_ref in enumerate((idx1_ref, idx2_ref)):
            _row_copy(src_ref, r, dst_ref, idx_ref[0, 0, r], sem).start(priority=k)
        return c

    lax.fori_loop(0, rows, issue, 0)
    for k in range(2):
        _wait_rows(src_ref, dst_ref, rows, sem)


def _idx_spec(rows, index_map):
    return pl.BlockSpec((1, 1, rows), index_map, memory_space=pltpu.SMEM)


def _dispatch(h_slab, dest1, dest2, *, rows, cap_rows):
    n_blk = dest1.shape[0] // rows
    blocked = lambda d: d.reshape(n_blk, 1, rows)
    return pl.pallas_call(
        functools.partial(_dispatch_kernel, rows=rows),
        grid=(n_blk,),
        in_specs=[_idx_spec(rows, lambda i: (i, 0, 0)), _idx_spec(rows, lambda i: (i, 0, 0)),
                  pl.BlockSpec((rows * SLAB, LANES), lambda i: (i, 0))],
        out_specs=pl.BlockSpec(memory_space=pl.ANY),
        out_shape=jax.ShapeDtypeStruct((cap_rows * SLAB, LANES), F32),
        scratch_shapes=[pltpu.SemaphoreType.DMA(())],
        compiler_params=pltpu.CompilerParams(dimension_semantics=("arbitrary",)),
        name="dispatch",
    )(blocked(dest1), blocked(dest2), h_slab)


W_ROWS, W_COLS = 256, 512
W_SLOTS = 16


def _load_expert_weights(e, srcs, dsts, stage, sem):
    jobs = []
    for src, dst in zip(srcs, dsts):
        n_r, n_c = dst.shape
        for r0 in range(0, n_r, W_ROWS):
            for c0 in range(0, n_c, W_COLS):
                jobs.append((src.at[e, pl.ds(r0, W_ROWS), pl.ds(c0, W_COLS)],
                             dst.at[r0:r0 + W_ROWS, c0:c0 + W_COLS]))
    copies = [pltpu.make_async_copy(src, stage.at[i % W_SLOTS], sem.at[i % W_SLOTS])
              for i, (src, _) in enumerate(jobs)]
    ahead = W_SLOTS - 1
    for cp in copies[:ahead]:
        cp.start()
    for i, (cp, (_, dst)) in enumerate(zip(copies, jobs)):
        cp.wait()
        dst[...] = stage[i % W_SLOTS].astype(BF16)
        if i + ahead < len(copies):
            copies[i + ahead].start()


def _weight_scratch():
    assert D_MODEL % W_ROWS == 0 and D_FF % W_ROWS == 0
    assert D_MODEL % W_COLS == 0 and D_FF % W_COLS == 0
    return [pltpu.VMEM((D_MODEL, D_FF), BF16), pltpu.VMEM((D_MODEL, D_FF), BF16),
            pltpu.VMEM((D_FF, D_MODEL), BF16), pltpu.VMEM((W_SLOTS, W_ROWS, W_COLS), F32),
            pltpu.SemaphoreType.DMA((W_SLOTS,))]


def _experts_kernel(tblk, texp, tn, tnew, xs_ref, wg_hbm, wu_hbm, wd_hbm, o_ref,
                    hb, wg_ref, wu_ref, wd_ref, stage, sem, *, tf):
    t = pl.program_id(0)
    n = tn[t]
    tm = hb.shape[0]

    @pl.when(tnew[t] == 1)
    def _():
        _load_expert_weights(texp[t], (wg_hbm, wu_hbm, wd_hbm), (wg_ref, wu_ref, wd_ref),
                             stage, sem)

    def run(rows):
        valid = lax.broadcasted_iota(jnp.int32, (rows, LANES), 0) < n
        for c in range(SLAB):
            xc = xs_ref[pl.ds(c, rows, stride=SLAB), :]
            hb[0:rows, c * LANES:(c + 1) * LANES] = jnp.where(valid, xc, 0.0).astype(BF16)
        h_b = hb[0:rows, :]
        y = _swiglu_part(h_b, wg_ref.at[:, 0:tf], wu_ref.at[:, 0:tf], wd_ref.at[0:tf, :])
        for c0 in range(tf, D_FF, tf):
            y = y + _swiglu_part(h_b, wg_ref.at[:, c0:c0 + tf], wu_ref.at[:, c0:c0 + tf],
                                 wd_ref.at[c0:c0 + tf, :])
        _to_slab(o_ref, y, rows)

    pl.when(n > tm // 2)(lambda: run(tm))
    pl.when(jnp.logical_and(n > 0, n <= tm // 2))(lambda: run(tm // 2))


def _experts(xs_slab, tblk, texp, tn, tnew, wg, wu, wd, *, tm, tf):
    n_tiles = tblk.shape[0]
    assert D_FF % tf == 0 and tm % (2 * SUBLANES) == 0
    tile = lambda t, tblk, texp, tn, tnew: (tblk[t], 0)
    hbm = pl.BlockSpec(memory_space=pl.ANY)
    grid_spec = pltpu.PrefetchScalarGridSpec(
        num_scalar_prefetch=4,
        grid=(n_tiles,),
        in_specs=[pl.BlockSpec((tm * SLAB, LANES), tile), hbm, hbm, hbm],
        out_specs=pl.BlockSpec((tm * SLAB, LANES), tile),
        scratch_shapes=[pltpu.VMEM((tm, D_MODEL), BF16)] + _weight_scratch())
    return pl.pallas_call(
        functools.partial(_experts_kernel, tf=tf),
        grid_spec=grid_spec,
        out_shape=jax.ShapeDtypeStruct(xs_slab.shape, F32),
        compiler_params=pltpu.CompilerParams(dimension_semantics=("arbitrary",),
                                             vmem_limit_bytes=VMEM_LIMIT),
        name="experts",
    )(tblk, texp, tn, tnew, xs_slab, wg, wu, wd)


def _combine_kernel(idx1_ref, idx2_ref, idx1_next_ref, idx2_next_ref, x_ref, mf_ref, gf_ref,
                    ys_ref, op_ref, os_ref, buf, ysum, sem, *, rows, n_prompt_tiles):
    i = pl.program_id(0)
    n_i = pl.num_programs(0)
    slot = i % 2

    def issue(refs, s):
        def body(r, c):
            for k, ref in enumerate(refs):
                _row_copy(ys_ref, ref[0, 0, r], buf.at[s], k * rows + r,
                          sem.at[s]).start(priority=k)
            return c
        lax.fori_loop(0, rows, body, 0, unroll=4)

    def finish(s):
        _wait_rows(ys_ref, buf.at[s], 2 * rows, sem.at[s])
        p1 = mf_ref[:, 0:1]
        p2 = mf_ref[:, 1:2]
        for c in range(SLAB):
            cs = slice(c * LANES, (c + 1) * LANES)
            y1 = buf[s, pl.ds(c, rows, stride=SLAB), :]
            y2 = buf[s, pl.ds(rows * SLAB + c, rows, stride=SLAB), :]
            ysum[:, cs] = x_ref[:, cs] + p1 * y1 + p2 * y2
        res = _rms(ysum[...], gf_ref[...])

        @pl.when(i < n_prompt_tiles)
        def _():
            op_ref[...] = res

        @pl.when(i >= n_prompt_tiles)
        def _():
            os_ref[...] = res

    for s in range(2):
        @pl.when(slot == s)
        def _():
            @pl.when(i == 0)
            def _():
                issue((idx1_ref, idx2_ref), s)

            @pl.when(i + 1 < n_i)
            def _():
                issue((idx1_next_ref, idx2_next_ref), 1 - s)

            finish(s)


def _combine(x, mf, g_final, y_slab, dest1, dest2, *, rows, m_prompt):
    m = x.shape[0]
    n_blk = m // rows
    assert m == n_blk * rows and m_prompt % rows == 0
    n_p = m_prompt // rows
    blocked = lambda d: d.reshape(n_blk, 1, rows)
    this = lambda i: (i, 0, 0)
    ahead = lambda i: (jnp.minimum(i + 1, n_blk - 1), 0, 0)
    return pl.pallas_call(
        functools.partial(_combine_kernel, rows=rows, n_prompt_tiles=n_p),
        grid=(n_blk,),
        in_specs=[_idx_spec(rows, this), _idx_spec(rows, this),
                  _idx_spec(rows, ahead), _idx_spec(rows, ahead),
                  pl.BlockSpec((rows, D_MODEL), lambda i: (i, 0)),
                  pl.BlockSpec((rows, LANES), lambda i: (i, 0)),
                  pl.BlockSpec((1, D_MODEL), lambda i: (0, 0)),
                  pl.BlockSpec(memory_space=pl.ANY)],
        out_specs=[pl.BlockSpec((rows, D_MODEL), lambda i: (jnp.minimum(i, n_p - 1), 0)),
                   pl.BlockSpec((rows, D_MODEL), lambda i: (jnp.maximum(i - n_p, 0), 0))],
        out_shape=[jax.ShapeDtypeStruct((m_prompt, D_MODEL), F32),
                   jax.ShapeDtypeStruct((m - m_prompt, D_MODEL), F32)],
        scratch_shapes=[pltpu.VMEM((2, 2 * rows * SLAB, LANES), F32),
                        pltpu.VMEM((rows, D_MODEL), F32),
                        pltpu.SemaphoreType.DMA((2,))],
        compiler_params=pltpu.CompilerParams(dimension_semantics=("arbitrary",),
                                             vmem_limit_bytes=VMEM_LIMIT),
        name="combine",
    )(blocked(dest1), blocked(dest2), blocked(dest1), blocked(dest2), x, mf, g_final, y_slab)


def _expert_tiles(counts, *, tm, cap, n_tiles):
    tiles = (counts + tm - 1) // tm
    ends = jnp.cumsum(tiles)
    t = jnp.arange(n_tiles, dtype=jnp.int32)
    tc = jnp.minimum(t, ends[-1] - 1)
    e = jnp.sum((tc[:, None] >= ends[None, :]).astype(jnp.int32), axis=1)
    k = tc - (ends - tiles)[e]
    tblk = e * (cap // tm) + k
    tn = jnp.where(t < ends[-1], jnp.minimum(counts[e] - k * tm, tm), 0)
    tnew = jnp.logical_and(t < ends[-1], k == 0)
    i32 = lambda v: v.astype(jnp.int32)
    return i32(tblk), i32(e), i32(tn), i32(tnew)


def _moe_final(x, g_ffn, wr, br, wg, wu, wd, g_final, *, m_prompt):
    m = x.shape[0]
    tm, tf, out_rows, in_rows = MLP_ROWS, MLP_COLS, DISPATCH_ROWS, COMBINE_ROWS
    cap = -(-m // tm) * tm
    h_slab, mi, mf, cnt = _route(x, g_ffn, wr, br, tm=ROUTE_ROWS, cap=cap)
    dest1, dest2 = mi[0], mi[1]
    counts = cnt[0, :N_EXPERTS].astype(jnp.int32)
    n_tiles = 2 * m // tm + N_EXPERTS
    tables = _expert_tiles(counts, tm=tm, cap=cap, n_tiles=n_tiles)
    xs_slab = _dispatch(h_slab, dest1, dest2, rows=out_rows, cap_rows=N_EXPERTS * cap)
    y_slab = _experts(xs_slab, *tables, wg, wu, wd, tm=tm, tf=tf)
    return _combine(x, mf, g_final, y_slab, dest1, dest2, rows=in_rows, m_prompt=m_prompt)


def _row(v):
    return v.reshape(1, -1)


def kernel(x_prompt, x_sample, state_conv_a, state_conv_b, norm_mix_g, w_in, conv_a_w, conv_a_b, ln_a_g, ln_a_b, conv_b_w, ln_c_g, ln_c_b, w_spatial, b_spatial, w_proj_a, w_proj_b, w_proj_c, w_out, norm_ffn_g, ffn_w_gate, ffn_w_up, ffn_w_down, w_router, b_router, moe_w_gate, moe_w_up, moe_w_down, norm_final_g):
    depth = w_in.shape[0]
    n_p, t_p, _ = x_prompt.shape
    n_s, t_s, _ = x_sample.shape
    seqs_per_chunk = CHUNK // t_s

    zeros_a = jnp.zeros((1, n_p, A_WIDTH - 1, D_BR), F32)
    zeros_b = jnp.zeros((1, n_p, B_WIDTH - 1, D_BR), F32)
    matrices = dict(w_in=w_in.astype(BF16), w_proj_a=w_proj_a.astype(BF16),
                    w_proj_b=w_proj_b.astype(BF16), w_proj_c=w_proj_c.astype(BF16),
                    w_out=w_out.astype(BF16))

    m_p, m_s = n_p * t_p, n_s * t_s
    m = m_p + m_s
    x_p, row_p = x_prompt.reshape(m_p, D_MODEL), 0
    x_s, row_s = x_sample.reshape(m_s, D_MODEL), 0
    states_p = states_s = None
    for l in range(depth):
        bsp_p = jnp.repeat(b_spatial[l].T, LANES, axis=1)
        eye = jnp.eye(seqs_per_chunk, dtype=F32)
        wsp_s = jnp.einsum("ab,hts->hatbs", eye, w_spatial[l][:, :t_s, :t_s]).reshape(
            C_HEADS, CHUNK, CHUNK)
        bsp_s = jnp.tile(bsp_p[:t_s], (seqs_per_chunk, 1))
        w = dict(matrices, g=_row(norm_mix_g[l]), conv_a_w=conv_a_w[l],
                 conv_a_b=_row(conv_a_b[l]), ln_a_g=_row(ln_a_g[l]), ln_a_b=_row(ln_a_b[l]),
                 conv_b_w=conv_b_w[l], ln_c_g=_row(ln_c_g[l]), ln_c_b=_row(ln_c_b[l]))
        x, *states_p = _mixer(x_p, zeros_a, zeros_b, dict(w, wsp=w_spatial[l], bsp=bsp_p),
                              t=t_p, nb=1, tt=MIXER_ROWS, in_row=row_p, out_row=0, out_rows=m,
                              layer=l, depth=depth, past_layer=0, state_bufs=states_p)
        x, *states_s = _mixer(x_s, state_conv_a, state_conv_b,
                              dict(w, wsp=wsp_s, bsp=bsp_s), t=t_s, nb=SAMPLE_SEQS, tt=t_s,
                              in_row=row_s, out_row=m_p, out_rows=m, layer=l, depth=depth,
                              past_layer=l, out_buf=x, state_bufs=states_s)

        i = l // 2
        g_ffn = _row(norm_ffn_g[l])
        if l % 2 == 0:
            x = _ffn(x, g_ffn, ffn_w_gate, ffn_w_up, ffn_w_down, which=i,
                     tm=MLP_ROWS, tf=MLP_COLS)
            x_p, row_p, x_s, row_s = x, 0, x, m_p
        else:
            assert l == depth - 1, "the routed layer applies the final norm"
            wr = jnp.pad(w_router[i], ((0, 0), (0, LANES - N_EXPERTS)))
            br = jnp.pad(_row(b_router[i]), ((0, 0), (0, LANES - N_EXPERTS)))
            y_p, y_s = _moe_final(x, g_ffn, wr, br, moe_w_gate[i], moe_w_up[i], moe_w_down[i],
                                  _row(norm_final_g), m_prompt=m_p)
    y_p = y_p.reshape(n_p, t_p, D_MODEL)
    y_s = y_s.reshape(n_s, t_s, D_MODEL)

    (a_p, b_p, v_p), (a_s, b_s, v_s) = states_p, states_s
    return (y_p, y_s, a_p, a_s, b_p, b_s, v_p, v_s)
```

```python
import functools

import jax
import jax.numpy as jnp
from jax import lax
from jax.experimental import pallas as pl
from jax.experimental.pallas import tpu as pltpu

D_MODEL = 1024
D_BR = 512
A_WIDTH = 31
B_WIDTH = 3
C_HEADS = 4
CHUNK = 128
D_FF = 3584
N_EXPERTS = 8
EPS = 1e-6

A_HIST = 32
B_HIST = 8
LANES = 128
SUBLANES = 8
CONV_ROWS = 32
VMEM_LIMIT = 56 * 1024 * 1024

MIXER_ROWS = 512
SAMPLE_SEQS = 32
MLP_ROWS = 512
MLP_COLS = 1792
ROUTE_ROWS = 512
DISPATCH_ROWS = 1024
COMBINE_ROWS = 512

BF16 = jnp.bfloat16
F32 = jnp.float32


def _rms(x, g):
    return x * lax.rsqrt(jnp.mean(x * x, -1, keepdims=True) + EPS) * g


def _layer_norm(x, g, b):
    mu = jnp.mean(x, -1, keepdims=True)
    xc = x - mu
    var = jnp.mean(xc * xc, -1, keepdims=True)
    return xc * lax.rsqrt(var + EPS) * g + b


def _sigmoid(x):
    return 0.5 * jnp.tanh(0.5 * x) + 0.5


def _dot(a, b):
    return jnp.dot(a, b, preferred_element_type=F32)


def _dot_split(a, b):
    a_hi = a.astype(BF16)
    b_hi = b.astype(BF16)
    a_lo = (a - a_hi.astype(F32)).astype(BF16)
    b_lo = (b - b_hi.astype(F32)).astype(BF16)
    return _dot(a_hi, b_hi) + (_dot(a_hi, b_lo) + _dot(a_lo, b_hi))


def _mixer_kernel(x_ref, pa_ref, pb_ref, g_ref, win_ref, caw_ref, cab_ref, lag_ref, lab_ref,
                  cbw_ref, lcg_ref, lcb_ref, wsp_ref, bsp_ref, wpa_ref, wpb_ref, wpc_ref,
                  wout_ref, xo_ref, na_ref, nb_ref, nv_ref, fa, fz, sc, sh, ca, wb, *, nb, tt, nv_rows):
    i = pl.program_id(1)
    rows = nb * tt

    @pl.when(i == 0)
    def _():
        fa[:, A_HIST - (A_WIDTH - 1):A_HIST, :] = pa_ref[...]
        fz[:, B_HIST - (B_WIDTH - 1):B_HIST, :] = pb_ref[...]

    x = x_ref[...]
    hb = _rms(x, g_ref[...]).astype(BF16)

    def proj(c0, c1):
        return _dot(hb, win_ref[:, c0:c1])

    a = proj(0, D_BR) * _sigmoid(proj(D_BR, 2 * D_BR))
    fa[:, A_HIST:A_HIST + tt, :] = a.reshape(nb, tt, D_BR)
    off_a = A_HIST - (A_WIDTH - 1)
    for j in range(SUBLANES):
        span = tt + SUBLANES * ((A_WIDTH - 1 - j) // SUBLANES)
        sh[j, :, 0:span, :] = fa[:, off_a + j:off_a + j + span, :]
    for k in range(A_WIDTH):
        wb[k] = jnp.broadcast_to(caw_ref[k:k + 1, :], (SUBLANES, D_BR))
    rb = min(tt, CONV_ROWS)
    sb = CONV_ROWS // rb
    n_rb = tt // rb

    def conv_step(u, carry):
        s0 = (u // n_rb) * sb
        r0 = (u % n_rb) * rb
        groups = range(0, rb, SUBLANES)
        acc = [jnp.broadcast_to(cab_ref[...], (sb, SUBLANES, D_BR)) for _ in groups]
        for k in range(A_WIDTH):
            q, j = divmod(k, SUBLANES)
            w_k = wb[k]
            for gi, g in enumerate(groups):
                lo = r0 + g + SUBLANES * q
                acc[gi] = acc[gi] + w_k * sh[j, s0:s0 + sb, lo:lo + SUBLANES, :]
        for gi, g in enumerate(groups):
            ca[s0:s0 + sb, r0 + g:r0 + g + SUBLANES, :] = acc[gi]
        return carry

    for u in range((nb // sb) * n_rb):
        conv_step(u, 0)
    a_ln = _layer_norm(ca[...].reshape(rows, D_BR), lag_ref[...], lab_ref[...])
    a_out = _dot((a_ln * _sigmoid(a_ln)).astype(BF16), wpa_ref[...])
    na_ref[...] = fa[:, tt + A_HIST - (A_WIDTH - 1):tt + A_HIST, :]
    fa[:, 0:A_HIST, :] = fa[:, tt:tt + A_HIST, :]

    c0 = 2 * D_BR
    z = proj(c0 + 2 * D_BR, c0 + 3 * D_BR) * proj(c0, c0 + D_BR)
    fz[:, B_HIST:B_HIST + tt, :] = z.reshape(nb, tt, D_BR)
    off_b = B_HIST - (B_WIDTH - 1)
    accb = jnp.zeros((nb, tt, D_BR), F32)
    for k in range(B_WIDTH):
        accb = accb + cbw_ref[k:k + 1, :] * fz[:, off_b + k:off_b + k + tt, :]
    b_out = _dot((proj(c0 + D_BR, c0 + 2 * D_BR) * accb.reshape(rows, D_BR)).astype(BF16),
                 wpb_ref[...])
    nb_ref[...] = fz[:, tt + B_HIST - (B_WIDTH - 1):tt + B_HIST, :]
    fz[:, 0:B_HIST, :] = fz[:, tt:tt + B_HIST, :]

    c0 = 5 * D_BR
    v = _layer_norm(proj(c0 + D_BR, c0 + 2 * D_BR), lcg_ref[...], lcb_ref[...])
    nv_ref[...] = v[rows - nv_rows * nb:, :].reshape(nb, nv_rows, D_BR)
    vb = v.astype(BF16)
    t_idx = lax.broadcasted_iota(jnp.int32, (CHUNK, CHUNK), 0)
    s_idx = lax.broadcasted_iota(jnp.int32, (CHUNK, CHUNK), 1)
    causal = s_idx <= t_idx
    for hd in range(C_HEADS):
        wm = jnp.where(causal, wsp_ref[hd], 0.0).astype(BF16)
        for c in range(rows // CHUNK):
            blk = (slice(c * CHUNK, (c + 1) * CHUNK), slice(hd * LANES, (hd + 1) * LANES))
            sc[blk] = _dot(wm, vb[blk])
    c_out = _dot((proj(c0, c0 + D_BR) * (sc[...].reshape(rows // CHUNK, CHUNK, D_BR)
                                        + bsp_ref[...]).reshape(rows, D_BR)).astype(BF16),
                 wpc_ref[...])

    c0 = 7 * D_BR
    merged = (_sigmoid(proj(c0, c0 + D_MODEL)) * a_out
              + _sigmoid(proj(c0 + D_MODEL, c0 + 2 * D_MODEL)) * b_out
              + _sigmoid(proj(c0 + 2 * D_MODEL, c0 + 3 * D_MODEL)) * c_out)
    xo_ref[...] = x + _dot(merged.astype(BF16), wout_ref[...])


def _const_spec(shape):
    nd = len(shape)
    return pl.BlockSpec(shape, lambda b, i: (0,) * nd, pipeline_mode=pl.Buffered(1))


def _drop_refs(kernel_fn, pos, count):
    def body(*refs):
        return kernel_fn(*refs[:pos], *refs[pos + count:])
    return body


def _mixer(x, past_a, past_b, w, *, t, nb, tt, in_row, out_row, out_rows, layer, depth,
           past_layer, out_buf=None, state_bufs=None):
    n = past_a.shape[1]
    rows = nb * tt
    assert n % nb == 0 and t % tt == 0 and rows % CHUNK == 0
    assert in_row % rows == 0 and out_row % rows == 0
    assert tt % SUBLANES == 0 and (tt % CONV_ROWS == 0 or
                                   (CONV_ROWS % tt == 0 and nb % (CONV_ROWS // tt) == 0))
    nv_rows = t - ((t - 1) // CHUNK) * CHUNK
    assert nv_rows <= tt
    n_t = t // tt
    past_spec = lambda r: pl.BlockSpec((None, nb, r, D_BR), lambda b, i: (past_layer, b, 0, 0))
    row_spec = lambda first: pl.BlockSpec((rows, D_MODEL),
                                          lambda b, i: (first // rows + b * n_t + i, 0))
    const_spec = lambda c, stacked: (
        pl.BlockSpec((None,) + c.shape[1:], lambda b, i: (layer, 0, 0),
                     pipeline_mode=pl.Buffered(1)) if stacked else _const_spec(c.shape))
    stacked = ("w_in", "w_proj_a", "w_proj_b", "w_proj_c", "w_out")
    names = ("g", "w_in", "conv_a_w", "conv_a_b", "ln_a_g", "ln_a_b", "conv_b_w", "ln_c_g",
             "ln_c_b", "wsp", "bsp", "w_proj_a", "w_proj_b", "w_proj_c", "w_out")
    body = functools.partial(_mixer_kernel, nb=nb, tt=tt, nv_rows=nv_rows)
    operands = [x, past_a, past_b] + [w[k] for k in names]
    in_specs = ([row_spec(in_row), past_spec(A_WIDTH - 1), past_spec(B_WIDTH - 1)]
                + [const_spec(w[k], k in stacked) for k in names])
    state_rows = (A_WIDTH - 1, B_WIDTH - 1, nv_rows)
    out_shape = [jax.ShapeDtypeStruct((out_rows, D_MODEL), F32)] + [
        jax.ShapeDtypeStruct((depth, n, r, D_BR), F32) for r in state_rows]
    n_in = len(operands)
    aliases = {}
    for k, carrier in enumerate((out_buf, *(state_bufs or (None,) * 3))):
        if carrier is not None:
            assert carrier.shape == out_shape[k].shape
            aliases[len(operands)] = k
            operands.append(carrier)
            in_specs.append(pl.BlockSpec(memory_space=pl.ANY))
    body = _drop_refs(body, n_in, len(operands) - n_in)
    layer_spec = lambda r: pl.BlockSpec((None, nb, r, D_BR), lambda b, i: (layer, b, 0, 0))
    return pl.pallas_call(
        body,
        grid=(n // nb, n_t),
        in_specs=in_specs,
        out_specs=[row_spec(out_row)] + [layer_spec(r) for r in state_rows],
        input_output_aliases=aliases,
        out_shape=out_shape,
        scratch_shapes=[pltpu.VMEM((nb, A_HIST + tt, D_BR), F32),
                        pltpu.VMEM((nb, B_HIST + tt, D_BR), F32),
                        pltpu.VMEM((nb * tt, D_BR), F32),
                        pltpu.VMEM((SUBLANES, nb, tt + A_HIST - SUBLANES, D_BR), F32),
                        pltpu.VMEM((nb, tt, D_BR), F32),
                        pltpu.VMEM((A_WIDTH, SUBLANES, D_BR), F32)],
        compiler_params=pltpu.CompilerParams(
            dimension_semantics=("arbitrary", "arbitrary"), vmem_limit_bytes=VMEM_LIMIT),
        name="mixer",
    )(*operands)


def _swiglu_part(h_b, wg_ref, wu_ref, wd_ref):
    gate = _dot(h_b, wg_ref[...])
    act = gate * _sigmoid(gate) * _dot(h_b, wu_ref[...])
    return _dot(act.astype(BF16), wd_ref[...])


def _ffn_kernel(x_ref, g_ref, wg_hbm, wu_hbm, wd_hbm, o_ref, wg_ref, wu_ref, wd_ref, stage, sem,
                *, tf, which):
    @pl.when(pl.program_id(0) == 0)
    def _():
        _load_expert_weights(which, (wg_hbm, wu_hbm, wd_hbm), (wg_ref, wu_ref, wd_ref), stage, sem)

    x = x_ref[...]
    h_b = _rms(x, g_ref[...]).astype(BF16)
    y = x
    for c0 in range(0, D_FF, tf):
        y = y + _swiglu_part(h_b, wg_ref.at[:, c0:c0 + tf], wu_ref.at[:, c0:c0 + tf],
                             wd_ref.at[c0:c0 + tf, :])
    o_ref[...] = y


def _ffn(x, g, wg, wu, wd, *, which, tm, tf):
    m, _ = x.shape
    assert m % tm == 0 and D_FF % tf == 0
    hbm = pl.BlockSpec(memory_space=pl.ANY)
    return pl.pallas_call(
        functools.partial(_ffn_kernel, tf=tf, which=which),
        grid=(m // tm,),
        in_specs=[pl.BlockSpec((tm, D_MODEL), lambda i: (i, 0)),
                  pl.BlockSpec((1, D_MODEL), lambda i: (0, 0)), hbm, hbm, hbm],
        out_specs=pl.BlockSpec((tm, D_MODEL), lambda i: (i, 0)),
        out_shape=jax.ShapeDtypeStruct((m, D_MODEL), F32),
        scratch_shapes=_weight_scratch(),
        compiler_params=pltpu.CompilerParams(
            dimension_semantics=("arbitrary",), vmem_limit_bytes=VMEM_LIMIT),
        name="ffn",
    )(x, g, wg, wu, wd)


SLAB = D_MODEL // LANES


def _to_slab(ref, val, rows):
    for c in range(SLAB):
        ref[pl.ds(c, rows, stride=SLAB), :] = val[:, c * LANES:(c + 1) * LANES]


def _route_kernel(x_ref, g_ref, wr_ref, br_ref, hs_ref, mi_ref, mf_ref, cnt_ref, carry, *, cap):
    i = pl.program_id(0)
    tm = x_ref.shape[0]

    @pl.when(i == 0)
    def _():
        carry[...] = jnp.zeros_like(carry)

    h = _rms(x_ref[...], g_ref[...])
    _to_slab(hs_ref, h, tm)

    col = lax.broadcasted_iota(jnp.int32, (tm, LANES), 1)
    logits = _dot_split(h, wr_ref[...]) + br_ref[...]
    logits = jnp.where(col < N_EXPERTS, logits, -jnp.inf)
    m1 = jnp.max(logits, -1, keepdims=True)
    i1 = jnp.min(jnp.where(logits == m1, col, LANES), -1, keepdims=True)
    rest = jnp.where(col == i1, -jnp.inf, logits)
    m2 = jnp.max(rest, -1, keepdims=True)
    i2 = jnp.min(jnp.where(rest == m2, col, LANES), -1, keepdims=True)
    e2 = jnp.exp(m2 - m1)
    p1 = 1.0 / (1.0 + e2)
    p2 = e2 * p1

    chosen = jnp.logical_or(col == i1, col == i2).astype(F32)
    r_idx = lax.broadcasted_iota(jnp.int32, (tm, tm), 0)
    c_idx = lax.broadcasted_iota(jnp.int32, (tm, tm), 1)
    before = (c_idx < r_idx).astype(BF16)
    rank = carry[...] + _dot(before, chosen.astype(BF16))
    dest = col.astype(F32) * float(cap) + rank
    d1 = jnp.sum(jnp.where(col == i1, dest, 0.0), -1, keepdims=True)
    d2 = jnp.sum(jnp.where(col == i2, dest, 0.0), -1, keepdims=True)
    by_choice = jnp.where(col == 0, d1, jnp.where(col == 1, d2, 0.0)).T
    mi_ref[...] = by_choice[0:SUBLANES, :].astype(jnp.int32)
    mf_ref[...] = jnp.where(col == 0, p1, jnp.where(col == 1, p2, 0.0))
    carry[...] += jnp.sum(chosen, 0, keepdims=True)
    cnt_ref[...] = carry[...]


def _route(x, g, wr, br, *, tm, cap):
    m = x.shape[0]
    assert m % tm == 0
    row = lambda i: (i, 0)
    const = lambda i: (0, 0)
    return pl.pallas_call(
        functools.partial(_route_kernel, cap=cap),
        grid=(m // tm,),
        in_specs=[pl.BlockSpec((tm, D_MODEL), row), pl.BlockSpec((1, D_MODEL), const),
                  pl.BlockSpec((D_MODEL, LANES), const), pl.BlockSpec((1, LANES), const)],
        out_specs=[pl.BlockSpec((tm * SLAB, LANES), row),
                   pl.BlockSpec((SUBLANES, tm), lambda i: (0, i)),
                   pl.BlockSpec((tm, LANES), row), pl.BlockSpec((1, LANES), const)],
        out_shape=[jax.ShapeDtypeStruct((m * SLAB, LANES), F32),
                   jax.ShapeDtypeStruct((SUBLANES, m), jnp.int32),
                   jax.ShapeDtypeStruct((m, LANES), F32),
                   jax.ShapeDtypeStruct((1, LANES), F32)],
        scratch_shapes=[pltpu.VMEM((1, LANES), F32)],
        compiler_params=pltpu.CompilerParams(dimension_semantics=("arbitrary",),
                                             vmem_limit_bytes=VMEM_LIMIT),
        name="route",
    )(x, g, wr, br)


def _row_copy(src_ref, src_row, dst_ref, dst_row, sem):
    return pltpu.make_async_copy(
        src_ref.at[pl.ds(pl.multiple_of(src_row * SLAB, SLAB), SLAB), :],
        dst_ref.at[pl.ds(pl.multiple_of(dst_row * SLAB, SLAB), SLAB), :], sem)


def _wait_rows(src_ref, dst_ref, n_rows, sem):
    pltpu.make_async_copy(src_ref.at[pl.ds(0, n_rows * SLAB), :],
                          dst_ref.at[pl.ds(0, n_rows * SLAB), :], sem).wait()


def _dispatch_kernel(idx1_ref, idx2_ref, src_ref, dst_ref, sem, *, rows):
    def issue(r, c):
        for k, idx_ref in enumerate((idx1_ref, idx2_ref)):
            _row_copy(src_ref, r, dst_ref, idx_ref[0, 0, r], sem).start(priority=k)
        return c

    lax.fori_loop(0, rows, issue, 0)
    for k in range(2):
        _wait_rows(src_ref, dst_ref, rows, sem)


def _idx_spec(rows, index_map):
    return pl.BlockSpec((1, 1, rows), index_map, memory_space=pltpu.SMEM)


def _dispatch(h_slab, dest1, dest2, *, rows, cap_rows):
    n_blk = dest1.shape[0] // rows
    blocked = lambda d: d.reshape(n_blk, 1, rows)
    return pl.pallas_call(
        functools.partial(_dispatch_kernel, rows=rows),
        grid=(n_blk,),
        in_specs=[_idx_spec(rows, lambda i: (i, 0, 0)), _idx_spec(rows, lambda i: (i, 0, 0)),
                  pl.BlockSpec((rows * SLAB, LANES), lambda i: (i, 0))],
        out_specs=pl.BlockSpec(memory_space=pl.ANY),
        out_shape=jax.ShapeDtypeStruct((cap_rows * SLAB, LANES), F32),
        scratch_shapes=[pltpu.SemaphoreType.DMA(())],
        compiler_params=pltpu.CompilerParams(dimension_semantics=("arbitrary",)),
        name="dispatch",
    )(blocked(dest1), blocked(dest2), h_slab)


W_ROWS, W_COLS = 256, 512
W_SLOTS = 32


def _load_expert_weights(e, srcs, dsts, stage, sem):
    jobs = []
    for src, dst in zip(srcs, dsts):
        n_r, n_c = dst.shape
        for r0 in range(0, n_r, W_ROWS):
            for c0 in range(0, n_c, W_COLS):
                jobs.append((src.at[e, pl.ds(r0, W_ROWS), pl.ds(c0, W_COLS)],
                             dst.at[r0:r0 + W_ROWS, c0:c0 + W_COLS]))
    copies = [pltpu.make_async_copy(src, stage.at[i % W_SLOTS], sem.at[i % W_SLOTS])
              for i, (src, _) in enumerate(jobs)]
    ahead = W_SLOTS - 1
    for cp in copies[:ahead]:
        cp.start()
    for i, (cp, (_, dst)) in enumerate(zip(copies, jobs)):
        cp.wait()
        dst[...] = stage[i % W_SLOTS].astype(BF16)
        if i + ahead < len(copies):
            copies[i + ahead].start()


def _weight_scratch():
    assert D_MODEL % W_ROWS == 0 and D_FF % W_ROWS == 0
    assert D_MODEL % W_COLS == 0 and D_FF % W_COLS == 0
    return [pltpu.VMEM((D_MODEL, D_FF), BF16), pltpu.VMEM((D_MODEL, D_FF), BF16),
            pltpu.VMEM((D_FF, D_MODEL), BF16), pltpu.VMEM((W_SLOTS, W_ROWS, W_COLS), F32),
            pltpu.SemaphoreType.DMA((W_SLOTS,))]


def _experts_kernel(tblk, texp, tn, tnew, xs_ref, wg_hbm, wu_hbm, wd_hbm, o_ref,
                    hb, wg_ref, wu_ref, wd_ref, stage, sem, *, tf):
    t = pl.program_id(0)
    n = tn[t]
    tm = hb.shape[0]

    @pl.when(tnew[t] == 1)
    def _():
        _load_expert_weights(texp[t], (wg_hbm, wu_hbm, wd_hbm), (wg_ref, wu_ref, wd_ref),
                             stage, sem)

    def run(rows):
        valid = lax.broadcasted_iota(jnp.int32, (rows, LANES), 0) < n
        for c in range(SLAB):
            xc = xs_ref[pl.ds(c, rows, stride=SLAB), :]
            hb[0:rows, c * LANES:(c + 1) * LANES] = jnp.where(valid, xc, 0.0).astype(BF16)
        h_b = hb[0:rows, :]
        y = _swiglu_part(h_b, wg_ref.at[:, 0:tf], wu_ref.at[:, 0:tf], wd_ref.at[0:tf, :])
        for c0 in range(tf, D_FF, tf):
            y = y + _swiglu_part(h_b, wg_ref.at[:, c0:c0 + tf], wu_ref.at[:, c0:c0 + tf],
                                 wd_ref.at[c0:c0 + tf, :])
        _to_slab(o_ref, y, rows)

    pl.when(n > tm // 2)(lambda: run(tm))
    pl.when(jnp.logical_and(n > 0, n <= tm // 2))(lambda: run(tm // 2))


def _experts(xs_slab, tblk, texp, tn, tnew, wg, wu, wd, *, tm, tf):
    n_tiles = tblk.shape[0]
    assert D_FF % tf == 0 and tm % (2 * SUBLANES) == 0
    tile = lambda t, tblk, texp, tn, tnew: (tblk[t], 0)
    hbm = pl.BlockSpec(memory_space=pl.ANY)
    grid_spec = pltpu.PrefetchScalarGridSpec(
        num_scalar_prefetch=4,
        grid=(n_tiles,),
        in_specs=[pl.BlockSpec((tm * SLAB, LANES), tile), hbm, hbm, hbm],
        out_specs=pl.BlockSpec((tm * SLAB, LANES), tile),
        scratch_shapes=[pltpu.VMEM((tm, D_MODEL), BF16)] + _weight_scratch())
    return pl.pallas_call(
        functools.partial(_experts_kernel, tf=tf),
        grid_spec=grid_spec,
        out_shape=jax.ShapeDtypeStruct(xs_slab.shape, F32),
        compiler_params=pltpu.CompilerParams(dimension_semantics=("arbitrary",),
                                             vmem_limit_bytes=VMEM_LIMIT),
        name="experts",
    )(tblk, texp, tn, tnew, xs_slab, wg, wu, wd)


def _combine_kernel(idx1_ref, idx2_ref, idx1_next_ref, idx2_next_ref, x_ref, mf_ref, gf_ref,
                    ys_ref, op_ref, os_ref, buf, ysum, sem, *, rows, n_prompt_tiles):
    i = pl.program_id(0)
    n_i = pl.num_programs(0)
    slot = i % 2

    def issue(refs, s):
        def body(r, c):
            for k, ref in enumerate(refs):
                _row_copy(ys_ref, ref[0, 0, r], buf.at[s], k * rows + r,
                          sem.at[s]).start(priority=k)
            return c
        lax.fori_loop(0, rows, body, 0, unroll=4)

    def finish(s):
        _wait_rows(ys_ref, buf.at[s], 2 * rows, sem.at[s])
        p1 = mf_ref[:, 0:1]
        p2 = mf_ref[:, 1:2]
        for c in range(SLAB):
            cs = slice(c * LANES, (c + 1) * LANES)
            y1 = buf[s, pl.ds(c, rows, stride=SLAB), :]
            y2 = buf[s, pl.ds(rows * SLAB + c, rows, stride=SLAB), :]
            ysum[:, cs] = x_ref[:, cs] + p1 * y1 + p2 * y2
        res = _rms(ysum[...], gf_ref[...])

        @pl.when(i < n_prompt_tiles)
        def _():
            op_ref[...] = res

        @pl.when(i >= n_prompt_tiles)
        def _():
            os_ref[...] = res

    for s in range(2):
        @pl.when(slot == s)
        def _():
            @pl.when(i == 0)
            def _():
                issue((idx1_ref, idx2_ref), s)

            @pl.when(i + 1 < n_i)
            def _():
                issue((idx1_next_ref, idx2_next_ref), 1 - s)

            finish(s)


def _combine(x, mf, g_final, y_slab, dest1, dest2, *, rows, m_prompt):
    m = x.shape[0]
    n_blk = m // rows
    assert m == n_blk * rows and m_prompt % rows == 0
    n_p = m_prompt // rows
    blocked = lambda d: d.reshape(n_blk, 1, rows)
    this = lambda i: (i, 0, 0)
    ahead = lambda i: (jnp.minimum(i + 1, n_blk - 1), 0, 0)
    return pl.pallas_call(
        functools.partial(_combine_kernel, rows=rows, n_prompt_tiles=n_p),
        grid=(n_blk,),
        in_specs=[_idx_spec(rows, this), _idx_spec(rows, this),
                  _idx_spec(rows, ahead), _idx_spec(rows, ahead),
                  pl.BlockSpec((rows, D_MODEL), lambda i: (i, 0)),
                  pl.BlockSpec((rows, LANES), lambda i: (i, 0)),
                  pl.BlockSpec((1, D_MODEL), lambda i: (0, 0)),
                  pl.BlockSpec(memory_space=pl.ANY)],
        out_specs=[pl.BlockSpec((rows, D_MODEL), lambda i: (jnp.minimum(i, n_p - 1), 0)),
                   pl.BlockSpec((rows, D_MODEL), lambda i: (jnp.maximum(i - n_p, 0), 0))],
        out_shape=[jax.ShapeDtypeStruct((m_prompt, D_MODEL), F32),
                   jax.ShapeDtypeStruct((m - m_prompt, D_MODEL), F32)],
        scratch_shapes=[pltpu.VMEM((2, 2 * rows * SLAB, LANES), F32),
                        pltpu.VMEM((rows, D_MODEL), F32),
                        pltpu.SemaphoreType.DMA((2,))],
        compiler_params=pltpu.CompilerParams(dimension_semantics=("arbitrary",),
                                             vmem_limit_bytes=VMEM_LIMIT),
        name="combine",
    )(blocked(dest1), blocked(dest2), blocked(dest1), blocked(dest2), x, mf, g_final, y_slab)


def _expert_tiles(counts, *, tm, cap, n_tiles):
    tiles = (counts + tm - 1) // tm
    ends = jnp.cumsum(tiles)
    t = jnp.arange(n_tiles, dtype=jnp.int32)
    tc = jnp.minimum(t, ends[-1] - 1)
    e = jnp.sum((tc[:, None] >= ends[None, :]).astype(jnp.int32), axis=1)
    k = tc - (ends - tiles)[e]
    tblk = e * (cap // tm) + k
    tn = jnp.where(t < ends[-1], jnp.minimum(counts[e] - k * tm, tm), 0)
    tnew = jnp.logical_and(t < ends[-1], k == 0)
    i32 = lambda v: v.astype(jnp.int32)
    return i32(tblk), i32(e), i32(tn), i32(tnew)


def _moe_final(x, g_ffn, wr, br, wg, wu, wd, g_final, *, m_prompt):
    m = x.shape[0]
    tm, tf, out_rows, in_rows = MLP_ROWS, MLP_COLS, DISPATCH_ROWS, COMBINE_ROWS
    cap = -(-m // tm) * tm
    h_slab, mi, mf, cnt = _route(x, g_ffn, wr, br, tm=ROUTE_ROWS, cap=cap)
    dest1, dest2 = mi[0], mi[1]
    counts = cnt[0, :N_EXPERTS].astype(jnp.int32)
    n_tiles = 2 * m // tm + N_EXPERTS
    tables = _expert_tiles(counts, tm=tm, cap=cap, n_tiles=n_tiles)
    xs_slab = _dispatch(h_slab, dest1, dest2, rows=out_rows, cap_rows=N_EXPERTS * cap)
    y_slab = _experts(xs_slab, *tables, wg, wu, wd, tm=tm, tf=tf)
    return _combine(x, mf, g_final, y_slab, dest1, dest2, rows=in_rows, m_prompt=m_prompt)


def _row(v):
    return v.reshape(1, -1)


def kernel(x_prompt, x_sample, state_conv_a, state_conv_b, norm_mix_g, w_in, conv_a_w, conv_a_b, ln_a_g, ln_a_b, conv_b_w, ln_c_g, ln_c_b, w_spatial, b_spatial, w_proj_a, w_proj_b, w_proj_c, w_out, norm_ffn_g, ffn_w_gate, ffn_w_up, ffn_w_down, w_router, b_router, moe_w_gate, moe_w_up, moe_w_down, norm_final_g):
    depth = w_in.shape[0]
    n_p, t_p, _ = x_prompt.shape
    n_s, t_s, _ = x_sample.shape
    seqs_per_chunk = CHUNK // t_s

    zeros_a = jnp.zeros((1, n_p, A_WIDTH - 1, D_BR), F32)
    zeros_b = jnp.zeros((1, n_p, B_WIDTH - 1, D_BR), F32)
    matrices = dict(w_in=w_in.astype(BF16), w_proj_a=w_proj_a.astype(BF16),
                    w_proj_b=w_proj_b.astype(BF16), w_proj_c=w_proj_c.astype(BF16),
                    w_out=w_out.astype(BF16))

    m_p, m_s = n_p * t_p, n_s * t_s
    m = m_p + m_s
    x_p, row_p = x_prompt.reshape(m_p, D_MODEL), 0
    x_s, row_s = x_sample.reshape(m_s, D_MODEL), 0
    states_p = states_s = None
    for l in range(depth):
        bsp_p = jnp.repeat(b_spatial[l].T, LANES, axis=1)
        eye = jnp.eye(seqs_per_chunk, dtype=F32)
        wsp_s = jnp.einsum("ab,hts->hatbs", eye, w_spatial[l][:, :t_s, :t_s]).reshape(
            C_HEADS, CHUNK, CHUNK)
        bsp_s = jnp.tile(bsp_p[:t_s], (seqs_per_chunk, 1))
        w = dict(matrices, g=_row(norm_mix_g[l]), conv_a_w=conv_a_w[l],
                 conv_a_b=_row(conv_a_b[l]), ln_a_g=_row(ln_a_g[l]), ln_a_b=_row(ln_a_b[l]),
                 conv_b_w=conv_b_w[l], ln_c_g=_row(ln_c_g[l]), ln_c_b=_row(ln_c_b[l]))
        x, *states_p = _mixer(x_p, zeros_a, zeros_b, dict(w, wsp=w_spatial[l], bsp=bsp_p),
                              t=t_p, nb=1, tt=MIXER_ROWS, in_row=row_p, out_row=0, out_rows=m,
                              layer=l, depth=depth, past_layer=0, state_bufs=states_p)
        x, *states_s = _mixer(x_s, state_conv_a, state_conv_b,
                              dict(w, wsp=wsp_s, bsp=bsp_s), t=t_s, nb=SAMPLE_SEQS, tt=t_s,
                              in_row=row_s, out_row=m_p, out_rows=m, layer=l, depth=depth,
                              past_layer=l, out_buf=x, state_bufs=states_s)

        i = l // 2
        g_ffn = _row(norm_ffn_g[l])
        if l % 2 == 0:
            x = _ffn(x, g_ffn, ffn_w_gate, ffn_w_up, ffn_w_down, which=i,
                     tm=MLP_ROWS, tf=MLP_COLS)
            x_p, row_p, x_s, row_s = x, 0, x, m_p
        else:
            assert l == depth - 1, "the routed layer applies the final norm"
            wr = jnp.pad(w_router[i], ((0, 0), (0, LANES - N_EXPERTS)))
            br = jnp.pad(_row(b_router[i]), ((0, 0), (0, LANES - N_EXPERTS)))
            y_p, y_s = _moe_final(x, g_ffn, wr, br, moe_w_gate[i], moe_w_up[i], moe_w_down[i],
                                  _row(norm_final_g), m_prompt=m_p)
    y_p = y_p.reshape(n_p, t_p, D_MODEL)
    y_s = y_s.reshape(n_s, t_s, D_MODEL)

    (a_p, b_p, v_p), (a_s, b_s, v_s) = states_p, states_s
    return (y_p, y_s, a_p, a_s, b_p, b_s, v_p, v_s)
```

```python
import functools

import jax
import jax.numpy as jnp
from jax import lax
from jax.experimental import pallas as pl
from jax.experimental.pallas import tpu as pltpu

D_MODEL = 1024
D_BR = 512
A_WIDTH = 31
B_WIDTH = 3
C_HEADS = 4
CHUNK = 128
D_FF = 3584
N_EXPERTS = 8
EPS = 1e-6

A_HIST = 32
B_HIST = 8
LANES = 128
SUBLANES = 8
CONV_ROWS = 32
VMEM_LIMIT = 56 * 1024 * 1024

MIXER_ROWS = 512
SAMPLE_SEQS = 32
MLP_ROWS = 512
MLP_COLS = 1792
ROUTE_ROWS = 512
DISPATCH_ROWS = 1024
COMBINE_ROWS = 512

BF16 = jnp.bfloat16
F32 = jnp.float32


def _rms(x, g):
    return x * lax.rsqrt(jnp.mean(x * x, -1, keepdims=True) + EPS) * g


def _layer_norm(x, g, b):
    mu = jnp.mean(x, -1, keepdims=True)
    xc = x - mu
    var = jnp.mean(xc * xc, -1, keepdims=True)
    return xc * lax.rsqrt(var + EPS) * g + b


def _sigmoid(x):
    return 0.5 * jnp.tanh(0.5 * x) + 0.5


def _dot(a, b):
    return jnp.dot(a, b, preferred_element_type=F32)


def _dot_split(a, b):
    a_hi = a.astype(BF16)
    b_hi = b.astype(BF16)
    a_lo = (a - a_hi.astype(F32)).astype(BF16)
    b_lo = (b - b_hi.astype(F32)).astype(BF16)
    return _dot(a_hi, b_hi) + (_dot(a_hi, b_lo) + _dot(a_lo, b_hi))


def _mixer_kernel(x_ref, pa_ref, pb_ref, g_ref, win_ref, caw_ref, cab_ref, lag_ref, lab_ref,
                  cbw_ref, lcg_ref, lcb_ref, wsp_ref, bsp_ref, wpa_ref, wpb_ref, wpc_ref,
                  wout_ref, xo_ref, na_ref, nb_ref, nv_ref, fa, fz, sc, sh, ca, wb, *, nb, tt, nv_rows):
    i = pl.program_id(1)
    rows = nb * tt

    @pl.when(i == 0)
    def _():
        fa[:, A_HIST - (A_WIDTH - 1):A_HIST, :] = pa_ref[...]
        fz[:, B_HIST - (B_WIDTH - 1):B_HIST, :] = pb_ref[...]

    x = x_ref[...]
    hb = _rms(x, g_ref[...]).astype(BF16)

    def proj(c0, c1):
        return _dot(hb, win_ref[:, c0:c1])

    a = proj(0, D_BR) * _sigmoid(proj(D_BR, 2 * D_BR))
    fa[:, A_HIST:A_HIST + tt, :] = a.reshape(nb, tt, D_BR)
    off_a = A_HIST - (A_WIDTH - 1)
    for j in range(SUBLANES):
        span = tt + SUBLANES * ((A_WIDTH - 1 - j) // SUBLANES)
        sh[j, :, 0:span, :] = fa[:, off_a + j:off_a + j + span, :]
    for k in range(A_WIDTH):
        wb[k] = jnp.broadcast_to(caw_ref[k:k + 1, :], (SUBLANES, D_BR))
    rb = min(tt, CONV_ROWS)
    sb = CONV_ROWS // rb
    n_rb = tt // rb

    def conv_step(u, carry):
        s0 = (u // n_rb) * sb
        r0 = (u % n_rb) * rb
        groups = range(0, rb, SUBLANES)
        acc = [jnp.broadcast_to(cab_ref[...], (sb, SUBLANES, D_BR)) for _ in groups]
        for k in range(A_WIDTH):
            q, j = divmod(k, SUBLANES)
            w_k = wb[k]
            for gi, g in enumerate(groups):
                lo = r0 + g + SUBLANES * q
                acc[gi] = acc[gi] + w_k * sh[j, s0:s0 + sb, lo:lo + SUBLANES, :]
        for gi, g in enumerate(groups):
            ca[s0:s0 + sb, r0 + g:r0 + g + SUBLANES, :] = acc[gi]
        return carry

    for u in range((nb // sb) * n_rb):
        conv_step(u, 0)
    a_ln = _layer_norm(ca[...].reshape(rows, D_BR), lag_ref[...], lab_ref[...])
    a_out = _dot((a_ln * _sigmoid(a_ln)).astype(BF16), wpa_ref[...])
    na_ref[...] = fa[:, tt + A_HIST - (A_WIDTH - 1):tt + A_HIST, :]
    fa[:, 0:A_HIST, :] = fa[:, tt:tt + A_HIST, :]

    c0 = 2 * D_BR
    z = proj(c0 + 2 * D_BR, c0 + 3 * D_BR) * proj(c0, c0 + D_BR)
    fz[:, B_HIST:B_HIST + tt, :] = z.reshape(nb, tt, D_BR)
    off_b = B_HIST - (B_WIDTH - 1)
    accb = jnp.zeros((nb, tt, D_BR), F32)
    for k in range(B_WIDTH):
        accb = accb + cbw_ref[k:k + 1, :] * fz[:, off_b + k:off_b + k + tt, :]
    b_out = _dot((proj(c0 + D_BR, c0 + 2 * D_BR) * accb.reshape(rows, D_BR)).astype(BF16),
                 wpb_ref[...])
    nb_ref[...] = fz[:, tt + B_HIST - (B_WIDTH - 1):tt + B_HIST, :]
    fz[:, 0:B_HIST, :] = fz[:, tt:tt + B_HIST, :]

    c0 = 5 * D_BR
    v = _layer_norm(proj(c0 + D_BR, c0 + 2 * D_BR), lcg_ref[...], lcb_ref[...])
    nv_ref[...] = v[rows - nv_rows * nb:, :].reshape(nb, nv_rows, D_BR)
    vb = v.astype(BF16)
    t_idx = lax.broadcasted_iota(jnp.int32, (CHUNK, CHUNK), 0)
    s_idx = lax.broadcasted_iota(jnp.int32, (CHUNK, CHUNK), 1)
    causal = s_idx <= t_idx
    for hd in range(C_HEADS):
        wm = jnp.where(causal, wsp_ref[hd], 0.0).astype(BF16)
        for c in range(rows // CHUNK):
            blk = (slice(c * CHUNK, (c + 1) * CHUNK), slice(hd * LANES, (hd + 1) * LANES))
            sc[blk] = _dot(wm, vb[blk])
    c_out = _dot((proj(c0, c0 + D_BR) * (sc[...].reshape(rows // CHUNK, CHUNK, D_BR)
                                        + bsp_ref[...]).reshape(rows, D_BR)).astype(BF16),
                 wpc_ref[...])

    c0 = 7 * D_BR
    merged = (_sigmoid(proj(c0, c0 + D_MODEL)) * a_out
              + _sigmoid(proj(c0 + D_MODEL, c0 + 2 * D_MODEL)) * b_out
              + _sigmoid(proj(c0 + 2 * D_MODEL, c0 + 3 * D_MODEL)) * c_out)
    xo_ref[...] = x + _dot(merged.astype(BF16), wout_ref[...])


def _const_spec(shape):
    nd = len(shape)
    return pl.BlockSpec(shape, lambda b, i: (0,) * nd, pipeline_mode=pl.Buffered(1))


def _drop_refs(kernel_fn, pos, count):
    def body(*refs):
        return kernel_fn(*refs[:pos], *refs[pos + count:])
    return body


def _mixer(x, past_a, past_b, w, *, t, nb, tt, in_row, out_row, out_rows, layer, depth,
           past_layer, out_buf=None, state_bufs=None):
    n = past_a.shape[1]
    rows = nb * tt
    assert n % nb == 0 and t % tt == 0 and rows % CHUNK == 0
    assert in_row % rows == 0 and out_row % rows == 0
    assert tt % SUBLANES == 0 and (tt % CONV_ROWS == 0 or
                                   (CONV_ROWS % tt == 0 and nb % (CONV_ROWS // tt) == 0))
    nv_rows = t - ((t - 1) // CHUNK) * CHUNK
    assert nv_rows <= tt
    n_t = t // tt
    past_spec = lambda r: pl.BlockSpec((None, nb, r, D_BR), lambda b, i: (past_layer, b, 0, 0))
    row_spec = lambda first: pl.BlockSpec((rows, D_MODEL),
                                          lambda b, i: (first // rows + b * n_t + i, 0))
    const_spec = lambda c, stacked: (
        pl.BlockSpec((None,) + c.shape[1:], lambda b, i: (layer, 0, 0),
                     pipeline_mode=pl.Buffered(1)) if stacked else _const_spec(c.shape))
    stacked = ("w_in", "w_proj_a", "w_proj_b", "w_proj_c", "w_out")
    names = ("g", "w_in", "conv_a_w", "conv_a_b", "ln_a_g", "ln_a_b", "conv_b_w", "ln_c_g",
             "ln_c_b", "wsp", "bsp", "w_proj_a", "w_proj_b", "w_proj_c", "w_out")
    body = functools.partial(_mixer_kernel, nb=nb, tt=tt, nv_rows=nv_rows)
    operands = [x, past_a, past_b] + [w[k] for k in names]
    in_specs = ([row_spec(in_row), past_spec(A_WIDTH - 1), past_spec(B_WIDTH - 1)]
                + [const_spec(w[k], k in stacked) for k in names])
    state_rows = (A_WIDTH - 1, B_WIDTH - 1, nv_rows)
    out_shape = [jax.ShapeDtypeStruct((out_rows, D_MODEL), F32)] + [
        jax.ShapeDtypeStruct((depth, n, r, D_BR), F32) for r in state_rows]
    n_in = len(operands)
    aliases = {}
    for k, carrier in enumerate((out_buf, *(state_bufs or (None,) * 3))):
        if carrier is not None:
            assert carrier.shape == out_shape[k].shape
            aliases[len(operands)] = k
            operands.append(carrier)
            in_specs.append(pl.BlockSpec(memory_space=pl.ANY))
    body = _drop_refs(body, n_in, len(operands) - n_in)
    layer_spec = lambda r: pl.BlockSpec((None, nb, r, D_BR), lambda b, i: (layer, b, 0, 0))
    return pl.pallas_call(
        body,
        grid=(n // nb, n_t),
        in_specs=in_specs,
        out_specs=[row_spec(out_row)] + [layer_spec(r) for r in state_rows],
        input_output_aliases=aliases,
        out_shape=out_shape,
        scratch_shapes=[pltpu.VMEM((nb, A_HIST + tt, D_BR), F32),
                        pltpu.VMEM((nb, B_HIST + tt, D_BR), F32),
                        pltpu.VMEM((nb * tt, D_BR), F32),
                        pltpu.VMEM((SUBLANES, nb, tt + A_HIST - SUBLANES, D_BR), F32),
                        pltpu.VMEM((nb, tt, D_BR), F32),
                        pltpu.VMEM((A_WIDTH, SUBLANES, D_BR), F32)],
        compiler_params=pltpu.CompilerParams(
            dimension_semantics=("arbitrary", "arbitrary"), vmem_limit_bytes=VMEM_LIMIT),
        name="mixer",
    )(*operands)


def _swiglu_part(h_b, wg_ref, wu_ref, wd_ref):
    gate = _dot(h_b, wg_ref[...])
    act = gate * _sigmoid(gate) * _dot(h_b, wu_ref[...])
    return _dot(act.astype(BF16), wd_ref[...])


def _ffn_kernel(x_ref, g_ref, wg_hbm, wu_hbm, wd_hbm, o_ref, wg_ref, wu_ref, wd_ref, stage, sem,
                *, tf, which):
    @pl.when(pl.program_id(0) == 0)
    def _():
        _load_expert_weights(which, (wg_hbm, wu_hbm, wd_hbm), (wg_ref, wu_ref, wd_ref), stage, sem)

    x = x_ref[...]
    h_b = _rms(x, g_ref[...]).astype(BF16)
    y = x
    for c0 in range(0, D_FF, tf):
        y = y + _swiglu_part(h_b, wg_ref.at[:, c0:c0 + tf], wu_ref.at[:, c0:c0 + tf],
                             wd_ref.at[c0:c0 + tf, :])
    o_ref[...] = y


def _ffn(x, g, wg, wu, wd, *, which, tm, tf):
    m, _ = x.shape
    assert m % tm == 0 and D_FF % tf == 0
    hbm = pl.BlockSpec(memory_space=pl.ANY)
    return pl.pallas_call(
        functools.partial(_ffn_kernel, tf=tf, which=which),
        grid=(m // tm,),
        in_specs=[pl.BlockSpec((tm, D_MODEL), lambda i: (i, 0)),
                  pl.BlockSpec((1, D_MODEL), lambda i: (0, 0)), hbm, hbm, hbm],
        out_specs=pl.BlockSpec((tm, D_MODEL), lambda i: (i, 0)),
        out_shape=jax.ShapeDtypeStruct((m, D_MODEL), F32),
        scratch_shapes=_weight_scratch(),
        compiler_params=pltpu.CompilerParams(
            dimension_semantics=("arbitrary",), vmem_limit_bytes=VMEM_LIMIT),
        name="ffn",
    )(x, g, wg, wu, wd)


SLAB = D_MODEL // LANES


def _to_slab(ref, val, rows):
    for c in range(SLAB):
        ref[pl.ds(c, rows, stride=SLAB), :] = val[:, c * LANES:(c + 1) * LANES]


def _route_kernel(x_ref, g_ref, wr_ref, br_ref, hs_ref, mi_ref, mf_ref, cnt_ref, carry, *, cap):
    i = pl.program_id(0)
    tm = x_ref.shape[0]

    @pl.when(i == 0)
    def _():
        carry[...] = jnp.zeros_like(carry)

    h = _rms(x_ref[...], g_ref[...])
    _to_slab(hs_ref, h, tm)

    col = lax.broadcasted_iota(jnp.int32, (tm, LANES), 1)
    logits = _dot_split(h, wr_ref[...]) + br_ref[...]
    logits = jnp.where(col < N_EXPERTS, logits, -jnp.inf)
    m1 = jnp.max(logits, -1, keepdims=True)
    i1 = jnp.min(jnp.where(logits == m1, col, LANES), -1, keepdims=True)
    rest = jnp.where(col == i1, -jnp.inf, logits)
    m2 = jnp.max(rest, -1, keepdims=True)
    i2 = jnp.min(jnp.where(rest == m2, col, LANES), -1, keepdims=True)
    e2 = jnp.exp(m2 - m1)
    p1 = 1.0 / (1.0 + e2)
    p2 = e2 * p1

    chosen = jnp.logical_or(col == i1, col == i2).astype(F32)
    r_idx = lax.broadcasted_iota(jnp.int32, (tm, tm), 0)
    c_idx = lax.broadcasted_iota(jnp.int32, (tm, tm), 1)
    before = (c_idx < r_idx).astype(BF16)
    rank = carry[...] + _dot(before, chosen.astype(BF16))
    dest = col.astype(F32) * float(cap) + rank
    d1 = jnp.sum(jnp.where(col == i1, dest, 0.0), -1, keepdims=True)
    d2 = jnp.sum(jnp.where(col == i2, dest, 0.0), -1, keepdims=True)
    by_choice = jnp.where(col == 0, d1, jnp.where(col == 1, d2, 0.0)).T
    mi_ref[...] = by_choice[0:SUBLANES, :].astype(jnp.int32)
    mf_ref[...] = jnp.where(col == 0, p1, jnp.where(col == 1, p2, 0.0))
    carry[...] += jnp.sum(chosen, 0, keepdims=True)
    cnt_ref[...] = carry[...]


def _route(x, g, wr, br, *, tm, cap):
    m = x.shape[0]
    assert m % tm == 0
    row = lambda i: (i, 0)
    const = lambda i: (0, 0)
    return pl.pallas_call(
        functools.partial(_route_kernel, cap=cap),
        grid=(m // tm,),
        in_specs=[pl.BlockSpec((tm, D_MODEL), row), pl.BlockSpec((1, D_MODEL), const),
                  pl.BlockSpec((D_MODEL, LANES), const), pl.BlockSpec((1, LANES), const)],
        out_specs=[pl.BlockSpec((tm * SLAB, LANES), row),
                   pl.BlockSpec((SUBLANES, tm), lambda i: (0, i)),
                   pl.BlockSpec((tm, LANES), row), pl.BlockSpec((1, LANES), const)],
        out_shape=[jax.ShapeDtypeStruct((m * SLAB, LANES), F32),
                   jax.ShapeDtypeStruct((SUBLANES, m), jnp.int32),
                   jax.ShapeDtypeStruct((m, LANES), F32),
                   jax.ShapeDtypeStruct((1, LANES), F32)],
        scratch_shapes=[pltpu.VMEM((1, LANES), F32)],
        compiler_params=pltpu.CompilerParams(dimension_semantics=("arbitrary",),
                                             vmem_limit_bytes=VMEM_LIMIT),
        name="route",
    )(x, g, wr, br)


def _row_copy(src_ref, src_row, dst_ref, dst_row, sem):
    return pltpu.make_async_copy(
        src_ref.at[pl.ds(pl.multiple_of(src_row * SLAB, SLAB), SLAB), :],
        dst_ref.at[pl.ds(pl.multiple_of(dst_row * SLAB, SLAB), SLAB), :], sem)


def _wait_rows(src_ref, dst_ref, n_rows, sem):
    pltpu.make_async_copy(src_ref.at[pl.ds(0, n_rows * SLAB), :],
                          dst_ref.at[pl.ds(0, n_rows * SLAB), :], sem).wait()


def _dispatch_kernel(idx1_ref, idx2_ref, src_ref, dst_ref, sem, *, rows):
    def issue(r, c):
        for k, idx_ref in enumerate((idx1_ref, idx2_ref)):
            _row_copy(src_ref, r, dst_ref, idx_ref[0, 0, r], sem).start(priority=k)
        return c

    lax.fori_loop(0, rows, issue, 0)
    for k in range(2):
        _wait_rows(src_ref, dst_ref, rows, sem)


def _idx_spec(rows, index_map):
    return pl.BlockSpec((1, 1, rows), index_map, memory_space=pltpu.SMEM)


def _dispatch(h_slab, dest1, dest2, *, rows, cap_rows):
    n_blk = dest1.shape[0] // rows
    blocked = lambda d: d.reshape(n_blk, 1, rows)
    return pl.pallas_call(
        functools.partial(_dispatch_kernel, rows=rows),
        grid=(n_blk,),
        in_specs=[_idx_spec(rows, lambda i: (i, 0, 0)), _idx_spec(rows, lambda i: (i, 0, 0)),
                  pl.BlockSpec((rows * SLAB, LANES), lambda i: (i, 0))],
        out_specs=pl.BlockSpec(memory_space=pl.ANY),
        out_shape=jax.ShapeDtypeStruct((cap_rows * SLAB, LANES), F32),
        scratch_shapes=[pltpu.SemaphoreType.DMA(())],
        compiler_params=pltpu.CompilerParams(dimension_semantics=("arbitrary",)),
        name="dispatch",
    )(blocked(dest1), blocked(dest2), h_slab)


W_ROWS, W_COLS = 256, 512
W_SLOTS = 32


def _load_expert_weights(e, srcs, dsts, stage, sem, phase="all"):
    jobs = []
    for src, dst in zip(srcs, dsts):
        n_r, n_c = dst.shape
        for r0 in range(0, n_r, W_ROWS):
            for c0 in range(0, n_c, W_COLS):
                jobs.append((src.at[e, pl.ds(r0, W_ROWS), pl.ds(c0, W_COLS)],
                             dst.at[r0:r0 + W_ROWS, c0:c0 + W_COLS]))
    copies = [pltpu.make_async_copy(src, stage.at[i % W_SLOTS], sem.at[i % W_SLOTS])
              for i, (src, _) in enumerate(jobs)]
    ahead = W_SLOTS - 1
    if phase != "finish":
        for cp in copies[:ahead]:
            cp.start()
    if phase == "start":
        return
    for i, (cp, (_, dst)) in enumerate(zip(copies, jobs)):
        cp.wait()
        dst[...] = stage[i % W_SLOTS].astype(BF16)
        if i + ahead < len(copies):
            copies[i + ahead].start()


def _weight_scratch():
    assert D_MODEL % W_ROWS == 0 and D_FF % W_ROWS == 0
    assert D_MODEL % W_COLS == 0 and D_FF % W_COLS == 0
    return [pltpu.VMEM((D_MODEL, D_FF), BF16), pltpu.VMEM((D_MODEL, D_FF), BF16),
            pltpu.VMEM((D_FF, D_MODEL), BF16), pltpu.VMEM((W_SLOTS, W_ROWS, W_COLS), F32),
            pltpu.SemaphoreType.DMA((W_SLOTS,))]


def _experts_kernel(tblk, texp, tn, tnew, tpre, xs_ref, wg_hbm, wu_hbm, wd_hbm, o_ref,
                    hb, wg_ref, wu_ref, wd_ref, stage, sem, *, tf):
    t = pl.program_id(0)
    n = tn[t]
    tm = hb.shape[0]
    load = functools.partial(_load_expert_weights, srcs=(wg_hbm, wu_hbm, wd_hbm),
                             dsts=(wg_ref, wu_ref, wd_ref), stage=stage, sem=sem)

    pl.when(t == 0)(lambda: load(texp[0], phase="start"))
    pl.when(tnew[t] == 1)(lambda: load(texp[t], phase="finish"))
    pl.when(tpre[t] >= 0)(lambda: load(tpre[t], phase="start"))

    def run(rows):
        valid = lax.broadcasted_iota(jnp.int32, (rows, LANES), 0) < n
        for c in range(SLAB):
            xc = xs_ref[pl.ds(c, rows, stride=SLAB), :]
            hb[0:rows, c * LANES:(c + 1) * LANES] = jnp.where(valid, xc, 0.0).astype(BF16)
        h_b = hb[0:rows, :]
        y = _swiglu_part(h_b, wg_ref.at[:, 0:tf], wu_ref.at[:, 0:tf], wd_ref.at[0:tf, :])
        for c0 in range(tf, D_FF, tf):
            y = y + _swiglu_part(h_b, wg_ref.at[:, c0:c0 + tf], wu_ref.at[:, c0:c0 + tf],
                                 wd_ref.at[c0:c0 + tf, :])
        _to_slab(o_ref, y, rows)

    pl.when(n > tm // 2)(lambda: run(tm))
    pl.when(jnp.logical_and(n > 0, n <= tm // 2))(lambda: run(tm // 2))


def _experts(xs_slab, tblk, texp, tn, tnew, tpre, wg, wu, wd, *, tm, tf):
    n_tiles = tblk.shape[0]
    assert D_FF % tf == 0 and tm % (2 * SUBLANES) == 0
    tile = lambda t, tblk, texp, tn, tnew, tpre: (tblk[t], 0)
    hbm = pl.BlockSpec(memory_space=pl.ANY)
    grid_spec = pltpu.PrefetchScalarGridSpec(
        num_scalar_prefetch=5,
        grid=(n_tiles,),
        in_specs=[pl.BlockSpec((tm * SLAB, LANES), tile), hbm, hbm, hbm],
        out_specs=pl.BlockSpec((tm * SLAB, LANES), tile),
        scratch_shapes=[pltpu.VMEM((tm, D_MODEL), BF16)] + _weight_scratch())
    return pl.pallas_call(
        functools.partial(_experts_kernel, tf=tf),
        grid_spec=grid_spec,
        out_shape=jax.ShapeDtypeStruct(xs_slab.shape, F32),
        compiler_params=pltpu.CompilerParams(dimension_semantics=("arbitrary",),
                                             vmem_limit_bytes=VMEM_LIMIT),
        name="experts",
    )(tblk, texp, tn, tnew, tpre, xs_slab, wg, wu, wd)


def _combine_kernel(idx1_ref, idx2_ref, idx1_next_ref, idx2_next_ref, x_ref, mf_ref, gf_ref,
                    ys_ref, op_ref, os_ref, buf, ysum, sem, *, rows, n_prompt_tiles):
    i = pl.program_id(0)
    n_i = pl.num_programs(0)
    slot = i % 2

    def issue(refs, s):
        def body(r, c):
            for k, ref in enumerate(refs):
                _row_copy(ys_ref, ref[0, 0, r], buf.at[s], k * rows + r,
                          sem.at[s]).start(priority=k)
            return c
        lax.fori_loop(0, rows, body, 0, unroll=4)

    def finish(s):
        _wait_rows(ys_ref, buf.at[s], 2 * rows, sem.at[s])
        p1 = mf_ref[:, 0:1]
        p2 = mf_ref[:, 1:2]
        for c in range(SLAB):
            cs = slice(c * LANES, (c + 1) * LANES)
            y1 = buf[s, pl.ds(c, rows, stride=SLAB), :]
            y2 = buf[s, pl.ds(rows * SLAB + c, rows, stride=SLAB), :]
            ysum[:, cs] = x_ref[:, cs] + p1 * y1 + p2 * y2
        res = _rms(ysum[...], gf_ref[...])

        @pl.when(i < n_prompt_tiles)
        def _():
            op_ref[...] = res

        @pl.when(i >= n_prompt_tiles)
        def _():
            os_ref[...] = res

    for s in range(2):
        @pl.when(slot == s)
        def _():
            @pl.when(i == 0)
            def _():
                issue((idx1_ref, idx2_ref), s)

            @pl.when(i + 1 < n_i)
            def _():
                issue((idx1_next_ref, idx2_next_ref), 1 - s)

            finish(s)


def _combine(x, mf, g_final, y_slab, dest1, dest2, *, rows, m_prompt):
    m = x.shape[0]
    n_blk = m // rows
    assert m == n_blk * rows and m_prompt % rows == 0
    n_p = m_prompt // rows
    blocked = lambda d: d.reshape(n_blk, 1, rows)
    this = lambda i: (i, 0, 0)
    ahead = lambda i: (jnp.minimum(i + 1, n_blk - 1), 0, 0)
    return pl.pallas_call(
        functools.partial(_combine_kernel, rows=rows, n_prompt_tiles=n_p),
        grid=(n_blk,),
        in_specs=[_idx_spec(rows, this), _idx_spec(rows, this),
                  _idx_spec(rows, ahead), _idx_spec(rows, ahead),
                  pl.BlockSpec((rows, D_MODEL), lambda i: (i, 0)),
                  pl.BlockSpec((rows, LANES), lambda i: (i, 0)),
                  pl.BlockSpec((1, D_MODEL), lambda i: (0, 0)),
                  pl.BlockSpec(memory_space=pl.ANY)],
        out_specs=[pl.BlockSpec((rows, D_MODEL), lambda i: (jnp.minimum(i, n_p - 1), 0)),
                   pl.BlockSpec((rows, D_MODEL), lambda i: (jnp.maximum(i - n_p, 0), 0))],
        out_shape=[jax.ShapeDtypeStruct((m_prompt, D_MODEL), F32),
                   jax.ShapeDtypeStruct((m - m_prompt, D_MODEL), F32)],
        scratch_shapes=[pltpu.VMEM((2, 2 * rows * SLAB, LANES), F32),
                        pltpu.VMEM((rows, D_MODEL), F32),
                        pltpu.SemaphoreType.DMA((2,))],
        compiler_params=pltpu.CompilerParams(dimension_semantics=("arbitrary",),
                                             vmem_limit_bytes=VMEM_LIMIT),
        name="combine",
    )(blocked(dest1), blocked(dest2), blocked(dest1), blocked(dest2), x, mf, g_final, y_slab)


def _expert_tiles(counts, *, tm, cap, n_tiles):
    tiles = (counts + tm - 1) // tm
    ends = jnp.cumsum(tiles)
    t = jnp.arange(n_tiles, dtype=jnp.int32)
    tc = jnp.minimum(t, ends[-1] - 1)
    e = jnp.sum((tc[:, None] >= ends[None, :]).astype(jnp.int32), axis=1)
    k = tc - (ends - tiles)[e]
    tblk = e * (cap // tm) + k
    tn = jnp.where(t < ends[-1], jnp.minimum(counts[e] - k * tm, tm), 0)
    tnew = jnp.logical_and(t < ends[-1], k == 0)
    i32 = lambda v: v.astype(jnp.int32)
    nxt = jnp.minimum(t + 1, n_tiles - 1)
    tpre = jnp.where(jnp.logical_and(t + 1 < n_tiles, tnew[nxt]), e[nxt], -1)
    return i32(tblk), i32(e), i32(tn), i32(tnew), i32(tpre)


def _moe_final(x, g_ffn, wr, br, wg, wu, wd, g_final, *, m_prompt):
    m = x.shape[0]
    tm, tf, out_rows, in_rows = MLP_ROWS, MLP_COLS, DISPATCH_ROWS, COMBINE_ROWS
    cap = -(-m // tm) * tm
    h_slab, mi, mf, cnt = _route(x, g_ffn, wr, br, tm=ROUTE_ROWS, cap=cap)
    dest1, dest2 = mi[0], mi[1]
    counts = cnt[0, :N_EXPERTS].astype(jnp.int32)
    n_tiles = 2 * m // tm + N_EXPERTS
    tables = _expert_tiles(counts, tm=tm, cap=cap, n_tiles=n_tiles)
    xs_slab = _dispatch(h_slab, dest1, dest2, rows=out_rows, cap_rows=N_EXPERTS * cap)
    y_slab = _experts(xs_slab, *tables, wg, wu, wd, tm=tm, tf=tf)
    return _combine(x, mf, g_final, y_slab, dest1, dest2, rows=in_rows, m_prompt=m_prompt)


def _row(v):
    return v.reshape(1, -1)


def kernel(x_prompt, x_sample, state_conv_a, state_conv_b, norm_mix_g, w_in, conv_a_w, conv_a_b, ln_a_g, ln_a_b, conv_b_w, ln_c_g, ln_c_b, w_spatial, b_spatial, w_proj_a, w_proj_b, w_proj_c, w_out, norm_ffn_g, ffn_w_gate, ffn_w_up, ffn_w_down, w_router, b_router, moe_w_gate, moe_w_up, moe_w_down, norm_final_g):
    depth = w_in.shape[0]
    n_p, t_p, _ = x_prompt.shape
    n_s, t_s, _ = x_sample.shape
    seqs_per_chunk = CHUNK // t_s

    zeros_a = jnp.zeros((1, n_p, A_WIDTH - 1, D_BR), F32)
    zeros_b = jnp.zeros((1, n_p, B_WIDTH - 1, D_BR), F32)
    matrices = dict(w_in=w_in.astype(BF16), w_proj_a=w_proj_a.astype(BF16),
                    w_proj_b=w_proj_b.astype(BF16), w_proj_c=w_proj_c.astype(BF16),
                    w_out=w_out.astype(BF16))

    m_p, m_s = n_p * t_p, n_s * t_s
    m = m_p + m_s
    x_p, row_p = x_prompt.reshape(m_p, D_MODEL), 0
    x_s, row_s = x_sample.reshape(m_s, D_MODEL), 0
    states_p = states_s = None
    for l in range(depth):
        bsp_p = jnp.repeat(b_spatial[l].T, LANES, axis=1)
        eye = jnp.eye(seqs_per_chunk, dtype=F32)
        wsp_s = jnp.einsum("ab,hts->hatbs", eye, w_spatial[l][:, :t_s, :t_s]).reshape(
            C_HEADS, CHUNK, CHUNK)
        bsp_s = jnp.tile(bsp_p[:t_s], (seqs_per_chunk, 1))
        w = dict(matrices, g=_row(norm_mix_g[l]), conv_a_w=conv_a_w[l],
                 conv_a_b=_row(conv_a_b[l]), ln_a_g=_row(ln_a_g[l]), ln_a_b=_row(ln_a_b[l]),
                 conv_b_w=conv_b_w[l], ln_c_g=_row(ln_c_g[l]), ln_c_b=_row(ln_c_b[l]))
        x, *states_p = _mixer(x_p, zeros_a, zeros_b, dict(w, wsp=w_spatial[l], bsp=bsp_p),
                              t=t_p, nb=1, tt=MIXER_ROWS, in_row=row_p, out_row=0, out_rows=m,
                              layer=l, depth=depth, past_layer=0, state_bufs=states_p)
        x, *states_s = _mixer(x_s, state_conv_a, state_conv_b,
                              dict(w, wsp=wsp_s, bsp=bsp_s), t=t_s, nb=SAMPLE_SEQS, tt=t_s,
                              in_row=row_s, out_row=m_p, out_rows=m, layer=l, depth=depth,
                              past_layer=l, out_buf=x, state_bufs=states_s)

        i = l // 2
        g_ffn = _row(norm_ffn_g[l])
        if l % 2 == 0:
            x = _ffn(x, g_ffn, ffn_w_gate, ffn_w_up, ffn_w_down, which=i,
                     tm=MLP_ROWS, tf=MLP_COLS)
            x_p, row_p, x_s, row_s = x, 0, x, m_p
        else:
            assert l == depth - 1, "the routed layer applies the final norm"
            wr = jnp.pad(w_router[i], ((0, 0), (0, LANES - N_EXPERTS)))
            br = jnp.pad(_row(b_router[i]), ((0, 0), (0, LANES - N_EXPERTS)))
            y_p, y_s = _moe_final(x, g_ffn, wr, br, moe_w_gate[i], moe_w_up[i], moe_w_down[i],
                                  _row(norm_final_g), m_prompt=m_p)
    y_p = y_p.reshape(n_p, t_p, D_MODEL)
    y_s = y_s.reshape(n_s, t_s, D_MODEL)

    (a_p, b_p, v_p), (a_s, b_s, v_s) = states_p, states_s
    return (y_p, y_s, a_p, a_s, b_p, b_s, v_p, v_s)
```

```python
import functools

import jax
import jax.numpy as jnp
from jax import lax
from jax.experimental import pallas as pl
from jax.experimental.pallas import tpu as pltpu

D_MODEL = 1024
D_BR = 512
A_WIDTH = 31
B_WIDTH = 3
C_HEADS = 4
CHUNK = 128
D_FF = 3584
N_EXPERTS = 8
EPS = 1e-6

A_HIST = 32
B_HIST = 8
LANES = 128
SUBLANES = 8
CONV_ROWS = 32
VMEM_LIMIT = 56 * 1024 * 1024

MIXER_ROWS = 512
SAMPLE_SEQS = 32
MLP_ROWS = 512
MLP_COLS = 1792
ROUTE_ROWS = 512
DISPATCH_ROWS = 1024
COMBINE_ROWS = 512

BF16 = jnp.bfloat16
F32 = jnp.float32


def _rms(x, g):
    return x * lax.rsqrt(jnp.mean(x * x, -1, keepdims=True) + EPS) * g


def _layer_norm(x, g, b):
    mu = jnp.mean(x, -1, keepdims=True)
    xc = x - mu
    var = jnp.mean(xc * xc, -1, keepdims=True)
    return xc * lax.rsqrt(var + EPS) * g + b


def _sigmoid(x):
    return 0.5 * jnp.tanh(0.5 * x) + 0.5


def _dot(a, b):
    return jnp.dot(a, b, preferred_element_type=F32)


def _dot_split(a, b):
    a_hi = a.astype(BF16)
    b_hi = b.astype(BF16)
    a_lo = (a - a_hi.astype(F32)).astype(BF16)
    b_lo = (b - b_hi.astype(F32)).astype(BF16)
    return _dot(a_hi, b_hi) + (_dot(a_hi, b_lo) + _dot(a_lo, b_hi))


def _mixer_kernel(x_ref, pa_ref, pb_ref, g_ref, win_ref, caw_ref, cab_ref, lag_ref, lab_ref,
                  cbw_ref, lcg_ref, lcb_ref, wsp_ref, bsp_ref, wpa_ref, wpb_ref, wpc_ref,
                  wout_ref, xo_ref, na_ref, nb_ref, nv_ref, fa, fz, sc, sh, ca, wb, *, nb, tt, nv_rows):
    i = pl.program_id(1)
    rows = nb * tt

    @pl.when(i == 0)
    def _():
        fa[:, A_HIST - (A_WIDTH - 1):A_HIST, :] = pa_ref[...]
        fz[:, B_HIST - (B_WIDTH - 1):B_HIST, :] = pb_ref[...]

    x = x_ref[...]
    hb = _rms(x, g_ref[...]).astype(BF16)

    def proj(c0, c1):
        return _dot(hb, win_ref[:, c0:c1])

    a = proj(0, D_BR) * _sigmoid(proj(D_BR, 2 * D_BR))
    fa[:, A_HIST:A_HIST + tt, :] = a.reshape(nb, tt, D_BR)
    off_a = A_HIST - (A_WIDTH - 1)
    for j in range(SUBLANES):
        span = tt + SUBLANES * ((A_WIDTH - 1 - j) // SUBLANES)
        sh[j, :, 0:span, :] = fa[:, off_a + j:off_a + j + span, :]
    for k in range(A_WIDTH):
        wb[k] = jnp.broadcast_to(caw_ref[k:k + 1, :], (SUBLANES, D_BR))
    rb = min(tt, CONV_ROWS)
    sb = CONV_ROWS // rb
    n_rb = tt // rb

    def conv_step(u, carry):
        s0 = (u // n_rb) * sb
        r0 = (u % n_rb) * rb
        groups = range(0, rb, SUBLANES)
        acc = [jnp.broadcast_to(cab_ref[...], (sb, SUBLANES, D_BR)) for _ in groups]
        for k in range(A_WIDTH):
            q, j = divmod(k, SUBLANES)
            w_k = wb[k]
            for gi, g in enumerate(groups):
                lo = r0 + g + SUBLANES * q
                acc[gi] = acc[gi] + w_k * sh[j, s0:s0 + sb, lo:lo + SUBLANES, :]
        for gi, g in enumerate(groups):
            ca[s0:s0 + sb, r0 + g:r0 + g + SUBLANES, :] = acc[gi]
        return carry

    for u in range((nb // sb) * n_rb):
        conv_step(u, 0)
    a_ln = _layer_norm(ca[...].reshape(rows, D_BR), lag_ref[...], lab_ref[...])
    a_out = _dot((a_ln * _sigmoid(a_ln)).astype(BF16), wpa_ref[...])
    na_ref[...] = fa[:, tt + A_HIST - (A_WIDTH - 1):tt + A_HIST, :]
    fa[:, 0:A_HIST, :] = fa[:, tt:tt + A_HIST, :]

    c0 = 2 * D_BR
    z = proj(c0 + 2 * D_BR, c0 + 3 * D_BR) * proj(c0, c0 + D_BR)
    fz[:, B_HIST:B_HIST + tt, :] = z.reshape(nb, tt, D_BR)
    off_b = B_HIST - (B_WIDTH - 1)
    accb = jnp.zeros((nb, tt, D_BR), F32)
    for k in range(B_WIDTH):
        accb = accb + cbw_ref[k:k + 1, :] * fz[:, off_b + k:off_b + k + tt, :]
    b_out = _dot((proj(c0 + D_BR, c0 + 2 * D_BR) * accb.reshape(rows, D_BR)).astype(BF16),
                 wpb_ref[...])
    nb_ref[...] = fz[:, tt + B_HIST - (B_WIDTH - 1):tt + B_HIST, :]
    fz[:, 0:B_HIST, :] = fz[:, tt:tt + B_HIST, :]

    c0 = 5 * D_BR
    v = _layer_norm(proj(c0 + D_BR, c0 + 2 * D_BR), lcg_ref[...], lcb_ref[...])
    nv_ref[...] = v[rows - nv_rows * nb:, :].reshape(nb, nv_rows, D_BR)
    vb = v.astype(BF16)
    t_idx = lax.broadcasted_iota(jnp.int32, (CHUNK, CHUNK), 0)
    s_idx = lax.broadcasted_iota(jnp.int32, (CHUNK, CHUNK), 1)
    causal = s_idx <= t_idx
    for hd in range(C_HEADS):
        wm = jnp.where(causal, wsp_ref[hd], 0.0).astype(BF16)
        for c in range(rows // CHUNK):
            blk = (slice(c * CHUNK, (c + 1) * CHUNK), slice(hd * LANES, (hd + 1) * LANES))
            sc[blk] = _dot(wm, vb[blk])
    c_out = _dot((proj(c0, c0 + D_BR) * (sc[...].reshape(rows // CHUNK, CHUNK, D_BR)
                                        + bsp_ref[...]).reshape(rows, D_BR)).astype(BF16),
                 wpc_ref[...])

    c0 = 7 * D_BR
    merged = (_sigmoid(proj(c0, c0 + D_MODEL)) * a_out
              + _sigmoid(proj(c0 + D_MODEL, c0 + 2 * D_MODEL)) * b_out
              + _sigmoid(proj(c0 + 2 * D_MODEL, c0 + 3 * D_MODEL)) * c_out)
    xo_ref[...] = x + _dot(merged.astype(BF16), wout_ref[...])


def _const_spec(shape):
    nd = len(shape)
    return pl.BlockSpec(shape, lambda b, i: (0,) * nd, pipeline_mode=pl.Buffered(1))


def _drop_refs(kernel_fn, pos, count):
    def body(*refs):
        return kernel_fn(*refs[:pos], *refs[pos + count:])
    return body


def _mixer(x, past_a, past_b, w, *, t, nb, tt, in_row, out_row, out_rows, layer, depth,
           past_layer, out_buf=None, state_bufs=None):
    n = past_a.shape[1]
    rows = nb * tt
    assert n % nb == 0 and t % tt == 0 and rows % CHUNK == 0
    assert in_row % rows == 0 and out_row % rows == 0
    assert tt % SUBLANES == 0 and (tt % CONV_ROWS == 0 or
                                   (CONV_ROWS % tt == 0 and nb % (CONV_ROWS // tt) == 0))
    nv_rows = t - ((t - 1) // CHUNK) * CHUNK
    assert nv_rows <= tt
    n_t = t // tt
    past_spec = lambda r: pl.BlockSpec((None, nb, r, D_BR), lambda b, i: (past_layer, b, 0, 0))
    row_spec = lambda first: pl.BlockSpec((rows, D_MODEL),
                                          lambda b, i: (first // rows + b * n_t + i, 0))
    const_spec = lambda c, stacked: (
        pl.BlockSpec((None,) + c.shape[1:], lambda b, i: (layer, 0, 0),
                     pipeline_mode=pl.Buffered(1)) if stacked else _const_spec(c.shape))
    stacked = ("w_in", "w_proj_a", "w_proj_b", "w_proj_c", "w_out")
    names = ("g", "w_in", "conv_a_w", "conv_a_b", "ln_a_g", "ln_a_b", "conv_b_w", "ln_c_g",
             "ln_c_b", "wsp", "bsp", "w_proj_a", "w_proj_b", "w_proj_c", "w_out")
    body = functools.partial(_mixer_kernel, nb=nb, tt=tt, nv_rows=nv_rows)
    operands = [x, past_a, past_b] + [w[k] for k in names]
    in_specs = ([row_spec(in_row), past_spec(A_WIDTH - 1), past_spec(B_WIDTH - 1)]
                + [const_spec(w[k], k in stacked) for k in names])
    state_rows = (A_WIDTH - 1, B_WIDTH - 1, nv_rows)
    out_shape = [jax.ShapeDtypeStruct((out_rows, D_MODEL), F32)] + [
        jax.ShapeDtypeStruct((depth, n, r, D_BR), F32) for r in state_rows]
    n_in = len(operands)
    aliases = {}
    for k, carrier in enumerate((out_buf, *(state_bufs or (None,) * 3))):
        if carrier is not None:
            assert carrier.shape == out_shape[k].shape
            aliases[len(operands)] = k
            operands.append(carrier)
            in_specs.append(pl.BlockSpec(memory_space=pl.ANY))
    body = _drop_refs(body, n_in, len(operands) - n_in)
    layer_spec = lambda r: pl.BlockSpec((None, nb, r, D_BR), lambda b, i: (layer, b, 0, 0))
    return pl.pallas_call(
        body,
        grid=(n // nb, n_t),
        in_specs=in_specs,
        out_specs=[row_spec(out_row)] + [layer_spec(r) for r in state_rows],
        input_output_aliases=aliases,
        out_shape=out_shape,
        scratch_shapes=[pltpu.VMEM((nb, A_HIST + tt, D_BR), F32),
                        pltpu.VMEM((nb, B_HIST + tt, D_BR), F32),
                        pltpu.VMEM((nb * tt, D_BR), F32),
                        pltpu.VMEM((SUBLANES, nb, tt + A_HIST - SUBLANES, D_BR), F32),
                        pltpu.VMEM((nb, tt, D_BR), F32),
                        pltpu.VMEM((A_WIDTH, SUBLANES, D_BR), F32)],
        compiler_params=pltpu.CompilerParams(
            dimension_semantics=("arbitrary", "arbitrary"), vmem_limit_bytes=VMEM_LIMIT),
        name="mixer",
    )(*operands)


def _swiglu_part(h_b, wg_ref, wu_ref, wd_ref):
    gate = _dot(h_b, wg_ref[...])
    act = gate * _sigmoid(gate) * _dot(h_b, wu_ref[...])
    return _dot(act.astype(BF16), wd_ref[...])


def _ffn_kernel(x_ref, g_ref, wg_hbm, wu_hbm, wd_hbm, o_ref, wg_ref, wu_ref, wd_ref, stage, sem,
                *, tf, which):
    @pl.when(pl.program_id(0) == 0)
    def _():
        _load_expert_weights(which, (wg_hbm, wu_hbm, wd_hbm), (wg_ref, wu_ref, wd_ref), stage, sem)

    x = x_ref[...]
    h_b = _rms(x, g_ref[...]).astype(BF16)
    y = x
    for c0 in range(0, D_FF, tf):
        y = y + _swiglu_part(h_b, wg_ref.at[:, c0:c0 + tf], wu_ref.at[:, c0:c0 + tf],
                             wd_ref.at[c0:c0 + tf, :])
    o_ref[...] = y


def _ffn(x, g, wg, wu, wd, *, which, tm, tf):
    m, _ = x.shape
    assert m % tm == 0 and D_FF % tf == 0
    hbm = pl.BlockSpec(memory_space=pl.ANY)
    return pl.pallas_call(
        functools.partial(_ffn_kernel, tf=tf, which=which),
        grid=(m // tm,),
        in_specs=[pl.BlockSpec((tm, D_MODEL), lambda i: (i, 0)),
                  pl.BlockSpec((1, D_MODEL), lambda i: (0, 0)), hbm, hbm, hbm],
        out_specs=pl.BlockSpec((tm, D_MODEL), lambda i: (i, 0)),
        out_shape=jax.ShapeDtypeStruct((m, D_MODEL), F32),
        scratch_shapes=_weight_scratch(),
        compiler_params=pltpu.CompilerParams(
            dimension_semantics=("arbitrary",), vmem_limit_bytes=VMEM_LIMIT),
        name="ffn",
    )(x, g, wg, wu, wd)


SLAB = D_MODEL // LANES


def _to_slab(ref, val, rows):
    for c in range(SLAB):
        ref[pl.ds(c, rows, stride=SLAB), :] = val[:, c * LANES:(c + 1) * LANES]


def _route_kernel(x_ref, g_ref, wr_ref, br_ref, hs_ref, mi_ref, mf_ref, cnt_ref, carry, *, cap):
    i = pl.program_id(0)
    tm = x_ref.shape[0]

    @pl.when(i == 0)
    def _():
        carry[...] = jnp.zeros_like(carry)

    h = _rms(x_ref[...], g_ref[...])
    _to_slab(hs_ref, h, tm)

    col = lax.broadcasted_iota(jnp.int32, (tm, LANES), 1)
    logits = _dot_split(h, wr_ref[...]) + br_ref[...]
    logits = jnp.where(col < N_EXPERTS, logits, -jnp.inf)
    m1 = jnp.max(logits, -1, keepdims=True)
    i1 = jnp.min(jnp.where(logits == m1, col, LANES), -1, keepdims=True)
    rest = jnp.where(col == i1, -jnp.inf, logits)
    m2 = jnp.max(rest, -1, keepdims=True)
    i2 = jnp.min(jnp.where(rest == m2, col, LANES), -1, keepdims=True)
    e2 = jnp.exp(m2 - m1)
    p1 = 1.0 / (1.0 + e2)
    p2 = e2 * p1

    chosen = jnp.logical_or(col == i1, col == i2).astype(F32)
    r_idx = lax.broadcasted_iota(jnp.int32, (tm, tm), 0)
    c_idx = lax.broadcasted_iota(jnp.int32, (tm, tm), 1)
    before = (c_idx < r_idx).astype(BF16)
    rank = carry[...] + _dot(before, chosen.astype(BF16))
    dest = col.astype(F32) * float(cap) + rank
    d1 = jnp.sum(jnp.where(col == i1, dest, 0.0), -1, keepdims=True)
    d2 = jnp.sum(jnp.where(col == i2, dest, 0.0), -1, keepdims=True)
    by_choice = jnp.where(col == 0, d1, jnp.where(col == 1, d2, 0.0)).T
    mi_ref[...] = by_choice[0:SUBLANES, :].astype(jnp.int32)
    mf_ref[...] = jnp.where(col == 0, p1, jnp.where(col == 1, p2, 0.0))
    carry[...] += jnp.sum(chosen, 0, keepdims=True)
    cnt_ref[...] = carry[...]


def _route(x, g, wr, br, *, tm, cap):
    m = x.shape[0]
    assert m % tm == 0
    row = lambda i: (i, 0)
    const = lambda i: (0, 0)
    return pl.pallas_call(
        functools.partial(_route_kernel, cap=cap),
        grid=(m // tm,),
        in_specs=[pl.BlockSpec((tm, D_MODEL), row), pl.BlockSpec((1, D_MODEL), const),
                  pl.BlockSpec((D_MODEL, LANES), const), pl.BlockSpec((1, LANES), const)],
        out_specs=[pl.BlockSpec((tm * SLAB, LANES), row),
                   pl.BlockSpec((SUBLANES, tm), lambda i: (0, i)),
                   pl.BlockSpec((tm, LANES), row), pl.BlockSpec((1, LANES), const)],
        out_shape=[jax.ShapeDtypeStruct((m * SLAB, LANES), F32),
                   jax.ShapeDtypeStruct((SUBLANES, m), jnp.int32),
                   jax.ShapeDtypeStruct((m, LANES), F32),
                   jax.ShapeDtypeStruct((1, LANES), F32)],
        scratch_shapes=[pltpu.VMEM((1, LANES), F32)],
        compiler_params=pltpu.CompilerParams(dimension_semantics=("arbitrary",),
                                             vmem_limit_bytes=VMEM_LIMIT),
        name="route",
    )(x, g, wr, br)


def _row_copy(src_ref, src_row, dst_ref, dst_row, sem):
    return pltpu.make_async_copy(
        src_ref.at[pl.ds(pl.multiple_of(src_row * SLAB, SLAB), SLAB), :],
        dst_ref.at[pl.ds(pl.multiple_of(dst_row * SLAB, SLAB), SLAB), :], sem)


def _wait_rows(src_ref, dst_ref, n_rows, sem):
    pltpu.make_async_copy(src_ref.at[pl.ds(0, n_rows * SLAB), :],
                          dst_ref.at[pl.ds(0, n_rows * SLAB), :], sem).wait()


def _dispatch_kernel(idx1_ref, idx2_ref, src_ref, dst_ref, sem, *, rows):
    def issue(r, c):
        for k, idx_ref in enumerate((idx1_ref, idx2_ref)):
            _row_copy(src_ref, r, dst_ref, idx_ref[0, 0, r], sem).start(priority=k)
        return c

    lax.fori_loop(0, rows, issue, 0)
    for k in range(2):
        _wait_rows(src_ref, dst_ref, rows, sem)


def _idx_spec(rows, index_map):
    return pl.BlockSpec((1, 1, rows), index_map, memory_space=pltpu.SMEM)


def _dispatch(h_slab, dest1, dest2, *, rows, cap_rows):
    n_blk = dest1.shape[0] // rows
    blocked = lambda d: d.reshape(n_blk, 1, rows)
    return pl.pallas_call(
        functools.partial(_dispatch_kernel, rows=rows),
        grid=(n_blk,),
        in_specs=[_idx_spec(rows, lambda i: (i, 0, 0)), _idx_spec(rows, lambda i: (i, 0, 0)),
                  pl.BlockSpec((rows * SLAB, LANES), lambda i: (i, 0))],
        out_specs=pl.BlockSpec(memory_space=pl.ANY),
        out_shape=jax.ShapeDtypeStruct((cap_rows * SLAB, LANES), F32),
        scratch_shapes=[pltpu.SemaphoreType.DMA(())],
        compiler_params=pltpu.CompilerParams(dimension_semantics=("arbitrary",)),
        name="dispatch",
    )(blocked(dest1), blocked(dest2), h_slab)


W_ROWS, W_COLS = 256, 512
W_SLOTS = 32
W_SPLIT = 2048


def _load_expert_weights(e, srcs, dsts, stage, sem, phase="all"):
    jobs, n_a = [], 0
    for lo, hi in ((0, W_SPLIT), (W_SPLIT, D_FF)):
        for src, dst in zip(srcs, dsts):
            n_r, n_c = dst.shape
            r_lo, r_hi, c_lo, c_hi = (0, n_r, lo, hi) if n_c == D_FF else (lo, hi, 0, n_c)
            for r0 in range(r_lo, r_hi, W_ROWS):
                for c0 in range(c_lo, c_hi, W_COLS):
                    jobs.append((src.at[e, pl.ds(r0, W_ROWS), pl.ds(c0, W_COLS)],
                                 dst.at[r0:r0 + W_ROWS, c0:c0 + W_COLS]))
        n_a = n_a or len(jobs)
    copies = [pltpu.make_async_copy(src, stage.at[i % W_SLOTS], sem.at[i % W_SLOTS])
              for i, (src, _) in enumerate(jobs)]
    ahead = W_SLOTS - 1
    if phase in ("all", "start"):
        for cp in copies[:ahead]:
            cp.start()
    if phase == "start":
        return
    first, last = {"finish_a": (0, n_a), "finish_b": (n_a, len(jobs))}.get(phase, (0, len(jobs)))
    for i in range(first, last):
        copies[i].wait()
        jobs[i][1][...] = stage[i % W_SLOTS].astype(BF16)
        if i + ahead < len(copies):
            copies[i + ahead].start()


def _weight_scratch():
    assert D_MODEL % W_ROWS == 0 and D_FF % W_ROWS == 0
    assert D_MODEL % W_COLS == 0 and D_FF % W_COLS == 0
    return [pltpu.VMEM((D_MODEL, D_FF), BF16), pltpu.VMEM((D_MODEL, D_FF), BF16),
            pltpu.VMEM((D_FF, D_MODEL), BF16), pltpu.VMEM((W_SLOTS, W_ROWS, W_COLS), F32),
            pltpu.SemaphoreType.DMA((W_SLOTS,))]


def _experts_kernel(tblk, texp, tn, tnew, tpre, xs_ref, wg_hbm, wu_hbm, wd_hbm, o_ref,
                    hb, wg_ref, wu_ref, wd_ref, stage, sem, *, tf):
    t = pl.program_id(0)
    n = tn[t]
    tm = hb.shape[0]
    load = functools.partial(_load_expert_weights, srcs=(wg_hbm, wu_hbm, wd_hbm),
                             dsts=(wg_ref, wu_ref, wd_ref), stage=stage, sem=sem)

    fresh = tnew[t] == 1
    prefetch = lambda: pl.when(tpre[t] >= 0)(lambda: load(tpre[t], phase="start"))
    pl.when(t == 0)(lambda: load(texp[0], phase="start"))
    pl.when(jnp.logical_not(fresh))(prefetch)

    def run(rows, chunks, finish=None):
        valid = lax.broadcasted_iota(jnp.int32, (rows, LANES), 0) < n
        for c in range(SLAB):
            xc = xs_ref[pl.ds(c, rows, stride=SLAB), :]
            hb[0:rows, c * LANES:(c + 1) * LANES] = jnp.where(valid, xc, 0.0).astype(BF16)
        h_b = hb[0:rows, :]
        y = None
        for i, (c0, c1) in enumerate(chunks):
            if finish and finish[i]:
                load(texp[t], phase=finish[i])
            part = _swiglu_part(h_b, wg_ref.at[:, c0:c1], wu_ref.at[:, c0:c1], wd_ref.at[c0:c1, :])
            y = part if y is None else y + part
        if finish:
            prefetch()
        _to_slab(o_ref, y, rows)

    even = [(c0, c0 + tf) for c0 in range(0, D_FF, tf)]
    pl.when(fresh)(lambda: run(tm, [(0, W_SPLIT // 2), (W_SPLIT // 2, W_SPLIT), (W_SPLIT, D_FF)],
                               ("finish_a", None, "finish_b")))
    pl.when(jnp.logical_and(jnp.logical_not(fresh), n > tm // 2))(lambda: run(tm, even))
    pl.when(jnp.logical_and(jnp.logical_not(fresh), jnp.logical_and(n > 0, n <= tm // 2)))(
        lambda: run(tm // 2, even))


def _experts(xs_slab, tblk, texp, tn, tnew, tpre, wg, wu, wd, *, tm, tf):
    n_tiles = tblk.shape[0]
    assert D_FF % tf == 0 and tm % (2 * SUBLANES) == 0
    tile = lambda t, tblk, texp, tn, tnew, tpre: (tblk[t], 0)
    hbm = pl.BlockSpec(memory_space=pl.ANY)
    grid_spec = pltpu.PrefetchScalarGridSpec(
        num_scalar_prefetch=5,
        grid=(n_tiles,),
        in_specs=[pl.BlockSpec((tm * SLAB, LANES), tile), hbm, hbm, hbm],
        out_specs=pl.BlockSpec((tm * SLAB, LANES), tile),
        scratch_shapes=[pltpu.VMEM((tm, D_MODEL), BF16)] + _weight_scratch())
    return pl.pallas_call(
        functools.partial(_experts_kernel, tf=tf),
        grid_spec=grid_spec,
        out_shape=jax.ShapeDtypeStruct(xs_slab.shape, F32),
        compiler_params=pltpu.CompilerParams(dimension_semantics=("arbitrary",),
                                             vmem_limit_bytes=VMEM_LIMIT),
        name="experts",
    )(tblk, texp, tn, tnew, tpre, xs_slab, wg, wu, wd)


def _combine_kernel(idx1_ref, idx2_ref, idx1_next_ref, idx2_next_ref, x_ref, mf_ref, gf_ref,
                    ys_ref, op_ref, os_ref, buf, ysum, sem, *, rows, n_prompt_tiles):
    i = pl.program_id(0)
    n_i = pl.num_programs(0)
    slot = i % 2

    def issue(refs, s):
        def body(r, c):
            for k, ref in enumerate(refs):
                _row_copy(ys_ref, ref[0, 0, r], buf.at[s], k * rows + r,
                          sem.at[s]).start(priority=k)
            return c
        lax.fori_loop(0, rows, body, 0, unroll=4)

    def finish(s):
        _wait_rows(ys_ref, buf.at[s], 2 * rows, sem.at[s])
        p1 = mf_ref[:, 0:1]
        p2 = mf_ref[:, 1:2]
        for c in range(SLAB):
            cs = slice(c * LANES, (c + 1) * LANES)
            y1 = buf[s, pl.ds(c, rows, stride=SLAB), :]
            y2 = buf[s, pl.ds(rows * SLAB + c, rows, stride=SLAB), :]
            ysum[:, cs] = x_ref[:, cs] + p1 * y1 + p2 * y2
        res = _rms(ysum[...], gf_ref[...])

        @pl.when(i < n_prompt_tiles)
        def _():
            op_ref[...] = res

        @pl.when(i >= n_prompt_tiles)
        def _():
            os_ref[...] = res

    for s in range(2):
        @pl.when(slot == s)
        def _():
            @pl.when(i == 0)
            def _():
                issue((idx1_ref, idx2_ref), s)

            @pl.when(i + 1 < n_i)
            def _():
                issue((idx1_next_ref, idx2_next_ref), 1 - s)

            finish(s)


def _combine(x, mf, g_final, y_slab, dest1, dest2, *, rows, m_prompt):
    m = x.shape[0]
    n_blk = m // rows
    assert m == n_blk * rows and m_prompt % rows == 0
    n_p = m_prompt // rows
    blocked = lambda d: d.reshape(n_blk, 1, rows)
    this = lambda i: (i, 0, 0)
    ahead = lambda i: (jnp.minimum(i + 1, n_blk - 1), 0, 0)
    return pl.pallas_call(
        functools.partial(_combine_kernel, rows=rows, n_prompt_tiles=n_p),
        grid=(n_blk,),
        in_specs=[_idx_spec(rows, this), _idx_spec(rows, this),
                  _idx_spec(rows, ahead), _idx_spec(rows, ahead),
                  pl.BlockSpec((rows, D_MODEL), lambda i: (i, 0)),
                  pl.BlockSpec((rows, LANES), lambda i: (i, 0)),
                  pl.BlockSpec((1, D_MODEL), lambda i: (0, 0)),
                  pl.BlockSpec(memory_space=pl.ANY)],
        out_specs=[pl.BlockSpec((rows, D_MODEL), lambda i: (jnp.minimum(i, n_p - 1), 0)),
                   pl.BlockSpec((rows, D_MODEL), lambda i: (jnp.maximum(i - n_p, 0), 0))],
        out_shape=[jax.ShapeDtypeStruct((m_prompt, D_MODEL), F32),
                   jax.ShapeDtypeStruct((m - m_prompt, D_MODEL), F32)],
        scratch_shapes=[pltpu.VMEM((2, 2 * rows * SLAB, LANES), F32),
                        pltpu.VMEM((rows, D_MODEL), F32),
                        pltpu.SemaphoreType.DMA((2,))],
        compiler_params=pltpu.CompilerParams(dimension_semantics=("arbitrary",),
                                             vmem_limit_bytes=VMEM_LIMIT),
        name="combine",
    )(blocked(dest1), blocked(dest2), blocked(dest1), blocked(dest2), x, mf, g_final, y_slab)


def _expert_tiles(counts, *, tm, cap, n_tiles):
    tiles = (counts + tm - 1) // tm
    ends = jnp.cumsum(tiles)
    t = jnp.arange(n_tiles, dtype=jnp.int32)
    tc = jnp.minimum(t, ends[-1] - 1)
    e = jnp.sum((tc[:, None] >= ends[None, :]).astype(jnp.int32), axis=1)
    k = tc - (ends - tiles)[e]
    tblk = e * (cap // tm) + k
    tn = jnp.where(t < ends[-1], jnp.minimum(counts[e] - k * tm, tm), 0)
    tnew = jnp.logical_and(t < ends[-1], k == 0)
    i32 = lambda v: v.astype(jnp.int32)
    nxt = jnp.minimum(t + 1, n_tiles - 1)
    tpre = jnp.where(jnp.logical_and(t + 1 < n_tiles, tnew[nxt]), e[nxt], -1)
    return i32(tblk), i32(e), i32(tn), i32(tnew), i32(tpre)


def _moe_final(x, g_ffn, wr, br, wg, wu, wd, g_final, *, m_prompt):
    m = x.shape[0]
    tm, tf, out_rows, in_rows = MLP_ROWS, MLP_COLS, DISPATCH_ROWS, COMBINE_ROWS
    cap = -(-m // tm) * tm
    h_slab, mi, mf, cnt = _route(x, g_ffn, wr, br, tm=ROUTE_ROWS, cap=cap)
    dest1, dest2 = mi[0], mi[1]
    counts = cnt[0, :N_EXPERTS].astype(jnp.int32)
    n_tiles = 2 * m // tm + N_EXPERTS
    tables = _expert_tiles(counts, tm=tm, cap=cap, n_tiles=n_tiles)
    xs_slab = _dispatch(h_slab, dest1, dest2, rows=out_rows, cap_rows=N_EXPERTS * cap)
    y_slab = _experts(xs_slab, *tables, wg, wu, wd, tm=tm, tf=tf)
    return _combine(x, mf, g_final, y_slab, dest1, dest2, rows=in_rows, m_prompt=m_prompt)


def _row(v):
    return v.reshape(1, -1)


def kernel(x_prompt, x_sample, state_conv_a, state_conv_b, norm_mix_g, w_in, conv_a_w, conv_a_b, ln_a_g, ln_a_b, conv_b_w, ln_c_g, ln_c_b, w_spatial, b_spatial, w_proj_a, w_proj_b, w_proj_c, w_out, norm_ffn_g, ffn_w_gate, ffn_w_up, ffn_w_down, w_router, b_router, moe_w_gate, moe_w_up, moe_w_down, norm_final_g):
    depth = w_in.shape[0]
    n_p, t_p, _ = x_prompt.shape
    n_s, t_s, _ = x_sample.shape
    seqs_per_chunk = CHUNK // t_s

    zeros_a = jnp.zeros((1, n_p, A_WIDTH - 1, D_BR), F32)
    zeros_b = jnp.zeros((1, n_p, B_WIDTH - 1, D_BR), F32)
    matrices = dict(w_in=w_in.astype(BF16), w_proj_a=w_proj_a.astype(BF16),
                    w_proj_b=w_proj_b.astype(BF16), w_proj_c=w_proj_c.astype(BF16),
                    w_out=w_out.astype(BF16))

    m_p, m_s = n_p * t_p, n_s * t_s
    m = m_p + m_s
    x_p, row_p = x_prompt.reshape(m_p, D_MODEL), 0
    x_s, row_s = x_sample.reshape(m_s, D_MODEL), 0
    states_p = states_s = None
    for l in range(depth):
        bsp_p = jnp.repeat(b_spatial[l].T, LANES, axis=1)
        eye = jnp.eye(seqs_per_chunk, dtype=F32)
        wsp_s = jnp.einsum("ab,hts->hatbs", eye, w_spatial[l][:, :t_s, :t_s]).reshape(
            C_HEADS, CHUNK, CHUNK)
        bsp_s = jnp.tile(bsp_p[:t_s], (seqs_per_chunk, 1))
        w = dict(matrices, g=_row(norm_mix_g[l]), conv_a_w=conv_a_w[l],
                 conv_a_b=_row(conv_a_b[l]), ln_a_g=_row(ln_a_g[l]), ln_a_b=_row(ln_a_b[l]),
                 conv_b_w=conv_b_w[l], ln_c_g=_row(ln_c_g[l]), ln_c_b=_row(ln_c_b[l]))
        x, *states_p = _mixer(x_p, zeros_a, zeros_b, dict(w, wsp=w_spatial[l], bsp=bsp_p),
                              t=t_p, nb=1, tt=MIXER_ROWS, in_row=row_p, out_row=0, out_rows=m,
                              layer=l, depth=depth, past_layer=0, state_bufs=states_p)
        x, *states_s = _mixer(x_s, state_conv_a, state_conv_b,
                              dict(w, wsp=wsp_s, bsp=bsp_s), t=t_s, nb=SAMPLE_SEQS, tt=t_s,
                              in_row=row_s, out_row=m_p, out_rows=m, layer=l, depth=depth,
                              past_layer=l, out_buf=x, state_bufs=states_s)

        i = l // 2
        g_ffn = _row(norm_ffn_g[l])
        if l % 2 == 0:
            x = _ffn(x, g_ffn, ffn_w_gate, ffn_w_up, ffn_w_down, which=i,
                     tm=MLP_ROWS, tf=MLP_COLS)
            x_p, row_p, x_s, row_s = x, 0, x, m_p
        else:
            assert l == depth - 1, "the routed layer applies the final norm"
            wr = jnp.pad(w_router[i], ((0, 0), (0, LANES - N_EXPERTS)))
            br = jnp.pad(_row(b_router[i]), ((0, 0), (0, LANES - N_EXPERTS)))
            y_p, y_s = _moe_final(x, g_ffn, wr, br, moe_w_gate[i], moe_w_up[i], moe_w_down[i],
                                  _row(norm_final_g), m_prompt=m_p)
    y_p = y_p.reshape(n_p, t_p, D_MODEL)
    y_s = y_s.reshape(n_s, t_s, D_MODEL)

    (a_p, b_p, v_p), (a_s, b_s, v_s) = states_p, states_s
    return (y_p, y_s, a_p, a_s, b_p, b_s, v_p, v_s)
```
